```python
import jax, jax.numpy as jnp
from jax import lax
import numpy as np

D_MODEL = 2048
BATCH = 2
SEQ = 8192
DEPTH = 1
DEC_BATCH = 4
DEC_SEQ = 2048
PAST_LEN = 128

GRID_W = 64
D_RNN = D_MODEL
N_LRU_BLOCKS = 16
LRU_BLOCK = D_RNN // N_LRU_BLOCKS
CONV_W = 4
LRU_C = 8.0
N_HEADS = 16
HEAD_DIM = D_MODEL // N_HEADS
D_ATTN = N_HEADS * HEAD_DIM
WIN_R = 8
WIN_C = 16
COL_BLK = 16
KEY_BLK_W = COL_BLK + WIN_C
N_COL_BLK = GRID_W // COL_BLK
N_GROUPS = 4
EXPERTS_PER_GROUP = 8
N_EXPERTS = N_GROUPS * EXPERTS_PER_GROUP
TOP_K = 2
D_EXPERT = D_MODEL // 2
MOE_BLOCK = 128
EPS = 1e-6
SPLIT_SIZES = (D_RNN, D_RNN, D_ATTN, D_ATTN, D_ATTN, D_MODEL, D_MODEL)
D_IN = 2 * D_RNN + 3 * D_ATTN + 2 * D_MODEL

kernel_name = "hawk_natten_hmoe_encoder"


def rmsnorm(x, g):
    xf = x.astype(jnp.float32)
    y = xf * lax.rsqrt(jnp.mean(xf * xf, axis=-1, keepdims=True) + EPS)
    return (y * g.astype(jnp.float32)).astype(x.dtype)


def centred_conv(x, w, b):
    S = x.shape[1]
    left = CONV_W // 2
    xp = jnp.pad(x, ((0, 0), (left, CONV_W - 1 - left), (0, 0)))
    y = b
    for tap in range(CONV_W):
        y = y + xp[:, tap:tap + S] * w[tap]
    return y


def rg_lru(x, w_a, b_a, w_x, b_x, lam, reverse):
    B, S, _ = x.shape
    xb = x.reshape(B, S, N_LRU_BLOCKS, LRU_BLOCK)
    r = jax.nn.sigmoid(jnp.einsum('bshi,hij->bshj', xb, w_a).reshape(B, S, D_RNN).astype(jnp.float32) + b_a.astype(jnp.float32))
    i = jax.nn.sigmoid(jnp.einsum('bshi,hij->bshj', xb, w_x).reshape(B, S, D_RNN).astype(jnp.float32) + b_x.astype(jnp.float32))
    log_a = -LRU_C * r * jax.nn.softplus(-lam.astype(jnp.float32))
    a = jnp.exp(log_a)
    u = jnp.sqrt(-jnp.expm1(2.0 * log_a)) * (i * x.astype(jnp.float32))

    def combine(left, right):
        a1, b1 = left
        a2, b2 = right
        return a1 * a2, a2 * b1 + b2

    _, h = lax.associative_scan(combine, (a, u), axis=1, reverse=reverse)
    return h


def _column_tables():
    j = np.arange(N_COL_BLK)
    kb = np.clip(j * COL_BLK - WIN_C // 2, 0, GRID_W - KEY_BLK_W)
    col_idx = kb[:, None] + np.arange(KEY_BLK_W)[None, :]
    qc = j[:, None] * COL_BLK + np.arange(COL_BLK)[None, :]
    cs = np.clip(qc - WIN_C // 2, 0, GRID_W - WIN_C)
    kcol = col_idx[:, None, :]
    valid = (kcol >= cs[:, :, None]) & (kcol < cs[:, :, None] + WIN_C)
    dc_idx = np.clip(kcol - qc[:, :, None] + WIN_C - 1, 0, 2 * WIN_C - 2)
    return col_idx, dc_idx, valid


def neighbourhood_attention(q, k, v, rpb):
    B, S = q.shape[0], q.shape[1]
    rows = S // GRID_W
    wr = min(WIN_R, rows)
    qg = q.reshape(B, rows, GRID_W, N_HEADS, HEAD_DIM) * (HEAD_DIM ** -0.5)
    kg = k.reshape(B, rows, GRID_W, N_HEADS, HEAD_DIM)
    vg = v.reshape(B, rows, GRID_W, N_HEADS, HEAD_DIM)
    col_idx, dc_idx, col_valid = _column_tables()
    mask = col_valid[None, :, :, None, None, :]

    def one_row(r):
        rs = jnp.clip(r - wr // 2, 0, rows - wr)
        k_band = lax.dynamic_slice_in_dim(kg, rs, wr, axis=1)
        v_band = lax.dynamic_slice_in_dim(vg, rs, wr, axis=1)
        k_blk = k_band[:, :, col_idx]
        v_blk = v_band[:, :, col_idx]
        q_row = lax.dynamic_index_in_dim(qg, r, axis=1, keepdims=False)
        q_row = q_row.reshape(B, N_COL_BLK, COL_BLK, N_HEADS, HEAD_DIM)
        s = jnp.einsum('bjqhd,brjkhd->bjqhrk', q_row, k_blk).astype(jnp.float32)
        row_off = rs + jnp.arange(wr) - r + (WIN_R - 1)
        rpb_rows = jnp.take(rpb, row_off, axis=1)
        bias = rpb_rows[:, :, dc_idx].transpose(2, 3, 0, 1, 4)
        s = jnp.where(mask, s + bias.astype(jnp.float32), -1e30)
        p = jax.nn.softmax(s.reshape(B, N_COL_BLK, COL_BLK, N_HEADS, wr * KEY_BLK_W), axis=-1)
        p = p.reshape(s.shape).astype(v.dtype)
        o = jnp.einsum('bjqhrk,brjkhd->bjqhd', p, v_blk)
        return o.reshape(B, GRID_W, N_HEADS, HEAD_DIM)

    out = lax.map(one_row, jnp.arange(rows))
    return out.transpose(1, 0, 2, 3, 4).reshape(B, S, D_ATTN)


def grouped_experts(xt, expert, gate_w, w_gate, w_up, w_down):
    N, D = xt.shape
    A = N * TOP_K
    flat_e = expert.reshape(A)
    order = jnp.argsort(flat_e)
    sorted_e = flat_e[order]
    tok = order // TOP_K
    w_sorted = gate_w.reshape(A)[order]
    counts = jnp.bincount(flat_e, length=N_EXPERTS)
    padded = (counts + MOE_BLOCK - 1) // MOE_BLOCK * MOE_BLOCK
    pad_end = jnp.cumsum(padded)
    pad_start = pad_end - padded
    start = jnp.cumsum(counts) - counts
    dest = pad_start[sorted_e] + jnp.arange(A) - start[sorted_e]
    n_blocks = -(-A // MOE_BLOCK) + N_EXPERTS
    P = n_blocks * MOE_BLOCK
    buf_tok = jnp.zeros((P,), jnp.int32).at[dest].set(tok)
    buf_w = jnp.zeros((P,), jnp.float32).at[dest].set(w_sorted)
    block_e = jnp.minimum(jnp.searchsorted(pad_end, jnp.arange(n_blocks) * MOE_BLOCK, side='right'), N_EXPERTS - 1)
    xb = xt[buf_tok].reshape(n_blocks, MOE_BLOCK, D)

    def run(args):
        xblk, e = args
        hid = jax.nn.silu(xblk @ w_gate[e]) * (xblk @ w_up[e])
        return hid @ w_down[e]

    yb = lax.map(run, (xb, block_e)).reshape(P, D)
    return jnp.zeros_like(xt).at[buf_tok].add(yb * buf_w[:, None].astype(yb.dtype))


def hier_moe(x, w_rg, b_rg, w_re, b_re, w_gate, w_up, w_down):
    B, S, D = x.shape
    N = B * S
    xt = x.reshape(N, D)
    p_group = jax.nn.softmax((xt @ w_rg).astype(jnp.float32) + b_rg.astype(jnp.float32), axis=-1)
    g_val, g_idx = lax.top_k(p_group, 1)
    e_logits = ((xt @ w_re).astype(jnp.float32) + b_re.astype(jnp.float32)).reshape(N, N_GROUPS, EXPERTS_PER_GROUP)
    sel = jnp.broadcast_to(g_idx[:, :, None], (N, 1, EXPERTS_PER_GROUP))
    e_logits = jnp.take_along_axis(e_logits, sel, axis=1)[:, 0]
    p_exp = jax.nn.softmax(e_logits, axis=-1)
    e_val, e_idx = lax.top_k(p_exp, TOP_K)
    e_val = e_val / jnp.sum(e_val, axis=-1, keepdims=True)
    gate_w = g_val * e_val
    expert = g_idx * EXPERTS_PER_GROUP + e_idx
    y = grouped_experts(xt, expert, gate_w, w_gate, w_up, w_down)
    return y.reshape(B, S, D)


def mixer(xn, w_in, conv_w, conv_b, lru_w_a, lru_b_a, lru_w_x, lru_b_x, lru_lambda, rpb, w_lru_up, w_attn_up, b_merge, w_o):
    B, S, _ = xn.shape
    proj = xn @ w_in
    splits = np.cumsum(SPLIT_SIZES)[:-1].tolist()
    lru_in, lru_gate, q, k, v, g_lru, g_attn = jnp.split(proj, splits, axis=-1)
    xc = centred_conv(lru_in, conv_w, conv_b)
    h = (rg_lru(xc, lru_w_a[0], lru_b_a[0], lru_w_x[0], lru_b_x[0], lru_lambda[0], False)
         + rg_lru(xc, lru_w_a[1], lru_b_a[1], lru_w_x[1], lru_b_x[1], lru_lambda[1], True))
    y_lru = (jax.nn.gelu(lru_gate.astype(jnp.float32)) * h).astype(xn.dtype) @ w_lru_up
    shp = (B, S, N_HEADS, HEAD_DIM)
    y_att = neighbourhood_attention(q.reshape(shp), k.reshape(shp), v.reshape(shp), rpb) @ w_attn_up
    s_lru = jax.nn.sigmoid(g_lru + b_merge[0])
    s_att = jax.nn.sigmoid(g_attn + b_merge[1])
    return (s_lru * y_lru + s_att * y_att) @ w_o


def encoder(x, norm_mix_g, w_in, conv_w, conv_b, lru_w_a, lru_b_a, lru_w_x, lru_b_x, lru_lambda, rpb,
            w_lru_up, w_attn_up, b_merge, w_o, norm_ffn_g, w_router_group, b_router_group,
            w_router_expert, b_router_expert, w_exp_gate, w_exp_up, w_exp_down, norm_final_g):
    h = x
    for l in range(DEPTH):
        h = h + mixer(rmsnorm(h, norm_mix_g[l]), w_in[l], conv_w[l], conv_b[l], lru_w_a[l], lru_b_a[l],
                      lru_w_x[l], lru_b_x[l], lru_lambda[l], rpb[l], w_lru_up[l], w_attn_up[l], b_merge[l], w_o[l])
        h = h + hier_moe(rmsnorm(h, norm_ffn_g[l]), w_router_group[l], b_router_group[l], w_router_expert[l],
                         b_router_expert[l], w_exp_gate[l], w_exp_up[l], w_exp_down[l])
    return rmsnorm(h, norm_final_g)


def setup_inputs(seed: int = 0) -> dict:
    key = jax.random.key(seed)
    ks = jax.random.split(key, 26)
    f32 = jnp.float32

    def nrm(k, shape, scale):
        return jax.random.normal(k, shape, f32) * scale

    a0 = jax.random.uniform(ks[10], (DEPTH, 2, D_RNN), f32, 0.9, 0.999)
    s0 = a0 ** (1.0 / LRU_C)
    lru_lambda = jnp.log(s0) - jnp.log1p(-s0)
    return {
        "x_prompt": nrm(ks[0], (BATCH, SEQ, D_MODEL), 1.0),
        "x_sample": nrm(ks[1], (DEC_BATCH, DEC_SEQ, D_MODEL), 1.0),
        "norm_mix_g": 1.0 + nrm(ks[2], (DEPTH, D_MODEL), 0.02),
        "w_in": nrm(ks[3], (DEPTH, D_MODEL, D_IN), D_MODEL ** -0.5),
        "conv_w": nrm(ks[4], (DEPTH, CONV_W, D_RNN), CONV_W ** -0.5),
        "conv_b": nrm(ks[5], (DEPTH, D_RNN), 0.01),
        "lru_w_a": nrm(ks[6], (DEPTH, 2, N_LRU_BLOCKS, LRU_BLOCK, LRU_BLOCK), LRU_BLOCK ** -0.5),
        "lru_b_a": nrm(ks[7], (DEPTH, 2, D_RNN), 0.01),
        "lru_w_x": nrm(ks[8], (DEPTH, 2, N_LRU_BLOCKS, LRU_BLOCK, LRU_BLOCK), LRU_BLOCK ** -0.5),
        "lru_b_x": nrm(ks[9], (DEPTH, 2, D_RNN), 0.01),
        "lru_lambda": lru_lambda,
        "rpb": nrm(ks[11], (DEPTH, N_HEADS, 2 * WIN_R - 1, 2 * WIN_C - 1), 0.1),
        "w_lru_up": nrm(ks[12], (DEPTH, D_RNN, D_MODEL), D_RNN ** -0.5),
        "w_attn_up": nrm(ks[13], (DEPTH, D_ATTN, D_MODEL), D_ATTN ** -0.5),
        "b_merge": nrm(ks[14], (DEPTH, 2, D_MODEL), 0.01),
        "w_o": nrm(ks[15], (DEPTH, D_MODEL, D_MODEL), D_MODEL ** -0.5),
        "norm_ffn_g": 1.0 + nrm(ks[16], (DEPTH, D_MODEL), 0.02),
        "w_router_group": nrm(ks[17], (DEPTH, D_MODEL, N_GROUPS), D_MODEL ** -0.5),
        "b_router_group": nrm(ks[18], (DEPTH, N_GROUPS), 0.01),
        "w_router_expert": nrm(ks[19], (DEPTH, D_MODEL, N_EXPERTS), D_MODEL ** -0.5),
        "b_router_expert": nrm(ks[20], (DEPTH, N_EXPERTS), 0.01),
        "w_exp_gate": nrm(ks[21], (DEPTH, N_EXPERTS, D_MODEL, D_EXPERT), D_MODEL ** -0.5),
        "w_exp_up": nrm(ks[22], (DEPTH, N_EXPERTS, D_MODEL, D_EXPERT), D_MODEL ** -0.5),
        "w_exp_down": nrm(ks[23], (DEPTH, N_EXPERTS, D_EXPERT, D_MODEL), D_EXPERT ** -0.5),
        "norm_final_g": 1.0 + nrm(ks[24], (D_MODEL,), 0.02),
    }


def reference(x_prompt, x_sample, norm_mix_g, w_in, conv_w, conv_b, lru_w_a, lru_b_a, lru_w_x, lru_b_x,
              lru_lambda, rpb, w_lru_up, w_attn_up, b_merge, w_o, norm_ffn_g, w_router_group, b_router_group,
              w_router_expert, b_router_expert, w_exp_gate, w_exp_up, w_exp_down, norm_final_g):
    weights = (norm_mix_g, w_in, conv_w, conv_b, lru_w_a, lru_b_a, lru_w_x, lru_b_x, lru_lambda, rpb,
               w_lru_up, w_attn_up, b_merge, w_o, norm_ffn_g, w_router_group, b_router_group,
               w_router_expert, b_router_expert, w_exp_gate, w_exp_up, w_exp_down, norm_final_g)
    y_prompt = encoder(x_prompt, *weights)
    y_sample = encoder(x_sample, *weights)
    return (y_prompt, y_sample)
```

```python
import functools

import numpy as np
import jax
import jax.numpy as jnp
from jax import lax
from jax.experimental import pallas as pl
from jax.experimental.pallas import tpu as pltpu

GRID_W = 64
WIN_R = 8
WIN_C = 16
CONV_W = 4
LRU_C = 8.0
LRU_BLOCK = 128
HEAD_DIM = 128
N_GROUPS = 4
EXPERTS_PER_GROUP = 8
N_EXPERTS = N_GROUPS * EXPERTS_PER_GROUP
TOP_K = 2
EPS = 1e-6
MASK_VALUE = -1e30

MXU_DTYPE = jnp.bfloat16

LANES = 128
SUBLANES = 8
VMEM_LIMIT_BYTES = 56 * 1024 * 1024

PROJ_TM = 1024
PROJ_TN = 1024
MERGE_TM = 1024
MERGE_TN = 512
LRU_TC = 2048
ATT_ROWS = 8
ROUTER_TM = 512
MOE_BM = 256
COMBINE_TM = 256


def _cparams(*sem):
    return pltpu.CompilerParams(dimension_semantics=sem, vmem_limit_bytes=VMEM_LIMIT_BYTES)


def _norm_proj_kernel(x_ref, g_ref, w_ref, o_ref, xn_ref):
    @pl.when(pl.program_id(1) == 0)
    def _():
        x = x_ref[...]
        ms = jnp.mean(x * x, axis=-1, keepdims=True)
        xn_ref[...] = ((x * lax.rsqrt(ms + EPS)) * g_ref[...]).astype(MXU_DTYPE)

    o_ref[...] = jnp.dot(xn_ref[...], w_ref[...], preferred_element_type=jnp.float32)


def _norm_proj(x2, g, w_bf16):
    n, d = x2.shape
    d_out = w_bf16.shape[1]
    tm = min(PROJ_TM, n)
    tn = min(PROJ_TN, d_out)
    return pl.pallas_call(
        _norm_proj_kernel,
        grid=(n // tm, d_out // tn),
        in_specs=[
            pl.BlockSpec((tm, d), lambda i, j: (i, 0)),
            pl.BlockSpec((1, d), lambda i, j: (0, 0)),
            pl.BlockSpec((d, tn), lambda i, j: (0, j)),
        ],
        out_specs=pl.BlockSpec((tm, tn), lambda i, j: (i, j)),
        out_shape=jax.ShapeDtypeStruct((n, d_out), jnp.float32),
        scratch_shapes=[pltpu.VMEM((tm, d), MXU_DTYPE)],
        compiler_params=_cparams("parallel", "arbitrary"),
    )(x2, g.reshape(1, d), w_bf16)


def _softplus(y):
    return jnp.maximum(y, 0.0) + jnp.log1p(jnp.exp(-jnp.abs(y)))


def _lru_kernel(*refs, reverse, n_chunks, tc):
    if reverse:
        (x_ref, xp_ref, xn_ref, cw_ref, cb_ref, wg_ref, ba_ref, bx_ref, lam_ref,
         hf_ref, gate_ref, o_ref, xpad_ref, a_ref, u_ref, carry_ref) = refs
    else:
        (x_ref, xp_ref, xn_ref, cw_ref, cb_ref, wg_ref, ba_ref, bx_ref, lam_ref,
         o_ref, xpad_ref, a_ref, u_ref, carry_ref) = refs
    t = pl.program_id(2)
    chunk = (n_chunks - 1 - t) if reverse else t

    @pl.when(t == 0)
    def _():
        carry_ref[...] = jnp.zeros_like(carry_ref)

    prev = jnp.where(chunk == 0, 0.0, xp_ref[...])
    nxt = jnp.where(chunk == n_chunks - 1, 0.0, xn_ref[...])
    xpad_ref[pl.ds(0, SUBLANES), :] = prev
    xpad_ref[pl.ds(SUBLANES, tc), :] = x_ref[...]
    xpad_ref[pl.ds(SUBLANES + tc, SUBLANES), :] = nxt
    left = CONV_W // 2
    xc = cb_ref[...]
    for tap in range(CONV_W):
        xc = xc + xpad_ref[pl.ds(SUBLANES - left + tap, tc), :] * cw_ref[pl.ds(tap, 1), :]

    z = jnp.dot(xc.astype(MXU_DTYPE), wg_ref[...], preferred_element_type=jnp.float32)
    r = jax.nn.sigmoid(z[:, :LRU_BLOCK] + ba_ref[...])
    i = jax.nn.sigmoid(z[:, LRU_BLOCK:] + bx_ref[...])
    log_a = (-LRU_C * r) * _softplus(-lam_ref[...])
    a = jnp.exp(log_a)
    a_ref[...] = a
    u_ref[...] = jnp.sqrt(jnp.tanh(-log_a) * (1.0 + a * a)) * (i * xc)

    row = lax.broadcasted_iota(jnp.int32, (SUBLANES, LANES), 0)
    n_vregs = tc // SUBLANES

    def body(j, c):
        jj = (n_vregs - 1 - j) if reverse else j
        start = pl.multiple_of(jj * SUBLANES, SUBLANES)
        a = a_ref[pl.ds(start, SUBLANES), :]
        b = u_ref[pl.ds(start, SUBLANES), :]
        for s in (1, 2, 4):
            if reverse:
                keep = row < SUBLANES - s
                shift = SUBLANES - s
            else:
                keep = row >= s
                shift = s
            a_sh = jnp.where(keep, pltpu.roll(a, shift, 0), 1.0)
            b_sh = jnp.where(keep, pltpu.roll(b, shift, 0), 0.0)
            b = a * b_sh + b
            a = a * a_sh
        h = a * c + b
        if reverse:
            res = jax.nn.gelu(gate_ref[pl.ds(start, SUBLANES), :]) * (hf_ref[pl.ds(start, SUBLANES), :] + h)
            o_ref[pl.ds(start, SUBLANES), :] = res.astype(o_ref.dtype)
            return h[0:1, :]
        o_ref[pl.ds(start, SUBLANES), :] = h
        return h[SUBLANES - 1:SUBLANES, :]

    carry_ref[...] = lax.fori_loop(0, n_vregs, body, carry_ref[...])


def _lru_direction(proj3, conv_w, conv_b, wg, ba, bx, lam, hf, *, reverse, d_rnn):
    b, s, _ = proj3.shape
    n_cb = d_rnn // LRU_BLOCK
    tc = min(LRU_TC, s)
    n_chunks = s // tc
    tc8 = tc // SUBLANES
    s8 = s // SUBLANES

    def chunk_of(t):
        return (n_chunks - 1 - t) if reverse else t

    vec_spec = pl.BlockSpec((1, LRU_BLOCK), lambda bi, c, t: (0, c))
    in_specs = [
        pl.BlockSpec((None, tc, LRU_BLOCK), lambda bi, c, t: (bi, chunk_of(t), c)),
        pl.BlockSpec((None, SUBLANES, LRU_BLOCK),
                     lambda bi, c, t: (bi, jnp.maximum(chunk_of(t) * tc8 - 1, 0), c)),
        pl.BlockSpec((None, SUBLANES, LRU_BLOCK),
                     lambda bi, c, t: (bi, jnp.minimum((chunk_of(t) + 1) * tc8, s8 - 1), c)),
        pl.BlockSpec((CONV_W, LRU_BLOCK), lambda bi, c, t: (0, c)),
        vec_spec,
        pl.BlockSpec((None, LRU_BLOCK, 2 * LRU_BLOCK), lambda bi, c, t: (c, 0, 0)),
        vec_spec, vec_spec, vec_spec,
    ]
    args = [proj3, proj3, proj3, conv_w, conv_b.reshape(1, d_rnn), wg,
            ba.reshape(1, d_rnn), bx.reshape(1, d_rnn), lam.reshape(1, d_rnn)]
    if reverse:
        in_specs += [
            pl.BlockSpec((None, tc, LRU_BLOCK), lambda bi, c, t: (bi, chunk_of(t), c)),
            pl.BlockSpec((None, tc, LRU_BLOCK), lambda bi, c, t: (bi, chunk_of(t), n_cb + c)),
        ]
        args += [hf, proj3]
        out_dtype = MXU_DTYPE
    else:
        out_dtype = jnp.float32
    return pl.pallas_call(
        functools.partial(_lru_kernel, reverse=reverse, n_chunks=n_chunks, tc=tc),
        grid=(b, n_cb, n_chunks),
        in_specs=in_specs,
        out_specs=pl.BlockSpec((None, tc, LRU_BLOCK), lambda bi, c, t: (bi, chunk_of(t), c)),
        out_shape=jax.ShapeDtypeStruct((b, s, d_rnn), out_dtype),
        scratch_shapes=[
            pltpu.VMEM((tc + 2 * SUBLANES, LRU_BLOCK), jnp.float32),
            pltpu.VMEM((tc, LRU_BLOCK), jnp.float32),
            pltpu.VMEM((tc, LRU_BLOCK), jnp.float32),
            pltpu.VMEM((1, LRU_BLOCK), jnp.float32),
        ],
        compiler_params=_cparams("parallel", "parallel", "arbitrary"),
    )(*args)


def _attention_bias_table(rpb):
    off = np.arange(WIN_R)
    ridx = off[None, :] - off[:, None] + (WIN_R - 1)
    cq = np.arange(GRID_W)
    cs = np.clip(cq - WIN_C // 2, 0, GRID_W - WIN_C)
    ck = np.arange(GRID_W)
    valid = (ck[None, :] >= cs[:, None]) & (ck[None, :] < cs[:, None] + WIN_C)
    cidx = np.clip(ck[None, :] - cq[:, None] + WIN_C - 1, 0, 2 * WIN_C - 2)
    tab = rpb[:, ridx[:, None, :, None], cidx[None, :, None, :]]
    tab = jnp.where(valid[None, None, :, None, :], tab.astype(jnp.float32), MASK_VALUE)
    return tab.reshape(rpb.shape[0], WIN_R, GRID_W, WIN_R * GRID_W)


def _attention_kernel(q_ref, k_ref, v_ref, tb_ref, o_ref, kb_ref, vb_ref, *, rows):
    i = pl.program_id(2)

    @pl.when(i == 0)
    def _():
        kb_ref[...] = k_ref[...].astype(MXU_DTYPE)
        vb_ref[...] = v_ref[...].astype(MXU_DTYPE)

    scale = HEAD_DIM ** -0.5
    band = WIN_R * GRID_W
    for rl in range(ATT_ROWS):
        r = i * ATT_ROWS + rl
        rs = jnp.clip(r - WIN_R // 2, 0, rows - WIN_R)
        off = r - rs
        start = pl.multiple_of(rs * GRID_W, GRID_W)
        q = (q_ref[pl.ds(rl * GRID_W, GRID_W), :] * scale).astype(MXU_DTYPE)
        kband = kb_ref[pl.ds(start, band), :]
        vband = vb_ref[pl.ds(start, band), :]
        s = lax.dot_general(q, kband, (((1,), (1,)), ((), ())), preferred_element_type=jnp.float32)
        s = s + tb_ref[off]
        m = jnp.max(s, axis=-1, keepdims=True)
        p = jnp.exp(s - m)
        l = jnp.sum(p, axis=-1, keepdims=True)
        o = jnp.dot(p.astype(MXU_DTYPE), vband, preferred_element_type=jnp.float32)
        o_ref[pl.ds(rl * GRID_W, GRID_W), :] = (o / l).astype(o_ref.dtype)


def _attention(proj3, table, *, d_rnn, d_attn):
    b, s, _ = proj3.shape
    rows = s // GRID_W
    n_heads = d_attn // HEAD_DIM
    q_blk = 2 * d_rnn // HEAD_DIM
    k_blk = q_blk + n_heads
    v_blk = k_blk + n_heads
    tile = ATT_ROWS * GRID_W
    return pl.pallas_call(
        functools.partial(_attention_kernel, rows=rows),
        grid=(b, n_heads, rows // ATT_ROWS),
        in_specs=[
            pl.BlockSpec((None, tile, HEAD_DIM), lambda bi, h, i: (bi, i, q_blk + h)),
            pl.BlockSpec((None, s, HEAD_DIM), lambda bi, h, i: (bi, 0, k_blk + h)),
            pl.BlockSpec((None, s, HEAD_DIM), lambda bi, h, i: (bi, 0, v_blk + h)),
            pl.BlockSpec((None, WIN_R, GRID_W, WIN_R * GRID_W), lambda bi, h, i: (h, 0, 0, 0)),
        ],
        out_specs=pl.BlockSpec((None, tile, HEAD_DIM), lambda bi, h, i: (bi, i, h)),
        out_shape=jax.ShapeDtypeStruct((b, s, d_attn), MXU_DTYPE),
        scratch_shapes=[
            pltpu.VMEM((s, HEAD_DIM), MXU_DTYPE),
            pltpu.VMEM((s, HEAD_DIM), MXU_DTYPE),
        ],
        compiler_params=_cparams("parallel", "parallel", "arbitrary"),
    )(proj3, proj3, proj3, table)


def _merge_kernel(ya_ref, att_ref, wl_ref, wa_ref, gl_ref, ga_ref, bm_ref, o_ref):
    y_lru = jnp.dot(ya_ref[...], wl_ref[...], preferred_element_type=jnp.float32)
    y_att = jnp.dot(att_ref[...], wa_ref[...], preferred_element_type=jnp.float32)
    s_lru = jax.nn.sigmoid(gl_ref[...] + bm_ref[pl.ds(0, 1), :])
    s_att = jax.nn.sigmoid(ga_ref[...] + bm_ref[pl.ds(1, 1), :])
    o_ref[...] = (s_lru * y_lru + s_att * y_att).astype(o_ref.dtype)


def _merge(ya2, att2, wl, wa, proj2, b_merge, *, gate_col):
    n, d = ya2.shape
    tm = min(MERGE_TM, n)
    tn = min(MERGE_TN, d)
    gl_blk = gate_col // tn
    ga_blk = (gate_col + d) // tn
    return pl.pallas_call(
        _merge_kernel,
        grid=(n // tm, d // tn),
        in_specs=[
            pl.BlockSpec((tm, d), lambda i, j: (i, 0)),
            pl.BlockSpec((tm, d), lambda i, j: (i, 0)),
            pl.BlockSpec((d, tn), lambda i, j: (0, j)),
            pl.BlockSpec((d, tn), lambda i, j: (0, j)),
            pl.BlockSpec((tm, tn), lambda i, j: (i, gl_blk + j)),
            pl.BlockSpec((tm, tn), lambda i, j: (i, ga_blk + j)),
            pl.BlockSpec((2, tn), lambda i, j: (0, j)),
        ],
        out_specs=pl.BlockSpec((tm, tn), lambda i, j: (i, j)),
        out_shape=jax.ShapeDtypeStruct((n, d), MXU_DTYPE),
        compiler_params=_cparams("parallel", "arbitrary"),
    )(ya2, att2, wl, wa, proj2, proj2, b_merge)


def _out_proj_kernel(m_ref, w_ref, x_ref, o_ref):
    o_ref[...] = x_ref[...] + jnp.dot(m_ref[...], w_ref[...], preferred_element_type=jnp.float32)


def _out_proj(m2, w_o, x2):
    n, d = m2.shape
    tm = min(MERGE_TM, n)
    tn = min(MERGE_TN, d)
    return pl.pallas_call(
        _out_proj_kernel,
        grid=(n // tm, d // tn),
        in_specs=[
            pl.BlockSpec((tm, d), lambda i, j: (i, 0)),
            pl.BlockSpec((d, tn), lambda i, j: (0, j)),
            pl.BlockSpec((tm, tn), lambda i, j: (i, j)),
        ],
        out_specs=pl.BlockSpec((tm, tn), lambda i, j: (i, j)),
        out_shape=jax.ShapeDtypeStruct((n, d), jnp.float32),
        compiler_params=_cparams("parallel", "arbitrary"),
    )(m2, w_o, x2)


GROUP_LANE0 = N_EXPERTS
META_E, META_W, META_RANK = 0, 2, 4
GROUP_SHIFT = EXPERTS_PER_GROUP.bit_length() - 1


def _router_kernel(h_ref, g_ref, w_ref, b_ref, xn_ref, meta_ref, cnt_ref):
    step = pl.program_id(0)

    @pl.when(step == 0)
    def _():
        cnt_ref[...] = jnp.zeros_like(cnt_ref)

    x = h_ref[...]
    tm = x.shape[0]
    ms = jnp.mean(x * x, axis=-1, keepdims=True)
    xn = (x * lax.rsqrt(ms + EPS)) * g_ref[...]
    xn_ref[...] = xn
    logits = jnp.dot(xn, w_ref[...], preferred_element_type=jnp.float32,
                     precision=lax.Precision.HIGHEST) + b_ref[...]
    lane = lax.broadcasted_iota(jnp.int32, (tm, LANES), 1)
    neg_inf = -jnp.inf

    def first_argmax(vals, vmax):
        return jnp.min(jnp.where(vals == vmax, lane, LANES), axis=-1, keepdims=True)

    is_group = (lane >= GROUP_LANE0) & (lane < GROUP_LANE0 + N_GROUPS)
    lg = jnp.where(is_group, logits, neg_inf)
    mg = jnp.max(lg, axis=-1, keepdims=True)
    g_idx = first_argmax(lg, mg) - GROUP_LANE0
    g_val = 1.0 / jnp.sum(jnp.exp(lg - mg), axis=-1, keepdims=True)

    in_group = (lane < N_EXPERTS) & (lax.shift_right_logical(lane, GROUP_SHIFT) == g_idx)
    le = jnp.where(in_group, logits, neg_inf)
    m1 = jnp.max(le, axis=-1, keepdims=True)
    e1 = first_argmax(le, m1)
    le2 = jnp.where(lane == e1, neg_inf, le)
    m2 = jnp.max(le2, axis=-1, keepdims=True)
    e2 = first_argmax(le2, m2)
    z = jnp.sum(jnp.exp(le - m1), axis=-1, keepdims=True)
    p1 = 1.0 / z
    p2 = jnp.exp(m2 - m1) / z
    den = p1 + p2
    w1 = g_val * (p1 / den)
    w2 = g_val * (p2 / den)

    hot1 = lane == e1
    hot2 = lane == e2
    hot = hot1.astype(jnp.float32) + hot2.astype(jnp.float32)
    r_i = lax.broadcasted_iota(jnp.int32, (tm, tm), 0)
    c_i = lax.broadcasted_iota(jnp.int32, (tm, tm), 1)
    lower = (c_i < r_i).astype(MXU_DTYPE)
    before = jnp.dot(lower, hot.astype(MXU_DTYPE), preferred_element_type=jnp.float32) + cnt_ref[...]
    rank1 = jnp.sum(jnp.where(hot1, before, 0.0), axis=-1, keepdims=True)
    rank2 = jnp.sum(jnp.where(hot2, before, 0.0), axis=-1, keepdims=True)
    cnt_ref[...] += jnp.sum(hot, axis=0, keepdims=True)

    meta = jnp.where(lane == META_E, e1.astype(jnp.float32), 0.0)
    meta = jnp.where(lane == META_E + 1, e2.astype(jnp.float32), meta)
    meta = jnp.where(lane == META_W, w1, meta)
    meta = jnp.where(lane == META_W + 1, w2, meta)
    meta = jnp.where(lane == META_RANK, rank1, meta)
    meta = jnp.where(lane == META_RANK + 1, rank2, meta)
    meta_ref[...] = meta


def _router(h2, g, w_router, b_router):
    n, d = h2.shape
    tm = min(ROUTER_TM, n)
    return pl.pallas_call(
        _router_kernel,
        grid=(n // tm,),
        in_specs=[
            pl.BlockSpec((tm, d), lambda i: (i, 0)),
            pl.BlockSpec((1, d), lambda i: (0, 0)),
            pl.BlockSpec((d, LANES), lambda i: (0, 0)),
            pl.BlockSpec((1, LANES), lambda i: (0, 0)),
        ],
        out_specs=[
            pl.BlockSpec((tm, d), lambda i: (i, 0)),
            pl.BlockSpec((tm, LANES), lambda i: (i, 0)),
            pl.BlockSpec((1, LANES), lambda i: (0, 0)),
        ],
        out_shape=[
            jax.ShapeDtypeStruct((n, d), jnp.float32),
            jax.ShapeDtypeStruct((n, LANES), jnp.float32),
            jax.ShapeDtypeStruct((1, LANES), jnp.float32),
        ],
        compiler_params=_cparams("arbitrary"),
    )(h2, g.reshape(1, d), w_router, b_router)


def _experts_kernel(be_ref, tok_ref, nused_ref, x_hbm, wg_ref, wu_ref, wd_ref, o_ref, xbuf, sem, *, bm):
    b = pl.program_id(0)
    n_used = nused_ref[0]

    def row_copy(blk, slot, i):
        tok = tok_ref[blk * bm + i]
        return pltpu.make_async_copy(x_hbm.at[pl.ds(tok, 1), :], xbuf.at[slot, pl.ds(i, 1), :], sem.at[slot])

    def start_gather(blk, slot):
        def body(i, _):
            row_copy(blk, slot, i).start()
            return 0
        lax.fori_loop(0, bm, body, 0)

    def wait_gather(blk, slot):
        def body(i, _):
            row_copy(blk, slot, i).wait()
            return 0
        lax.fori_loop(0, bm, body, 0)

    slot = b % 2

    @pl.when((b == 0) & (n_used > 0))
    def _():
        start_gather(0, 0)

    @pl.when(b + 1 < n_used)
    def _():
        start_gather(b + 1, 1 - slot)

    @pl.when(b < n_used)
    def _():
        wait_gather(b, slot)
        x = xbuf[slot].astype(MXU_DTYPE)
        hg = jnp.dot(x, wg_ref[...], preferred_element_type=jnp.float32)
        hu = jnp.dot(x, wu_ref[...], preferred_element_type=jnp.float32)
        hid = (jax.nn.silu(hg) * hu).astype(MXU_DTYPE)
        o_ref[...] = jnp.dot(hid, wd_ref[...], preferred_element_type=jnp.float32)

    @pl.when(b >= n_used)
    def _():
        o_ref[...] = jnp.zeros_like(o_ref)


def _experts(xn2, block_e, buf_tok, n_used, w_gate, w_up, w_down, *, n_blocks):
    n, d = xn2.shape
    f = w_gate.shape[2]
    bm = MOE_BM
    grid_spec = pltpu.PrefetchScalarGridSpec(
        num_scalar_prefetch=3,
        grid=(n_blocks,),
        in_specs=[
            pl.BlockSpec(memory_space=pl.ANY),
            pl.BlockSpec((None, d, f), lambda b, be, tok, nu: (be[b], 0, 0)),
            pl.BlockSpec((None, d, f), lambda b, be, tok, nu: (be[b], 0, 0)),
            pl.BlockSpec((None, f, d), lambda b, be, tok, nu: (be[b], 0, 0)),
        ],
        out_specs=pl.BlockSpec((bm, d), lambda b, be, tok, nu: (b, 0)),
        scratch_shapes=[
            pltpu.VMEM((2, bm, d), jnp.float32),
            pltpu.SemaphoreType.DMA((2,)),
        ],
    )
    return pl.pallas_call(
        functools.partial(_experts_kernel, bm=bm),
        grid_spec=grid_spec,
        out_shape=jax.ShapeDtypeStruct((n_blocks * bm, d), jnp.float32),
        compiler_params=_cparams("arbitrary"),
    )(block_e, buf_tok, n_used, xn2, w_gate, w_up, w_down)


def _combine_kernel(pos_ref, y_hbm, h_ref, meta_ref, g_ref, o_ref, ybuf, sem, *, tm, n_steps):
    step = pl.program_id(0)

    def row_copy(blk, slot, i, k):
        pos = pos_ref[(blk * tm + i) * TOP_K + k]
        return pltpu.make_async_copy(y_hbm.at[pl.ds(pos, 1), :], ybuf.at[slot, k, pl.ds(i, 1), :], sem.at[slot])

    def start_gather(blk, slot):
        def body(i, _):
            for k in range(TOP_K):
                row_copy(blk, slot, i, k).start()
            return 0
        lax.fori_loop(0, tm, body, 0)

    def wait_gather(blk, slot):
        def body(i, _):
            for k in range(TOP_K):
                row_copy(blk, slot, i, k).wait()
            return 0
        lax.fori_loop(0, tm, body, 0)

    slot = step % 2

    @pl.when(step == 0)
    def _():
        start_gather(0, 0)

    @pl.when(step + 1 < n_steps)
    def _():
        start_gather(step + 1, 1 - slot)

    wait_gather(step, slot)
    w1 = meta_ref[:, META_W:META_W + 1]
    w2 = meta_ref[:, META_W + 1:META_W + 2]
    h = h_ref[...] + (ybuf[slot, 0] * w1 + ybuf[slot, 1] * w2)
    ms = jnp.mean(h * h, axis=-1, keepdims=True)
    o_ref[...] = (h * lax.rsqrt(ms + EPS)) * g_ref[...]


def _combine(pos, yb, h2, meta, g):
    n, d = h2.shape
    tm = min(COMBINE_TM, n)
    n_steps = n // tm
    grid_spec = pltpu.PrefetchScalarGridSpec(
        num_scalar_prefetch=1,
        grid=(n_steps,),
        in_specs=[
            pl.BlockSpec(memory_space=pl.ANY),
            pl.BlockSpec((tm, d), lambda i, pos: (i, 0)),
            pl.BlockSpec((tm, LANES), lambda i, pos: (i, 0)),
            pl.BlockSpec((1, d), lambda i, pos: (0, 0)),
        ],
        out_specs=pl.BlockSpec((tm, d), lambda i, pos: (i, 0)),
        scratch_shapes=[
            pltpu.VMEM((2, TOP_K, tm, d), jnp.float32),
            pltpu.SemaphoreType.DMA((2,)),
        ],
    )
    return pl.pallas_call(
        functools.partial(_combine_kernel, tm=tm, n_steps=n_steps),
        grid_spec=grid_spec,
        out_shape=jax.ShapeDtypeStruct((n, d), jnp.float32),
        compiler_params=_cparams("arbitrary"),
    )(pos, yb, h2, meta, g.reshape(1, d))


def _dispatch_plan(meta, counts_f, n):
    bm = MOE_BM
    a = n * TOP_K
    expert = meta[:, META_E:META_E + TOP_K].astype(jnp.int32)
    rank = meta[:, META_RANK:META_RANK + TOP_K].astype(jnp.int32)
    counts = counts_f[0, :N_EXPERTS].astype(jnp.int32)
    padded = (counts + bm - 1) // bm * bm
    pad_end = jnp.cumsum(padded)
    pad_start = pad_end - padded
    pos = (pad_start[expert] + rank).reshape(a)
    n_blocks = -(-a // bm) + N_EXPERTS
    tok = jnp.arange(a, dtype=jnp.int32) // TOP_K
    buf_tok = jnp.zeros((n_blocks * bm,), jnp.int32).at[pos].set(tok)
    block_e = jnp.minimum(jnp.searchsorted(pad_end, jnp.arange(n_blocks, dtype=jnp.int32) * bm, side='right'),
                          N_EXPERTS - 1).astype(jnp.int32)
    n_used = (pad_end[-1] // bm).astype(jnp.int32).reshape(1)
    return pos, buf_tok, block_e, n_used, n_blocks


def _encoder(x, p):
    b, s, d = x.shape
    n = b * s
    d_rnn = p["d_rnn"]
    d_attn = p["d_attn"]
    x2 = x.reshape(n, d)
    proj2 = _norm_proj(x2, p["norm_mix_g"], p["w_in"])
    d_in = proj2.shape[1]
    proj3 = proj2.reshape(b, s, d_in)
    hf = _lru_direction(proj3, p["conv_w"], p["conv_b"], p["lru_wg"][0], p["lru_b_a"][0], p["lru_b_x"][0],
                        p["lru_lambda"][0], None, reverse=False, d_rnn=d_rnn)
    ya = _lru_direction(proj3, p["conv_w"], p["conv_b"], p["lru_wg"][1], p["lru_b_a"][1], p["lru_b_x"][1],
                        p["lru_lambda"][1], hf, reverse=True, d_rnn=d_rnn)
    att = _attention(proj3, p["att_table"], d_rnn=d_rnn, d_attn=d_attn)
    m2 = _merge(ya.reshape(n, d_rnn), att.reshape(n, d_attn), p["w_lru_up"], p["w_attn_up"], proj2,
                p["b_merge"], gate_col=2 * d_rnn + 3 * d_attn)
    h1 = _out_proj(m2, p["w_o"], x2)
    xn2, meta, counts = _router(h1, p["norm_ffn_g"], p["w_router"], p["b_router"])
    pos, buf_tok, block_e, n_used, n_blocks = _dispatch_plan(meta, counts, n)
    yb = _experts(xn2, block_e, buf_tok, n_used, p["w_exp_gate"], p["w_exp_up"], p["w_exp_down"],
                  n_blocks=n_blocks)
    y = _combine(pos, yb, h1, meta, p["norm_final_g"])
    return y.reshape(b, s, d)


def _prepare_params(norm_mix_g, w_in, conv_w, conv_b, lru_w_a, lru_b_a, lru_w_x, lru_b_x, lru_lambda, rpb,
                    w_lru_up, w_attn_up, b_merge, w_o, norm_ffn_g, w_router_group, b_router_group,
                    w_router_expert, b_router_expert, w_exp_gate, w_exp_up, w_exp_down, norm_final_g):
    bf16 = MXU_DTYPE
    d = w_in.shape[1]
    d_rnn = conv_w.shape[2]
    d_attn = w_attn_up.shape[1]
    lru_wg = jnp.concatenate([lru_w_a[0], lru_w_x[0]], axis=-1).astype(bf16)
    w_router = jnp.zeros((d, LANES), jnp.float32)
    w_router = w_router.at[:, :N_EXPERTS].set(w_router_expert[0])
    w_router = w_router.at[:, GROUP_LANE0:GROUP_LANE0 + N_GROUPS].set(w_router_group[0])
    b_router = jnp.zeros((1, LANES), jnp.float32)
    b_router = b_router.at[0, :N_EXPERTS].set(b_router_expert[0])
    b_router = b_router.at[0, GROUP_LANE0:GROUP_LANE0 + N_GROUPS].set(b_router_group[0])
    return dict(
        d_rnn=d_rnn, d_attn=d_attn,
        norm_mix_g=norm_mix_g[0], w_in=w_in[0].astype(bf16),
        conv_w=conv_w[0], conv_b=conv_b[0], lru_wg=lru_wg,
        lru_b_a=lru_b_a[0], lru_b_x=lru_b_x[0], lru_lambda=lru_lambda[0],
        att_table=_attention_bias_table(rpb[0]),
        w_lru_up=w_lru_up[0].astype(bf16), w_attn_up=w_attn_up[0].astype(bf16),
        b_merge=b_merge[0], w_o=w_o[0].astype(bf16),
        norm_ffn_g=norm_ffn_g[0], w_router=w_router, b_router=b_router,
        w_exp_gate=w_exp_gate[0].astype(bf16), w_exp_up=w_exp_up[0].astype(bf16),
        w_exp_down=w_exp_down[0].astype(bf16), norm_final_g=norm_final_g,
    )


def kernel(x_prompt, x_sample, norm_mix_g, w_in, conv_w, conv_b, lru_w_a, lru_b_a, lru_w_x, lru_b_x, lru_lambda, rpb, w_lru_up, w_attn_up, b_merge, w_o, norm_ffn_g, w_router_group, b_router_group, w_router_expert, b_router_expert, w_exp_gate, w_exp_up, w_exp_down, norm_final_g):
    assert norm_mix_g.shape[0] == 1, "single-layer encoder"
    p = _prepare_params(norm_mix_g, w_in, conv_w, conv_b, lru_w_a, lru_b_a, lru_w_x, lru_b_x, lru_lambda, rpb,
                        w_lru_up, w_attn_up, b_merge, w_o, norm_ffn_g, w_router_group, b_router_group,
                        w_router_expert, b_router_expert, w_exp_gate, w_exp_up, w_exp_down, norm_final_g)
    return (_encoder(x_prompt, p), _encoder(x_sample, p))
```

```python
import functools

import numpy as np
import jax
import jax.numpy as jnp
from jax import lax
from jax.experimental import pallas as pl
from jax.experimental.pallas import tpu as pltpu

GRID_W = 64
WIN_R = 8
WIN_C = 16
CONV_W = 4
LRU_C = 8.0
LRU_BLOCK = 128
HEAD_DIM = 128
N_GROUPS = 4
EXPERTS_PER_GROUP = 8
N_EXPERTS = N_GROUPS * EXPERTS_PER_GROUP
TOP_K = 2
EPS = 1e-6
MASK_VALUE = -1e30

MXU_DTYPE = jnp.bfloat16

LANES = 128
SUBLANES = 8
VMEM_LIMIT_BYTES = 56 * 1024 * 1024

PROJ_TM = 1024
PROJ_TN = 1024
MERGE_TM = 1024
MERGE_TN = 512
LRU_TC = 2048
LRU_GROUPS = 4
LRU_UNROLL = 4
ATT_ROWS = 8
ROUTER_TM = 512
MOE_BM = 256
COMBINE_TM = 256
GATHER_UNROLL = 8


def _cparams(*sem):
    return pltpu.CompilerParams(dimension_semantics=sem, vmem_limit_bytes=VMEM_LIMIT_BYTES)


def _norm_proj_kernel(x_ref, g_ref, w_ref, o_ref, xn_ref):
    @pl.when(pl.program_id(1) == 0)
    def _():
        x = x_ref[...]
        ms = jnp.mean(x * x, axis=-1, keepdims=True)
        xn_ref[...] = ((x * lax.rsqrt(ms + EPS)) * g_ref[...]).astype(MXU_DTYPE)

    o_ref[...] = jnp.dot(xn_ref[...], w_ref[...], preferred_element_type=jnp.float32)


def _norm_proj(x2, g, w_bf16):
    n, d = x2.shape
    d_out = w_bf16.shape[1]
    tm = min(PROJ_TM, n)
    tn = min(PROJ_TN, d_out)
    return pl.pallas_call(
        _norm_proj_kernel,
        grid=(n // tm, d_out // tn),
        in_specs=[
            pl.BlockSpec((tm, d), lambda i, j: (i, 0)),
            pl.BlockSpec((1, d), lambda i, j: (0, 0)),
            pl.BlockSpec((d, tn), lambda i, j: (0, j)),
        ],
        out_specs=pl.BlockSpec((tm, tn), lambda i, j: (i, j)),
        out_shape=jax.ShapeDtypeStruct((n, d_out), jnp.float32),
        scratch_shapes=[pltpu.VMEM((tm, d), MXU_DTYPE)],
        compiler_params=_cparams("parallel", "arbitrary"),
    )(x2, g.reshape(1, d), w_bf16)


def _softplus(y):
    return jnp.maximum(y, 0.0) + jnp.log1p(jnp.exp(-jnp.abs(y)))


def _lru_pitch(tc):
    seg_len = tc // (LRU_GROUPS * SUBLANES)
    assert seg_len * LRU_GROUPS * SUBLANES == tc and seg_len % SUBLANES == 0
    return seg_len + SUBLANES // 2


def _lru_kernel(*refs, reverse, n_chunks, tc):
    if reverse:
        (x_ref, xp_ref, xn_ref, cw_ref, cb_ref, wg_ref, ba_ref, bx_ref, lam_ref,
         hf_ref, gate_ref, o_ref, xpad_ref, a_ref, u_ref, h_ref, carry_ref) = refs
    else:
        (x_ref, xp_ref, xn_ref, cw_ref, cb_ref, wg_ref, ba_ref, bx_ref, lam_ref,
         o_ref, xpad_ref, a_ref, u_ref, h_ref, carry_ref) = refs
    t = pl.program_id(2)
    chunk = (n_chunks - 1 - t) if reverse else t

    @pl.when(t == 0)
    def _():
        carry_ref[...] = jnp.zeros_like(carry_ref)

    prev = jnp.where(chunk == 0, 0.0, xp_ref[...])
    nxt = jnp.where(chunk == n_chunks - 1, 0.0, xn_ref[...])
    xpad_ref[pl.ds(0, SUBLANES), :] = prev
    xpad_ref[pl.ds(SUBLANES, tc), :] = x_ref[...]
    xpad_ref[pl.ds(SUBLANES + tc, SUBLANES), :] = nxt
    left = CONV_W // 2
    xc = cb_ref[...]
    for tap in range(CONV_W):
        xc = xc + xpad_ref[pl.ds(SUBLANES - left + tap, tc), :] * cw_ref[pl.ds(tap, 1), :]

    z = jnp.dot(xc.astype(MXU_DTYPE), wg_ref[...], preferred_element_type=jnp.float32)
    r = jax.nn.sigmoid(z[:, :LRU_BLOCK] + ba_ref[...])
    i = jax.nn.sigmoid(z[:, LRU_BLOCK:] + bx_ref[...])
    log_a = (-LRU_C * r) * _softplus(-lam_ref[...])
    a = jnp.exp(log_a)
    u = jnp.sqrt(jnp.tanh(-log_a) * (1.0 + a * a)) * (i * xc)

    pitch = _lru_pitch(tc)
    slots = LRU_GROUPS * SUBLANES * pitch
    a_ref[pl.ds(0, tc), :] = a
    u_ref[pl.ds(0, tc), :] = u
    a_ref[pl.ds(tc, slots - tc), :] = jnp.ones((slots - tc, LRU_BLOCK), jnp.float32)
    u_ref[pl.ds(tc, slots - tc), :] = jnp.zeros((slots - tc, LRU_BLOCK), jnp.float32)

    def seg_rows(g, k):
        return pl.ds(g * SUBLANES * pitch + k, SUBLANES, stride=pitch)

    def slot_of(kk):
        return (pitch - 1 - kk) if reverse else kk

    def scan_step(kk, carry):
        hs, ps = carry
        k = slot_of(kk)
        new_h, new_p = [], []
        for g in range(LRU_GROUPS):
            a_k = a_ref[seg_rows(g, k), :]
            h = a_k * hs[g] + u_ref[seg_rows(g, k), :]
            h_ref[seg_rows(g, k), :] = h
            new_h.append(h)
            new_p.append(a_k * ps[g])
        return tuple(new_h), tuple(new_p)

    zeros = tuple(jnp.zeros((SUBLANES, LANES), jnp.float32) for _ in range(LRU_GROUPS))
    ones = tuple(jnp.ones((SUBLANES, LANES), jnp.float32) for _ in range(LRU_GROUPS))
    h_end, p_end = lax.fori_loop(0, pitch, scan_step, (zeros, ones), unroll=LRU_UNROLL)

    row = lax.broadcasted_iota(jnp.int32, (SUBLANES, LANES), 0)
    c_in = [None] * LRU_GROUPS
    cg = carry_ref[...]
    for g in (reversed(range(LRU_GROUPS)) if reverse else range(LRU_GROUPS)):
        pa, hb = p_end[g], h_end[g]
        for s in (1, 2, 4):
            keep = (row < SUBLANES - s) if reverse else (row >= s)
            shift = (SUBLANES - s) if reverse else s
            a_sh = jnp.where(keep, pltpu.roll(pa, shift, 0), 1.0)
            b_sh = jnp.where(keep, pltpu.roll(hb, shift, 0), 0.0)
            hb = pa * b_sh + hb
            pa = pa * a_sh
        seg_out = pa * cg + hb
        if reverse:
            c_in[g] = jnp.where(row == SUBLANES - 1, cg, pltpu.roll(seg_out, SUBLANES - 1, 0))
            cg = seg_out[0:1, :]
        else:
            c_in[g] = jnp.where(row == 0, cg, pltpu.roll(seg_out, 1, 0))
            cg = seg_out[SUBLANES - 1:SUBLANES, :]
    carry_ref[...] = cg

    def fix_step(kk, ps):
        k = slot_of(kk)
        new_p = []
        for g in range(LRU_GROUPS):
            p = a_ref[seg_rows(g, k), :] * ps[g]
            u_ref[seg_rows(g, k), :] = h_ref[seg_rows(g, k), :] + p * c_in[g]
            new_p.append(p)
        return tuple(new_p)

    lax.fori_loop(0, pitch, fix_step, ones, unroll=LRU_UNROLL)

    h_all = u_ref[pl.ds(0, tc), :]
    if reverse:
        o_ref[...] = (jax.nn.gelu(gate_ref[...]) * (hf_ref[...] + h_all)).astype(o_ref.dtype)
    else:
        o_ref[...] = h_all


def _lru_direction(proj3, conv_w, conv_b, wg, ba, bx, lam, hf, *, reverse, d_rnn):
    b, s, _ = proj3.shape
    n_cb = d_rnn // LRU_BLOCK
    tc = min(LRU_TC, s)
    n_chunks = s // tc
    tc8 = tc // SUBLANES
    s8 = s // SUBLANES
    slots = LRU_GROUPS * SUBLANES * _lru_pitch(tc)

    def chunk_of(t):
        return (n_chunks - 1 - t) if reverse else t

    vec_spec = pl.BlockSpec((1, LRU_BLOCK), lambda bi, c, t: (0, c))
    in_specs = [
        pl.BlockSpec((None, tc, LRU_BLOCK), lambda bi, c, t: (bi, chunk_of(t), c)),
        pl.BlockSpec((None, SUBLANES, LRU_BLOCK),
                     lambda bi, c, t: (bi, jnp.maximum(chunk_of(t) * tc8 - 1, 0), c)),
        pl.BlockSpec((None, SUBLANES, LRU_BLOCK),
                     lambda bi, c, t: (bi, jnp.minimum((chunk_of(t) + 1) * tc8, s8 - 1), c)),
        pl.BlockSpec((CONV_W, LRU_BLOCK), lambda bi, c, t: (0, c)),
        vec_spec,
        pl.BlockSpec((None, LRU_BLOCK, 2 * LRU_BLOCK), lambda bi, c, t: (c, 0, 0)),
        vec_spec, vec_spec, vec_spec,
    ]
    args = [proj3, proj3, proj3, conv_w, conv_b.reshape(1, d_rnn), wg,
            ba.reshape(1, d_rnn), bx.reshape(1, d_rnn), lam.reshape(1, d_rnn)]
    if reverse:
        in_specs += [
            pl.BlockSpec((None, tc, LRU_BLOCK), lambda bi, c, t: (bi, chunk_of(t), c)),
            pl.BlockSpec((None, tc, LRU_BLOCK), lambda bi, c, t: (bi, chunk_of(t), n_cb + c)),
        ]
        args += [hf, proj3]
        out_dtype = MXU_DTYPE
    else:
        out_dtype = jnp.float32
    return pl.pallas_call(
        functools.partial(_lru_kernel, reverse=reverse, n_chunks=n_chunks, tc=tc),
        grid=(b, n_cb, n_chunks),
        in_specs=in_specs,
        out_specs=pl.BlockSpec((None, tc, LRU_BLOCK), lambda bi, c, t: (bi, chunk_of(t), c)),
        out_shape=jax.ShapeDtypeStruct((b, s, d_rnn), out_dtype),
        scratch_shapes=[
            pltpu.VMEM((tc + 2 * SUBLANES, LRU_BLOCK), jnp.float32),
            pltpu.VMEM((slots, LRU_BLOCK), jnp.float32),
            pltpu.VMEM((slots, LRU_BLOCK), jnp.float32),
            pltpu.VMEM((slots, LRU_BLOCK), jnp.float32),
            pltpu.VMEM((1, LRU_BLOCK), jnp.float32),
        ],
        compiler_params=_cparams("parallel", "parallel", "arbitrary"),
    )(*args)


def _attention_bias_table(rpb):
    n_heads, n_dr, n_dc = rpb.shape
    cq = np.arange(GRID_W)
    cs = np.clip(cq - WIN_C // 2, 0, GRID_W - WIN_C)
    ck = np.arange(GRID_W)
    valid = (ck[None, :] >= cs[:, None]) & (ck[None, :] < cs[:, None] + WIN_C)
    pad = GRID_W - WIN_C
    width = n_dc + 2 * pad + 1
    padded = jnp.pad(rpb.astype(jnp.float32), ((0, 0), (0, 0), (pad, pad + 1)))
    tiled = jnp.tile(padded, (1, 1, GRID_W))[:, :, :GRID_W * (width - 1)]
    skew = tiled.reshape(n_heads, n_dr, GRID_W, width - 1)
    toep = skew[:, :, :, GRID_W - 1:2 * GRID_W - 1]
    toep = jnp.where(valid[None, None], toep, MASK_VALUE)
    per_off = [toep[:, WIN_R - 1 - off:2 * WIN_R - 1 - off] for off in range(WIN_R)]
    tab = jnp.stack(per_off, axis=1)
    tab = tab.transpose(0, 1, 3, 2, 4)
    return tab.reshape(n_heads, WIN_R, GRID_W, WIN_R * GRID_W)


def _attention_kernel(q_ref, k_ref, v_ref, tb_ref, o_ref, kb_ref, vb_ref, *, rows):
    i = pl.program_id(2)

    @pl.when(i == 0)
    def _():
        kb_ref[...] = k_ref[...].astype(MXU_DTYPE)
        vb_ref[...] = v_ref[...].astype(MXU_DTYPE)

    scale = HEAD_DIM ** -0.5
    band = WIN_R * GRID_W
    starts, scores = [], []
    for rl in range(ATT_ROWS):
        r = i * ATT_ROWS + rl
        rs = jnp.clip(r - WIN_R // 2, 0, rows - WIN_R)
        start = pl.multiple_of(rs * GRID_W, GRID_W)
        q = (q_ref[pl.ds(rl * GRID_W, GRID_W), :] * scale).astype(MXU_DTYPE)
        kband = kb_ref[pl.ds(start, band), :]
        s = lax.dot_general(q, kband, (((1,), (1,)), ((), ())), preferred_element_type=jnp.float32)
        starts.append(start)
        scores.append(s + tb_ref[r - rs])
    probs, denoms = [], []
    for s in scores:
        m = jnp.max(s, axis=-1, keepdims=True)
        p = jnp.exp(s - m)
        denoms.append(jnp.sum(p, axis=-1, keepdims=True))
        probs.append(p.astype(MXU_DTYPE))
    for rl in range(ATT_ROWS):
        vband = vb_ref[pl.ds(starts[rl], band), :]
        o = jnp.dot(probs[rl], vband, preferred_element_type=jnp.float32)
        o_ref[pl.ds(rl * GRID_W, GRID_W), :] = (o / denoms[rl]).astype(o_ref.dtype)


def _attention(proj3, table, *, d_rnn, d_attn):
    b, s, _ = proj3.shape
    rows = s // GRID_W
    n_heads = d_attn // HEAD_DIM
    q_blk = 2 * d_rnn // HEAD_DIM
    k_blk = q_blk + n_heads
    v_blk = k_blk + n_heads
    tile = ATT_ROWS * GRID_W
    return pl.pallas_call(
        functools.partial(_attention_kernel, rows=rows),
        grid=(b, n_heads, rows // ATT_ROWS),
        in_specs=[
            pl.BlockSpec((None, tile, HEAD_DIM), lambda bi, h, i: (bi, i, q_blk + h)),
            pl.BlockSpec((None, s, HEAD_DIM), lambda bi, h, i: (bi, 0, k_blk + h)),
            pl.BlockSpec((None, s, HEAD_DIM), lambda bi, h, i: (bi, 0, v_blk + h)),
            pl.BlockSpec((None, WIN_R, GRID_W, WIN_R * GRID_W), lambda bi, h, i: (h, 0, 0, 0)),
        ],
        out_specs=pl.BlockSpec((None, tile, HEAD_DIM), lambda bi, h, i: (bi, i, h)),
        out_shape=jax.ShapeDtypeStruct((b, s, d_attn), MXU_DTYPE),
        scratch_shapes=[
            pltpu.VMEM((s, HEAD_DIM), MXU_DTYPE),
            pltpu.VMEM((s, HEAD_DIM), MXU_DTYPE),
        ],
        compiler_params=_cparams("parallel", "parallel", "arbitrary"),
    )(proj3, proj3, proj3, table)


def _merge_kernel(ya_ref, att_ref, wl_ref, wa_ref, gl_ref, ga_ref, bm_ref, o_ref):
    y_lru = jnp.dot(ya_ref[...], wl_ref[...], preferred_element_type=jnp.float32)
    y_att = jnp.dot(att_ref[...], wa_ref[...], preferred_element_type=jnp.float32)
    s_lru = jax.nn.sigmoid(gl_ref[...] + bm_ref[pl.ds(0, 1), :])
    s_att = jax.nn.sigmoid(ga_ref[...] + bm_ref[pl.ds(1, 1), :])
    o_ref[...] = (s_lru * y_lru + s_att * y_att).astype(o_ref.dtype)


def _merge(ya2, att2, wl, wa, proj2, b_merge, *, gate_col):
    n, d = ya2.shape
    tm = min(MERGE_TM, n)
    tn = min(MERGE_TN, d)
    gl_blk = gate_col // tn
    ga_blk = (gate_col + d) // tn
    return pl.pallas_call(
        _merge_kernel,
        grid=(n // tm, d // tn),
        in_specs=[
            pl.BlockSpec((tm, d), lambda i, j: (i, 0)),
            pl.BlockSpec((tm, d), lambda i, j: (i, 0)),
            pl.BlockSpec((d, tn), lambda i, j: (0, j)),
            pl.BlockSpec((d, tn), lambda i, j: (0, j)),
            pl.BlockSpec((tm, tn), lambda i, j: (i, gl_blk + j)),
            pl.BlockSpec((tm, tn), lambda i, j: (i, ga_blk + j)),
            pl.BlockSpec((2, tn), lambda i, j: (0, j)),
        ],
        out_specs=pl.BlockSpec((tm, tn), lambda i, j: (i, j)),
        out_shape=jax.ShapeDtypeStruct((n, d), MXU_DTYPE),
        compiler_params=_cparams("parallel", "arbitrary"),
    )(ya2, att2, wl, wa, proj2, proj2, b_merge)


def _out_proj_kernel(m_ref, w_ref, x_ref, o_ref):
    o_ref[...] = x_ref[...] + jnp.dot(m_ref[...], w_ref[...], preferred_element_type=jnp.float32)


def _out_proj(m2, w_o, x2):
    n, d = m2.shape
    tm = min(MERGE_TM, n)
    tn = min(MERGE_TN, d)
    return pl.pallas_call(
        _out_proj_kernel,
        grid=(n // tm, d // tn),
        in_specs=[
            pl.BlockSpec((tm, d), lambda i, j: (i, 0)),
            pl.BlockSpec((d, tn), lambda i, j: (0, j)),
            pl.BlockSpec((tm, tn), lambda i, j: (i, j)),
        ],
        out_specs=pl.BlockSpec((tm, tn), lambda i, j: (i, j)),
        out_shape=jax.ShapeDtypeStruct((n, d), jnp.float32),
        compiler_params=_cparams("parallel", "arbitrary"),
    )(m2, w_o, x2)


GROUP_LANE0 = N_EXPERTS
META_E, META_W, META_RANK = 0, 2, 4
GROUP_SHIFT = EXPERTS_PER_GROUP.bit_length() - 1


def _router_kernel(h_ref, g_ref, w_ref, b_ref, xn_ref, meta_ref, cnt_ref):
    step = pl.program_id(0)

    @pl.when(step == 0)
    def _():
        cnt_ref[...] = jnp.zeros_like(cnt_ref)

    x = h_ref[...]
    tm = x.shape[0]
    ms = jnp.mean(x * x, axis=-1, keepdims=True)
    xn = (x * lax.rsqrt(ms + EPS)) * g_ref[...]
    xn_ref[...] = xn
    logits = jnp.dot(xn, w_ref[...], preferred_element_type=jnp.float32,
                     precision=lax.Precision.HIGHEST) + b_ref[...]
    lane = lax.broadcasted_iota(jnp.int32, (tm, LANES), 1)
    neg_inf = -jnp.inf

    def first_argmax(vals, vmax):
        return jnp.min(jnp.where(vals == vmax, lane, LANES), axis=-1, keepdims=True)

    is_group = (lane >= GROUP_LANE0) & (lane < GROUP_LANE0 + N_GROUPS)
    lg = jnp.where(is_group, logits, neg_inf)
    mg = jnp.max(lg, axis=-1, keepdims=True)
    g_idx = first_argmax(lg, mg) - GROUP_LANE0
    g_val = 1.0 / jnp.sum(jnp.exp(lg - mg), axis=-1, keepdims=True)

    in_group = (lane < N_EXPERTS) & (lax.shift_right_logical(lane, GROUP_SHIFT) == g_idx)
    le = jnp.where(in_group, logits, neg_inf)
    m1 = jnp.max(le, axis=-1, keepdims=True)
    e1 = first_argmax(le, m1)
    le2 = jnp.where(lane == e1, neg_inf, le)
    m2 = jnp.max(le2, axis=-1, keepdims=True)
    e2 = first_argmax(le2, m2)
    z = jnp.sum(jnp.exp(le - m1), axis=-1, keepdims=True)
    p1 = 1.0 / z
    p2 = jnp.exp(m2 - m1) / z
    den = p1 + p2
    w1 = g_val * (p1 / den)
    w2 = g_val * (p2 / den)

    hot1 = lane == e1
    hot2 = lane == e2
    hot = hot1.astype(jnp.float32) + hot2.astype(jnp.float32)
    r_i = lax.broadcasted_iota(jnp.int32, (tm, tm), 0)
    c_i = lax.broadcasted_iota(jnp.int32, (tm, tm), 1)
    lower = (c_i < r_i).astype(MXU_DTYPE)
    before = jnp.dot(lower, hot.astype(MXU_DTYPE), preferred_element_type=jnp.float32) + cnt_ref[...]
    rank1 = jnp.sum(jnp.where(hot1, before, 0.0), axis=-1, keepdims=True)
    rank2 = jnp.sum(jnp.where(hot2, before, 0.0), axis=-1, keepdims=True)
    cnt_ref[...] += jnp.sum(hot, axis=0, keepdims=True)

    meta = jnp.where(lane == META_E, e1.astype(jnp.float32), 0.0)
    meta = jnp.where(lane == META_E + 1, e2.astype(jnp.float32), meta)
    meta = jnp.where(lane == META_W, w1, meta)
    meta = jnp.where(lane == META_W + 1, w2, meta)
    meta = jnp.where(lane == META_RANK, rank1, meta)
    meta = jnp.where(lane == META_RANK + 1, rank2, meta)
    meta_ref[...] = meta


def _router(h2, g, w_router, b_router):
    n, d = h2.shape
    tm = min(ROUTER_TM, n)
    return pl.pallas_call(
        _router_kernel,
        grid=(n // tm,),
        in_specs=[
            pl.BlockSpec((tm, d), lambda i: (i, 0)),
            pl.BlockSpec((1, d), lambda i: (0, 0)),
            pl.BlockSpec((d, LANES), lambda i: (0, 0)),
            pl.BlockSpec((1, LANES), lambda i: (0, 0)),
        ],
        out_specs=[
            pl.BlockSpec((tm, d), lambda i: (i, 0)),
            pl.BlockSpec((tm, LANES), lambda i: (i, 0)),
            pl.BlockSpec((1, LANES), lambda i: (0, 0)),
        ],
        out_shape=[
            jax.ShapeDtypeStruct((n, d), jnp.float32),
            jax.ShapeDtypeStruct((n, LANES), jnp.float32),
            jax.ShapeDtypeStruct((1, LANES), jnp.float32),
        ],
        compiler_params=_cparams("arbitrary"),
    )(h2, g.reshape(1, d), w_router, b_router)


def _experts_kernel(be_ref, tok_ref, nused_ref, x_hbm, wg_ref, wu_ref, wd_ref, o_ref, xbuf, sem, *, bm):
    b = pl.program_id(0)
    n_used = nused_ref[0]

    def row_copy(blk, slot, i):
        tok = tok_ref[blk * bm + i]
        return pltpu.make_async_copy(x_hbm.at[pl.ds(tok, 1), :], xbuf.at[slot, pl.ds(i, 1), :], sem.at[slot])

    def start_gather(blk, slot):
        def body(i, _):
            row_copy(blk, slot, i).start()
            return 0
        lax.fori_loop(0, bm, body, 0, unroll=GATHER_UNROLL)

    def wait_gather(slot):
        pltpu.make_async_copy(x_hbm.at[pl.ds(0, bm), :], xbuf.at[slot], sem.at[slot]).wait()

    slot = b % 2

    @pl.when((b == 0) & (n_used > 0))
    def _():
        start_gather(0, 0)

    @pl.when(b + 1 < n_used)
    def _():
        start_gather(b + 1, 1 - slot)

    @pl.when(b < n_used)
    def _():
        wait_gather(slot)
        x = xbuf[slot].astype(MXU_DTYPE)
        hg = jnp.dot(x, wg_ref[...], preferred_element_type=jnp.float32)
        hu = jnp.dot(x, wu_ref[...], preferred_element_type=jnp.float32)
        hid = (jax.nn.silu(hg) * hu).astype(MXU_DTYPE)
        o_ref[...] = jnp.dot(hid, wd_ref[...], preferred_element_type=jnp.float32)

    @pl.when(b >= n_used)
    def _():
        o_ref[...] = jnp.zeros_like(o_ref)


def _experts(xn2, block_e, buf_tok, n_used, w_gate, w_up, w_down, *, n_blocks):
    n, d = xn2.shape
    f = w_gate.shape[2]
    bm = MOE_BM
    grid_spec = pltpu.PrefetchScalarGridSpec(
        num_scalar_prefetch=3,
        grid=(n_blocks,),
        in_specs=[
            pl.BlockSpec(memory_space=pl.ANY),
            pl.BlockSpec((None, d, f), lambda b, be, tok, nu: (be[b], 0, 0)),
            pl.BlockSpec((None, d, f), lambda b, be, tok, nu: (be[b], 0, 0)),
            pl.BlockSpec((None, f, d), lambda b, be, tok, nu: (be[b], 0, 0)),
        ],
        out_specs=pl.BlockSpec((bm, d), lambda b, be, tok, nu: (b, 0)),
        scratch_shapes=[
            pltpu.VMEM((2, bm, d), jnp.float32),
            pltpu.SemaphoreType.DMA((2,)),
        ],
    )
    return pl.pallas_call(
        functools.partial(_experts_kernel, bm=bm),
        grid_spec=grid_spec,
        out_shape=jax.ShapeDtypeStruct((n_blocks * bm, d), jnp.float32),
        compiler_params=_cparams("arbitrary"),
    )(block_e, buf_tok, n_used, xn2, w_gate, w_up, w_down)


def _combine_kernel(pos_ref, y_hbm, h_ref, meta_ref, g_ref, o_ref, ybuf, sem, *, tm, n_steps):
    step = pl.program_id(0)

    def row_copy(blk, slot, i, k):
        pos = pos_ref[(blk * tm + i) * TOP_K + k]
        return pltpu.make_async_copy(y_hbm.at[pl.ds(pos, 1), :], ybuf.at[slot, k, pl.ds(i, 1), :], sem.at[slot])

    def start_gather(blk, slot):
        def body(i, _):
            for k in range(TOP_K):
                row_copy(blk, slot, i, k).start()
            return 0
        lax.fori_loop(0, tm, body, 0, unroll=GATHER_UNROLL)

    def wait_gather(slot):
        for k in range(TOP_K):
            pltpu.make_async_copy(y_hbm.at[pl.ds(0, tm), :], ybuf.at[slot, k], sem.at[slot]).wait()

    slot = step % 2

    @pl.when(step == 0)
    def _():
        start_gather(0, 0)

    @pl.when(step + 1 < n_steps)
    def _():
        start_gather(step + 1, 1 - slot)

    wait_gather(slot)
    w1 =meta_ref[:, META_W:META_W + 1]
    w2 = meta_ref[:, META_W + 1:META_W + 2]
    h = h_ref[...] + (ybuf[slot, 0] * w1 + ybuf[slot, 1] * w2)
    ms = jnp.mean(h * h, axis=-1, keepdims=True)
    o_ref[...] = (h * lax.rsqrt(ms + EPS)) * g_ref[...]


def _combine(pos, yb, h2, meta, g):
    n, d = h2.shape
    tm = min(COMBINE_TM, n)
    n_steps = n // tm
    grid_spec = pltpu.PrefetchScalarGridSpec(
        num_scalar_prefetch=1,
        grid=(n_steps,),
        in_specs=[
            pl.BlockSpec(memory_space=pl.ANY),
            pl.BlockSpec((tm, d), lambda i, pos: (i, 0)),
            pl.BlockSpec((tm, LANES), lambda i, pos: (i, 0)),
            pl.BlockSpec((1, d), lambda i, pos: (0, 0)),
        ],
        out_specs=pl.BlockSpec((tm, d), lambda i, pos: (i, 0)),
        scratch_shapes=[
            pltpu.VMEM((2, TOP_K, tm, d), jnp.float32),
            pltpu.SemaphoreType.DMA((2,)),
        ],
    )
    return pl.pallas_call(
        functools.partial(_combine_kernel, tm=tm, n_steps=n_steps),
        grid_spec=grid_spec,
        out_shape=jax.ShapeDtypeStruct((n, d), jnp.float32),
        compiler_params=_cparams("arbitrary"),
    )(pos, yb, h2, meta, g.reshape(1, d))


def _dispatch_plan(meta, counts_f, n):
    bm = MOE_BM
    a = n * TOP_K
    expert = meta[:, META_E:META_E + TOP_K].astype(jnp.int32)
    rank = meta[:, META_RANK:META_RANK + TOP_K].astype(jnp.int32)
    counts = counts_f[0, :N_EXPERTS].astype(jnp.int32)
    padded = (counts + bm - 1) // bm * bm
    pad_end = jnp.cumsum(padded)
    pad_start = pad_end - padded
    pos = (pad_start[expert] + rank).reshape(a)
    n_blocks = -(-a // bm) + N_EXPERTS
    tok = jnp.arange(a, dtype=jnp.int32) // TOP_K
    buf_tok = jnp.zeros((n_blocks * bm,), jnp.int32).at[pos].set(tok)
    block_row0 = jnp.arange(n_blocks, dtype=jnp.int32) * bm
    block_e = jnp.sum((pad_end[None, :] <= block_row0[:, None]).astype(jnp.int32), axis=1)
    block_e = jnp.minimum(block_e, N_EXPERTS - 1)
    n_used = (pad_end[-1] // bm).astype(jnp.int32).reshape(1)
    return pos, buf_tok, block_e, n_used, n_blocks


def _encoder(x, p):
    b, s, d = x.shape
    n = b * s
    d_rnn = p["d_rnn"]
    d_attn = p["d_attn"]
    x2 = x.reshape(n, d)
    proj2 = _norm_proj(x2, p["norm_mix_g"], p["w_in"])
    d_in = proj2.shape[1]
    proj3 = proj2.reshape(b, s, d_in)
    hf = _lru_direction(proj3, p["conv_w"], p["conv_b"], p["lru_wg"][0], p["lru_b_a"][0], p["lru_b_x"][0],
                        p["lru_lambda"][0], None, reverse=False, d_rnn=d_rnn)
    ya = _lru_direction(proj3, p["conv_w"], p["conv_b"], p["lru_wg"][1], p["lru_b_a"][1], p["lru_b_x"][1],
                        p["lru_lambda"][1], hf, reverse=True, d_rnn=d_rnn)
    att = _attention(proj3, p["att_table"], d_rnn=d_rnn, d_attn=d_attn)
    m2 = _merge(ya.reshape(n, d_rnn), att.reshape(n, d_attn), p["w_lru_up"], p["w_attn_up"], proj2,
                p["b_merge"], gate_col=2 * d_rnn + 3 * d_attn)
    h1 = _out_proj(m2, p["w_o"], x2)
    xn2, meta, counts = _router(h1, p["norm_ffn_g"], p["w_router"], p["b_router"])
    pos, buf_tok, block_e, n_used, n_blocks = _dispatch_plan(meta, counts, n)
    yb = _experts(xn2, block_e, buf_tok, n_used, p["w_exp_gate"], p["w_exp_up"], p["w_exp_down"],
                  n_blocks=n_blocks)
    y = _combine(pos, yb, h1, meta, p["norm_final_g"])
    return y.reshape(b, s, d)


def _prepare_params(norm_mix_g, w_in, conv_w, conv_b, lru_w_a, lru_b_a, lru_w_x, lru_b_x, lru_lambda, rpb,
                    w_lru_up, w_attn_up, b_merge, w_o, norm_ffn_g, w_router_group, b_router_group,
                    w_router_expert, b_router_expert, w_exp_gate, w_exp_up, w_exp_down, norm_final_g):
    bf16 = MXU_DTYPE
    d = w_in.shape[1]
    d_rnn = conv_w.shape[2]
    d_attn = w_attn_up.shape[1]
    lru_wg = jnp.concatenate([lru_w_a[0], lru_w_x[0]], axis=-1).astype(bf16)
    w_router = jnp.zeros((d, LANES), jnp.float32)
    w_router = w_router.at[:, :N_EXPERTS].set(w_router_expert[0])
    w_router = w_router.at[:, GROUP_LANE0:GROUP_LANE0 + N_GROUPS].set(w_router_group[0])
    b_router = jnp.zeros((1, LANES), jnp.float32)
    b_router = b_router.at[0, :N_EXPERTS].set(b_router_expert[0])
    b_router = b_router.at[0, GROUP_LANE0:GROUP_LANE0 + N_GROUPS].set(b_router_group[0])
    return dict(
        d_rnn=d_rnn, d_attn=d_attn,
        norm_mix_g=norm_mix_g[0], w_in=w_in[0].astype(bf16),
        conv_w=conv_w[0], conv_b=conv_b[0], lru_wg=lru_wg,
        lru_b_a=lru_b_a[0], lru_b_x=lru_b_x[0], lru_lambda=lru_lambda[0],
        att_table=_attention_bias_table(rpb[0]),
        w_lru_up=w_lru_up[0].astype(bf16), w_attn_up=w_attn_up[0].astype(bf16),
        b_merge=b_merge[0], w_o=w_o[0].astype(bf16),
        norm_ffn_g=norm_ffn_g[0], w_router=w_router, b_router=b_router,
        w_exp_gate=w_exp_gate[0].astype(bf16), w_exp_up=w_exp_up[0].astype(bf16),
        w_exp_down=w_exp_down[0].astype(bf16), norm_final_g=norm_final_g,
    )


def kernel(x_prompt, x_sample, norm_mix_g, w_in, conv_w, conv_b, lru_w_a, lru_b_a, lru_w_x, lru_b_x, lru_lambda, rpb, w_lru_up, w_attn_up, b_merge, w_o, norm_ffn_g, w_router_group, b_router_group, w_router_expert, b_router_expert, w_exp_gate, w_exp_up, w_exp_down, norm_final_g):
    assert norm_mix_g.shape[0] == 1, "single-layer encoder"
    p = _prepare_params(norm_mix_g, w_in, conv_w, conv_b, lru_w_a, lru_b_a, lru_w_x, lru_b_x, lru_lambda, rpb,
                        w_lru_up, w_attn_up, b_merge, w_o, norm_ffn_g, w_router_group, b_router_group,
                        w_router_expert, b_router_expert, w_exp_gate, w_exp_up, w_exp_down, norm_final_g)
    return (_encoder(x_prompt, p), _encoder(x_sample, p))
```

```python
import functools

import numpy as np
import jax
import jax.numpy as jnp
from jax import lax
from jax.experimental import pallas as pl
from jax.experimental.pallas import tpu as pltpu

GRID_W = 64
WIN_R = 8
WIN_C = 16
CONV_W = 4
LRU_C = 8.0
LRU_BLOCK = 128
HEAD_DIM = 128
N_GROUPS = 4
EXPERTS_PER_GROUP = 8
N_EXPERTS = N_GROUPS * EXPERTS_PER_GROUP
TOP_K = 2
EPS = 1e-6
MASK_VALUE = -1e30

MXU_DTYPE = jnp.bfloat16

LANES = 128
SUBLANES = 8
VMEM_LIMIT_BYTES = 56 * 1024 * 1024
EXPERTS_VMEM_LIMIT_BYTES = 60 * 1024 * 1024

PROJ_TM = 1024
PROJ_TN = 1024
MERGE_TM = 1024
MERGE_TN = 512
LRU_TC = 2048
LRU_GROUPS = 4
LRU_UNROLL = 4
ATT_ROWS = 8
ROUTER_TM = 512
MOE_BM = 256
COMBINE_TM = 256


def _cparams(*sem):
    return pltpu.CompilerParams(dimension_semantics=sem, vmem_limit_bytes=VMEM_LIMIT_BYTES)


def _gather_cparams(vmem_limit):
    return pltpu.CompilerParams(dimension_semantics=("arbitrary",), vmem_limit_bytes=vmem_limit)


def _norm_proj_kernel(x_ref, g_ref, w_ref, o_ref, xn_ref):
    @pl.when(pl.program_id(1) == 0)
    def _():
        x = x_ref[...]
        ms = jnp.mean(x * x, axis=-1, keepdims=True)
        xn_ref[...] = ((x * lax.rsqrt(ms + EPS)) * g_ref[...]).astype(MXU_DTYPE)

    o_ref[...] = jnp.dot(xn_ref[...], w_ref[...], preferred_element_type=jnp.float32)


def _norm_proj(x2, g, w_bf16):
    n, d = x2.shape
    d_out = w_bf16.shape[1]
    tm = min(PROJ_TM, n)
    tn = min(PROJ_TN, d_out)
    return pl.pallas_call(
        _norm_proj_kernel,
        grid=(n // tm, d_out // tn),
        in_specs=[
            pl.BlockSpec((tm, d), lambda i, j: (i, 0)),
            pl.BlockSpec((1, d), lambda i, j: (0, 0)),
            pl.BlockSpec((d, tn), lambda i, j: (0, j)),
        ],
        out_specs=pl.BlockSpec((tm, tn), lambda i, j: (i, j)),
        out_shape=jax.ShapeDtypeStruct((n, d_out), jnp.float32),
        scratch_shapes=[pltpu.VMEM((tm, d), MXU_DTYPE)],
        compiler_params=_cparams("parallel", "arbitrary"),
    )(x2, g.reshape(1, d), w_bf16)


def _softplus(y):
    return jnp.maximum(y, 0.0) + jnp.log1p(jnp.exp(-jnp.abs(y)))


def _lru_pitch(tc):
    seg_len = tc // (LRU_GROUPS * SUBLANES)
    assert seg_len * LRU_GROUPS * SUBLANES == tc and seg_len % SUBLANES == 0
    return seg_len + SUBLANES // 2


def _lru_kernel(*refs, reverse, n_chunks, tc):
    if reverse:
        (x_ref, xp_ref, xn_ref, cw_ref, cb_ref, wg_ref, ba_ref, bx_ref, lam_ref,
         hf_ref, gate_ref, o_ref, xpad_ref, a_ref, u_ref, h_ref, carry_ref) = refs
    else:
        (x_ref, xp_ref, xn_ref, cw_ref, cb_ref, wg_ref, ba_ref, bx_ref, lam_ref,
         o_ref, xpad_ref, a_ref, u_ref, h_ref, carry_ref) = refs
    t = pl.program_id(2)
    chunk = (n_chunks - 1 - t) if reverse else t

    @pl.when(t == 0)
    def _():
        carry_ref[...] = jnp.zeros_like(carry_ref)

    prev = jnp.where(chunk == 0, 0.0, xp_ref[...])
    nxt = jnp.where(chunk == n_chunks - 1, 0.0, xn_ref[...])
    xpad_ref[pl.ds(0, SUBLANES), :] = prev
    xpad_ref[pl.ds(SUBLANES, tc), :] = x_ref[...]
    xpad_ref[pl.ds(SUBLANES + tc, SUBLANES), :] = nxt
    left = CONV_W // 2
    xc = cb_ref[...]
    for tap in range(CONV_W):
        xc = xc + xpad_ref[pl.ds(SUBLANES - left + tap, tc), :] * cw_ref[pl.ds(tap, 1), :]

    z = jnp.dot(xc.astype(MXU_DTYPE), wg_ref[...], preferred_element_type=jnp.float32)
    r = jax.nn.sigmoid(z[:, :LRU_BLOCK] + ba_ref[...])
    i = jax.nn.sigmoid(z[:, LRU_BLOCK:] + bx_ref[...])
    log_a = (-LRU_C * r) * _softplus(-lam_ref[...])
    a = jnp.exp(log_a)
    u = jnp.sqrt(jnp.tanh(-log_a) * (1.0 + a * a)) * (i * xc)

    pitch = _lru_pitch(tc)
    slots = LRU_GROUPS * SUBLANES * pitch
    a_ref[pl.ds(0, tc), :] = a
    u_ref[pl.ds(0, tc), :] = u
    a_ref[pl.ds(tc, slots - tc), :] = jnp.ones((slots - tc, LRU_BLOCK), jnp.float32)
    u_ref[pl.ds(tc, slots - tc), :] = jnp.zeros((slots - tc, LRU_BLOCK), jnp.float32)

    def seg_rows(g, k):
        return pl.ds(g * SUBLANES * pitch + k, SUBLANES, stride=pitch)

    def slot_of(kk):
        return (pitch - 1 - kk) if reverse else kk

    def scan_step(kk, carry):
        hs, ps = carry
        k = slot_of(kk)
        new_h, new_p = [], []
        for g in range(LRU_GROUPS):
            a_k = a_ref[seg_rows(g, k), :]
            h = a_k * hs[g] + u_ref[seg_rows(g, k), :]
            h_ref[seg_rows(g, k), :] = h
            new_h.append(h)
            new_p.append(a_k * ps[g])
        return tuple(new_h), tuple(new_p)

    zeros = tuple(jnp.zeros((SUBLANES, LANES), jnp.float32) for _ in range(LRU_GROUPS))
    ones = tuple(jnp.ones((SUBLANES, LANES), jnp.float32) for _ in range(LRU_GROUPS))
    h_end, p_end = lax.fori_loop(0, pitch, scan_step, (zeros, ones), unroll=LRU_UNROLL)

    row = lax.broadcasted_iota(jnp.int32, (SUBLANES, LANES), 0)
    c_in = [None] * LRU_GROUPS
    cg = carry_ref[...]
    for g in (reversed(range(LRU_GROUPS)) if reverse else range(LRU_GROUPS)):
        pa, hb = p_end[g], h_end[g]
        for s in (1, 2, 4):
            keep = (row < SUBLANES - s) if reverse else (row >= s)
            shift = (SUBLANES - s) if reverse else s
            a_sh = jnp.where(keep, pltpu.roll(pa, shift, 0), 1.0)
            b_sh = jnp.where(keep, pltpu.roll(hb, shift, 0), 0.0)
            hb = pa * b_sh + hb
            pa = pa * a_sh
        seg_out = pa * cg + hb
        if reverse:
            c_in[g] = jnp.where(row == SUBLANES - 1, cg, pltpu.roll(seg_out, SUBLANES - 1, 0))
            cg = seg_out[0:1, :]
        else:
            c_in[g] = jnp.where(row == 0, cg, pltpu.roll(seg_out, 1, 0))
            cg = seg_out[SUBLANES - 1:SUBLANES, :]
    carry_ref[...] = cg

    def fix_step(kk, ps):
        k = slot_of(kk)
        new_p = []
        for g in range(LRU_GROUPS):
            p = a_ref[seg_rows(g, k), :] * ps[g]
            u_ref[seg_rows(g, k), :] = h_ref[seg_rows(g, k), :] + p * c_in[g]
            new_p.append(p)
        return tuple(new_p)

    lax.fori_loop(0, pitch, fix_step, ones, unroll=LRU_UNROLL)

    h_all = u_ref[pl.ds(0, tc), :]
    if reverse:
        o_ref[...] = (jax.nn.gelu(gate_ref[...]) * (hf_ref[...] + h_all)).astype(o_ref.dtype)
    else:
        o_ref[...] = h_all


def _lru_direction(proj3, conv_w, conv_b, wg, ba, bx, lam, hf, *, reverse, d_rnn):
    b, s, _ = proj3.shape
    n_cb = d_rnn // LRU_BLOCK
    tc = min(LRU_TC, s)
    n_chunks = s // tc
    tc8 = tc // SUBLANES
    s8 = s // SUBLANES
    slots = LRU_GROUPS * SUBLANES * _lru_pitch(tc)

    def chunk_of(t):
        return (n_chunks - 1 - t) if reverse else t

    vec_spec = pl.BlockSpec((1, LRU_BLOCK), lambda bi, c, t: (0, c))
    in_specs = [
        pl.BlockSpec((None, tc, LRU_BLOCK), lambda bi, c, t: (bi, chunk_of(t), c)),
        pl.BlockSpec((None, SUBLANES, LRU_BLOCK),
                     lambda bi, c, t: (bi, jnp.maximum(chunk_of(t) * tc8 - 1, 0), c)),
        pl.BlockSpec((None, SUBLANES, LRU_BLOCK),
                     lambda bi, c, t: (bi, jnp.minimum((chunk_of(t) + 1) * tc8, s8 - 1), c)),
        pl.BlockSpec((CONV_W, LRU_BLOCK), lambda bi, c, t: (0, c)),
        vec_spec,
        pl.BlockSpec((None, LRU_BLOCK, 2 * LRU_BLOCK), lambda bi, c, t: (c, 0, 0)),
        vec_spec, vec_spec, vec_spec,
    ]
    args = [proj3, proj3, proj3, conv_w, conv_b.reshape(1, d_rnn), wg,
            ba.reshape(1, d_rnn), bx.reshape(1, d_rnn), lam.reshape(1, d_rnn)]
    if reverse:
        in_specs += [
            pl.BlockSpec((None, tc, LRU_BLOCK), lambda bi, c, t: (bi, chunk_of(t), c)),
            pl.BlockSpec((None, tc, LRU_BLOCK), lambda bi, c, t: (bi, chunk_of(t), n_cb + c)),
        ]
        args += [hf, proj3]
        out_dtype = MXU_DTYPE
    else:
        out_dtype = jnp.float32
    return pl.pallas_call(
        functools.partial(_lru_kernel, reverse=reverse, n_chunks=n_chunks, tc=tc),
        grid=(b, n_cb, n_chunks),
        in_specs=in_specs,
        out_specs=pl.BlockSpec((None, tc, LRU_BLOCK), lambda bi, c, t: (bi, chunk_of(t), c)),
        out_shape=jax.ShapeDtypeStruct((b, s, d_rnn), out_dtype),
        scratch_shapes=[
            pltpu.VMEM((tc + 2 * SUBLANES, LRU_BLOCK), jnp.float32),
            pltpu.VMEM((slots, LRU_BLOCK), jnp.float32),
            pltpu.VMEM((slots, LRU_BLOCK), jnp.float32),
            pltpu.VMEM((slots, LRU_BLOCK), jnp.float32),
            pltpu.VMEM((1, LRU_BLOCK), jnp.float32),
        ],
        compiler_params=_cparams("parallel", "parallel", "arbitrary"),
    )(*args)


def _attention_bias_table(rpb):
    n_heads, n_dr, n_dc = rpb.shape
    cq = np.arange(GRID_W)
    cs = np.clip(cq - WIN_C // 2, 0, GRID_W - WIN_C)
    ck = np.arange(GRID_W)
    valid = (ck[None, :] >= cs[:, None]) & (ck[None, :] < cs[:, None] + WIN_C)
    pad = GRID_W - WIN_C
    width = n_dc + 2 * pad + 1
    padded = jnp.pad(rpb.astype(jnp.float32), ((0, 0), (0, 0), (pad, pad + 1)))
    tiled = jnp.tile(padded, (1, 1, GRID_W))[:, :, :GRID_W * (width - 1)]
    skew = tiled.reshape(n_heads, n_dr, GRID_W, width - 1)
    toep = skew[:, :, :, GRID_W - 1:2 * GRID_W - 1]
    toep = jnp.where(valid[None, None], toep, MASK_VALUE)
    per_off = [toep[:, WIN_R - 1 - off:2 * WIN_R - 1 - off] for off in range(WIN_R)]
    tab = jnp.stack(per_off, axis=1)
    tab = tab.transpose(0, 1, 3, 2, 4)
    return tab.reshape(n_heads, WIN_R, GRID_W, WIN_R * GRID_W)


def _attention_kernel(q_ref, k_ref, v_ref, tb_ref, o_ref, kb_ref, vb_ref, *, rows):
    kb_ref[...] = k_ref[...].astype(MXU_DTYPE)
    vb_ref[...] = v_ref[...].astype(MXU_DTYPE)
    scale = HEAD_DIM ** -0.5
    band = WIN_R * GRID_W

    def tile_body(i, _):
        starts, scores = [], []
        for rl in range(ATT_ROWS):
            r = i * ATT_ROWS + rl
            rs = jnp.clip(r - WIN_R // 2, 0, rows - WIN_R)
            start = pl.multiple_of(rs * GRID_W, GRID_W)
            q0 = pl.multiple_of(r * GRID_W, GRID_W)
            q = (q_ref[pl.ds(q0, GRID_W), :] * scale).astype(MXU_DTYPE)
            kband = kb_ref[pl.ds(start, band), :]
            s = lax.dot_general(q, kband, (((1,), (1,)), ((), ())), preferred_element_type=jnp.float32)
            starts.append(start)
            scores.append(s + tb_ref[r - rs])
        probs, denoms = [], []
        for s in scores:
            m = jnp.max(s, axis=-1, keepdims=True)
            p = jnp.exp(s - m)
            denoms.append(jnp.sum(p, axis=-1, keepdims=True))
            probs.append(p.astype(MXU_DTYPE))
        for rl in range(ATT_ROWS):
            vband = vb_ref[pl.ds(starts[rl], band), :]
            o = jnp.dot(probs[rl], vband, preferred_element_type=jnp.float32)
            q0 = pl.multiple_of((i * ATT_ROWS + rl) * GRID_W, GRID_W)
            o_ref[pl.ds(q0, GRID_W), :] = (o / denoms[rl]).astype(o_ref.dtype)
        return 0

    lax.fori_loop(0, rows // ATT_ROWS, tile_body, 0)


def _attention(proj3, table, *, d_rnn, d_attn):
    b, s, _ = proj3.shape
    rows = s // GRID_W
    n_heads = d_attn // HEAD_DIM
    q_blk = 2 * d_rnn // HEAD_DIM
    k_blk = q_blk + n_heads
    v_blk = k_blk + n_heads
    return pl.pallas_call(
        functools.partial(_attention_kernel, rows=rows),
        grid=(b, n_heads),
        in_specs=[
            pl.BlockSpec((None, s, HEAD_DIM), lambda bi, h: (bi, 0, q_blk + h)),
            pl.BlockSpec((None, s, HEAD_DIM), lambda bi, h: (bi, 0, k_blk + h)),
            pl.BlockSpec((None, s, HEAD_DIM), lambda bi, h: (bi, 0, v_blk + h)),
            pl.BlockSpec((None, WIN_R, GRID_W, WIN_R * GRID_W), lambda bi, h: (h, 0, 0, 0)),
        ],
        out_specs=pl.BlockSpec((None, s, HEAD_DIM), lambda bi, h: (bi, 0, h)),
        out_shape=jax.ShapeDtypeStruct((b, s, d_attn), MXU_DTYPE),
        scratch_shapes=[
            pltpu.VMEM((s, HEAD_DIM), MXU_DTYPE),
            pltpu.VMEM((s, HEAD_DIM), MXU_DTYPE),
        ],
        compiler_params=_cparams("parallel", "parallel"),
    )(proj3, proj3, proj3, table)


def _merge_kernel(ya_ref, att_ref, wl_ref, wa_ref, gl_ref, ga_ref, bm_ref, o_ref):
    y_lru = jnp.dot(ya_ref[...], wl_ref[...], preferred_element_type=jnp.float32)
    y_att = jnp.dot(att_ref[...], wa_ref[...], preferred_element_type=jnp.float32)
    s_lru = jax.nn.sigmoid(gl_ref[...] + bm_ref[pl.ds(0, 1), :])
    s_att = jax.nn.sigmoid(ga_ref[...] + bm_ref[pl.ds(1, 1), :])
    o_ref[...] = (s_lru * y_lru + s_att * y_att).astype(o_ref.dtype)


def _merge(ya2, att2, wl, wa, proj2, b_merge, *, gate_col):
    n, d = ya2.shape
    tm = min(MERGE_TM, n)
    tn = min(MERGE_TN, d)
    gl_blk = gate_col // tn
    ga_blk = (gate_col + d) // tn
    return pl.pallas_call(
        _merge_kernel,
        grid=(n // tm, d // tn),
        in_specs=[
            pl.BlockSpec((tm, d), lambda i, j: (i, 0)),
            pl.BlockSpec((tm, d), lambda i, j: (i, 0)),
            pl.BlockSpec((d, tn), lambda i, j: (0, j)),
            pl.BlockSpec((d, tn), lambda i, j: (0, j)),
            pl.BlockSpec((tm, tn), lambda i, j: (i, gl_blk + j)),
            pl.BlockSpec((tm, tn), lambda i, j: (i, ga_blk + j)),
            pl.BlockSpec((2, tn), lambda i, j: (0, j)),
        ],
        out_specs=pl.BlockSpec((tm, tn), lambda i, j: (i, j)),
        out_shape=jax.ShapeDtypeStruct((n, d), MXU_DTYPE),
        compiler_params=_cparams("parallel", "arbitrary"),
    )(ya2, att2, wl, wa, proj2, proj2, b_merge)


def _out_proj_kernel(m_ref, w_ref, x_ref, o_ref):
    o_ref[...] = x_ref[...] + jnp.dot(m_ref[...], w_ref[...], preferred_element_type=jnp.float32)


def _out_proj(m2, w_o, x2):
    n, d = m2.shape
    tm = min(MERGE_TM, n)
    tn = min(MERGE_TN, d)
    return pl.pallas_call(
        _out_proj_kernel,
        grid=(n // tm, d // tn),
        in_specs=[
            pl.BlockSpec((tm, d), lambda i, j: (i, 0)),
            pl.BlockSpec((d, tn), lambda i, j: (0, j)),
            pl.BlockSpec((tm, tn), lambda i, j: (i, j)),
        ],
        out_specs=pl.BlockSpec((tm, tn), lambda i, j: (i, j)),
        out_shape=jax.ShapeDtypeStruct((n, d), jnp.float32),
        compiler_params=_cparams("parallel", "arbitrary"),
    )(m2, w_o, x2)


GROUP_LANE0 = N_EXPERTS
META_E, META_W, META_RANK = 0, 2, 4
GROUP_SHIFT = EXPERTS_PER_GROUP.bit_length() - 1


def _router_kernel(ha_ref, hb_ref, g_ref, w_ref, b_ref, xn_ref, meta_ref, cnt_ref, *, steps_a):
    step = pl.program_id(0)

    @pl.when(step == 0)
    def _():
        cnt_ref[...] = jnp.zeros_like(cnt_ref)

    x = jnp.where(step < steps_a, ha_ref[...], hb_ref[...])
    tm = x.shape[0]
    ms = jnp.mean(x * x, axis=-1, keepdims=True)
    xn = (x * lax.rsqrt(ms + EPS)) * g_ref[...]
    xn_ref[...] = xn
    logits = jnp.dot(xn, w_ref[...], preferred_element_type=jnp.float32,
                     precision=lax.Precision.HIGHEST) + b_ref[...]
    lane = lax.broadcasted_iota(jnp.int32, (tm, LANES), 1)
    neg_inf = -jnp.inf

    def first_argmax(vals, vmax):
        return jnp.min(jnp.where(vals == vmax, lane, LANES), axis=-1, keepdims=True)

    is_group = (lane >= GROUP_LANE0) & (lane < GROUP_LANE0 + N_GROUPS)
    lg = jnp.where(is_group, logits, neg_inf)
    mg = jnp.max(lg, axis=-1, keepdims=True)
    g_idx = first_argmax(lg, mg) - GROUP_LANE0
    g_val = 1.0 / jnp.sum(jnp.exp(lg - mg), axis=-1, keepdims=True)

    in_group = (lane < N_EXPERTS) & (lax.shift_right_logical(lane, GROUP_SHIFT) == g_idx)
    le = jnp.where(in_group, logits, neg_inf)
    m1 = jnp.max(le, axis=-1, keepdims=True)
    e1 = first_argmax(le, m1)
    le2 = jnp.where(lane == e1, neg_inf, le)
    m2 = jnp.max(le2, axis=-1, keepdims=True)
    e2 = first_argmax(le2, m2)
    z = jnp.sum(jnp.exp(le - m1), axis=-1, keepdims=True)
    p1 = 1.0 / z
    p2 = jnp.exp(m2 - m1) / z
    den = p1 + p2
    w1 = g_val * (p1 / den)
    w2 = g_val * (p2 / den)

    hot1 = lane == e1
    hot2 = lane == e2
    hot = hot1.astype(jnp.float32) + hot2.astype(jnp.float32)
    r_i = lax.broadcasted_iota(jnp.int32, (tm, tm), 0)
    c_i = lax.broadcasted_iota(jnp.int32, (tm, tm), 1)
    lower = (c_i < r_i).astype(MXU_DTYPE)
    before = jnp.dot(lower, hot.astype(MXU_DTYPE), preferred_element_type=jnp.float32) + cnt_ref[...]
    rank1 = jnp.sum(jnp.where(hot1, before, 0.0), axis=-1, keepdims=True)
    rank2 = jnp.sum(jnp.where(hot2, before, 0.0), axis=-1, keepdims=True)
    cnt_ref[...] += jnp.sum(hot, axis=0, keepdims=True)

    meta = jnp.where(lane == META_E, e1.astype(jnp.float32), 0.0)
    meta = jnp.where(lane == META_E + 1, e2.astype(jnp.float32), meta)
    meta = jnp.where(lane == META_W, w1, meta)
    meta = jnp.where(lane == META_W + 1, w2, meta)
    meta = jnp.where(lane == META_RANK, rank1, meta)
    meta = jnp.where(lane == META_RANK + 1, rank2, meta)
    meta_ref[...] = meta


def _router(h_a, h_b, g, w_router, b_router):
    n_a, d = h_a.shape
    n_b = h_b.shape[0]
    n = n_a + n_b
    tm = min(ROUTER_TM, n_a, n_b)
    steps_a = n_a // tm
    return pl.pallas_call(
        functools.partial(_router_kernel, steps_a=steps_a),
        grid=(n // tm,),
        in_specs=[
            pl.BlockSpec((tm, d), lambda i: (jnp.minimum(i, steps_a - 1), 0)),
            pl.BlockSpec((tm, d), lambda i: (jnp.maximum(i - steps_a, 0), 0)),
            pl.BlockSpec((1, d), lambda i: (0, 0)),
            pl.BlockSpec((d, LANES), lambda i: (0, 0)),
            pl.BlockSpec((1, LANES), lambda i: (0, 0)),
        ],
        out_specs=[
            pl.BlockSpec((tm, d), lambda i: (i, 0)),
            pl.BlockSpec((tm, LANES), lambda i: (i, 0)),
            pl.BlockSpec((1, LANES), lambda i: (0, 0)),
        ],
        out_shape=[
            jax.ShapeDtypeStruct((n, d), jnp.float32),
            jax.ShapeDtypeStruct((n, LANES), jnp.float32),
            jax.ShapeDtypeStruct((1, LANES), jnp.float32),
        ],
        compiler_params=_cparams("arbitrary"),
    )(h_a, h_b, g.reshape(1, d), w_router, b_router)


def _experts_kernel(be_ref, tok_ref, nused_ref, x_hbm, wg_ref, wu_ref, wd_ref, o_ref, xbuf, sem, *, bm):
    b = pl.program_id(0)
    n_used = nused_ref[0]

    def start_gather(blk, slot):
        def body(i8, _):
            row0 = pl.multiple_of(i8 * SUBLANES, SUBLANES)
            for k in range(SUBLANES):
                tok = tok_ref[blk * bm + row0 + k]
                pltpu.make_async_copy(x_hbm.at[pl.ds(tok, 1), :], xbuf.at[slot, pl.ds(row0 + k, 1), :],
                                      sem.at[slot]).start()
            return 0
        lax.fori_loop(0, bm // SUBLANES, body, 0)

    def wait_gather(slot):
        pltpu.make_async_copy(x_hbm.at[pl.ds(0, bm), :], xbuf.at[slot], sem.at[slot]).wait()

    slot = b % 2

    @pl.when((b == 0) & (n_used > 0))
    def _():
        start_gather(0, 0)

    @pl.when(b + 1 < n_used)
    def _():
        start_gather(b + 1, 1 - slot)

    @pl.when(b < n_used)
    def _():
        wait_gather(slot)
        x = xbuf[slot].astype(MXU_DTYPE)
        hg = jnp.dot(x, wg_ref[...].astype(MXU_DTYPE), preferred_element_type=jnp.float32)
        hu = jnp.dot(x, wu_ref[...].astype(MXU_DTYPE), preferred_element_type=jnp.float32)
        hid = (jax.nn.silu(hg) * hu).astype(MXU_DTYPE)
        o_ref[...] = jnp.dot(hid, wd_ref[...].astype(MXU_DTYPE), preferred_element_type=jnp.float32)

    @pl.when(b >= n_used)
    def _():
        o_ref[...] = jnp.zeros_like(o_ref)


def _experts(xn2, block_e, buf_tok, n_used, w_gate, w_up, w_down, *, n_blocks):
    n, d = xn2.shape
    f = w_gate.shape[2]
    bm = MOE_BM
    grid_spec = pltpu.PrefetchScalarGridSpec(
        num_scalar_prefetch=3,
        grid=(n_blocks,),
        in_specs=[
            pl.BlockSpec(memory_space=pl.ANY),
            pl.BlockSpec((None, d, f), lambda b, be, tok, nu: (be[b], 0, 0)),
            pl.BlockSpec((None, d, f), lambda b, be, tok, nu: (be[b], 0, 0)),
            pl.BlockSpec((None, f, d), lambda b, be, tok, nu: (be[b], 0, 0), pipeline_mode=pl.Buffered(1)),
        ],
        out_specs=pl.BlockSpec((bm, d), lambda b, be, tok, nu: (b, 0)),
        scratch_shapes=[
            pltpu.VMEM((2, bm, d), jnp.float32),
            pltpu.SemaphoreType.DMA((2,)),
        ],
    )
    return pl.pallas_call(
        functools.partial(_experts_kernel, bm=bm),
        grid_spec=grid_spec,
        out_shape=jax.ShapeDtypeStruct((n_blocks * bm, d), jnp.float32),
        compiler_params=_gather_cparams(EXPERTS_VMEM_LIMIT_BYTES),
    )(block_e, buf_tok, n_used, xn2, w_gate, w_up, w_down)


def _combine_kernel(pos_ref, y_hbm, h_ref, meta_ref, g_ref, o_ref, ybuf, sem, *, tm, n_steps):
    step = pl.program_id(0)

    def start_gather(blk, slot):
        def body(i8, _):
            row0 = pl.multiple_of(i8 * SUBLANES, SUBLANES)
            for j in range(SUBLANES):
                for k in range(TOP_K):
                    pos = pos_ref[(blk * tm + row0 + j) * TOP_K + k]
                    pltpu.make_async_copy(y_hbm.at[pl.ds(pos, 1), :], ybuf.at[slot, k, pl.ds(row0 + j, 1), :],
                                          sem.at[slot]).start()
            return 0
        lax.fori_loop(0, tm // SUBLANES, body, 0)

    def wait_gather(slot):
        for k in range(TOP_K):
            pltpu.make_async_copy(y_hbm.at[pl.ds(0, tm), :], ybuf.at[slot, k], sem.at[slot]).wait()

    slot = step % 2

    @pl.when(step == 0)
    def _():
        start_gather(0, 0)

    @pl.when(step + 1 < n_steps)
    def _():
        start_gather(step + 1, 1 - slot)

    wait_gather(slot)
    w1 =meta_ref[:, META_W:META_W + 1]
    w2 = meta_ref[:, META_W + 1:META_W + 2]
    h = h_ref[...] + (ybuf[slot, 0] * w1 + ybuf[slot, 1] * w2)
    ms = jnp.mean(h * h, axis=-1, keepdims=True)
    o_ref[...] = (h * lax.rsqrt(ms + EPS)) * g_ref[...]


def _combine(pos, yb, h2, meta, g):
    n, d = h2.shape
    tm = min(COMBINE_TM, n)
    n_steps = n // tm
    grid_spec = pltpu.PrefetchScalarGridSpec(
        num_scalar_prefetch=1,
        grid=(n_steps,),
        in_specs=[
            pl.BlockSpec(memory_space=pl.ANY),
            pl.BlockSpec((tm, d), lambda i, pos: (i, 0)),
            pl.BlockSpec((tm, LANES), lambda i, pos: (i, 0)),
            pl.BlockSpec((1, d), lambda i, pos: (0, 0)),
        ],
        out_specs=pl.BlockSpec((tm, d), lambda i, pos: (i, 0)),
        scratch_shapes=[
            pltpu.VMEM((2, TOP_K, tm, d), jnp.float32),
            pltpu.SemaphoreType.DMA((2,)),
        ],
    )
    return pl.pallas_call(
        functools.partial(_combine_kernel, tm=tm, n_steps=n_steps),
        grid_spec=grid_spec,
        out_shape=jax.ShapeDtypeStruct((n, d), jnp.float32),
        compiler_params=_gather_cparams(VMEM_LIMIT_BYTES),
    )(pos, yb, h2, meta, g.reshape(1, d))


def _dispatch_plan(meta, counts_f, n):
    bm = MOE_BM
    a = n * TOP_K
    expert = meta[:, META_E:META_E + TOP_K].astype(jnp.int32)
    rank = meta[:, META_RANK:META_RANK + TOP_K].astype(jnp.int32)
    counts = counts_f[0, :N_EXPERTS].astype(jnp.int32)
    padded = (counts + bm - 1) // bm * bm
    pad_end = jnp.cumsum(padded)
    pad_start = pad_end - padded
    pos = (pad_start[expert] + rank).reshape(a)
    n_blocks = -(-a // bm) + N_EXPERTS
    tok = jnp.arange(a, dtype=jnp.int32) // TOP_K
    buf_tok = jnp.zeros((n_blocks * bm,), jnp.int32).at[pos].set(tok)
    block_row0 = jnp.arange(n_blocks, dtype=jnp.int32) * bm
    block_e = jnp.sum((pad_end[None, :] <= block_row0[:, None]).astype(jnp.int32), axis=1)
    block_e = jnp.minimum(block_e, N_EXPERTS - 1)
    n_used = (pad_end[-1] // bm).astype(jnp.int32).reshape(1)
    return pos, buf_tok, block_e, n_used, n_blocks


def _mixer_residual(x, p):
    b, s, d = x.shape
    n = b * s
    d_rnn = p["d_rnn"]
    d_attn = p["d_attn"]
    x2 = x.reshape(n, d)
    proj2 = _norm_proj(x2, p["norm_mix_g"], p["w_in"])
    d_in = proj2.shape[1]
    proj3 = proj2.reshape(b, s, d_in)
    hf = _lru_direction(proj3, p["conv_w"], p["conv_b"], p["lru_wg"][0], p["lru_b_a"][0], p["lru_b_x"][0],
                        p["lru_lambda"][0], None, reverse=False, d_rnn=d_rnn)
    ya = _lru_direction(proj3, p["conv_w"], p["conv_b"], p["lru_wg"][1], p["lru_b_a"][1], p["lru_b_x"][1],
                        p["lru_lambda"][1], hf, reverse=True, d_rnn=d_rnn)
    att = _attention(proj3, p["att_table"], d_rnn=d_rnn, d_attn=d_attn)
    m2 = _merge(ya.reshape(n, d_rnn), att.reshape(n, d_attn), p["w_lru_up"], p["w_attn_up"], proj2,
                p["b_merge"], gate_col=2 * d_rnn + 3 * d_attn)
    return _out_proj(m2, p["w_o"], x2)


def _moe_and_final_norm(h_a, h_b, p):
    n_a, n_b = h_a.shape[0], h_b.shape[0]
    xn2, meta, counts = _router(h_a, h_b, p["norm_ffn_g"], p["w_router"], p["b_router"])
    pos, buf_tok, block_e, n_used, n_blocks = _dispatch_plan(meta, counts, n_a + n_b)
    yb = _experts(xn2, block_e, buf_tok, n_used, p["w_exp_gate"], p["w_exp_up"], p["w_exp_down"],
                  n_blocks=n_blocks)
    y_a = _combine(pos[:n_a * TOP_K], yb, h_a, meta[:n_a], p["norm_final_g"])
    y_b = _combine(pos[n_a * TOP_K:], yb, h_b, meta[n_a:], p["norm_final_g"])
    return y_a, y_b


def _prepare_params(norm_mix_g, w_in, conv_w, conv_b, lru_w_a, lru_b_a, lru_w_x, lru_b_x, lru_lambda, rpb,
                    w_lru_up, w_attn_up, b_merge, w_o, norm_ffn_g, w_router_group, b_router_group,
                    w_router_expert, b_router_expert, w_exp_gate, w_exp_up, w_exp_down, norm_final_g):
    bf16 = MXU_DTYPE
    d = w_in.shape[1]
    d_rnn = conv_w.shape[2]
    d_attn = w_attn_up.shape[1]
    lru_wg = jnp.concatenate([lru_w_a[0], lru_w_x[0]], axis=-1).astype(bf16)
    w_router = jnp.zeros((d, LANES), jnp.float32)
    w_router = w_router.at[:, :N_EXPERTS].set(w_router_expert[0])
    w_router = w_router.at[:, GROUP_LANE0:GROUP_LANE0 + N_GROUPS].set(w_router_group[0])
    b_router = jnp.zeros((1, LANES), jnp.float32)
    b_router = b_router.at[0, :N_EXPERTS].set(b_router_expert[0])
    b_router = b_router.at[0, GROUP_LANE0:GROUP_LANE0 + N_GROUPS].set(b_router_group[0])
    return dict(
        d_rnn=d_rnn, d_attn=d_attn,
        norm_mix_g=norm_mix_g[0], w_in=w_in[0].astype(bf16),
        conv_w=conv_w[0], conv_b=conv_b[0], lru_wg=lru_wg,
        lru_b_a=lru_b_a[0], lru_b_x=lru_b_x[0], lru_lambda=lru_lambda[0],
        att_table=_attention_bias_table(rpb[0]),
        w_lru_up=w_lru_up[0].astype(bf16), w_attn_up=w_attn_up[0].astype(bf16),
        b_merge=b_merge[0], w_o=w_o[0].astype(bf16),
        norm_ffn_g=norm_ffn_g[0], w_router=w_router, b_router=b_router,
        w_exp_gate=w_exp_gate[0], w_exp_up=w_exp_up[0], w_exp_down=w_exp_down[0],
        norm_final_g=norm_final_g,
    )


def kernel(x_prompt, x_sample, norm_mix_g, w_in, conv_w, conv_b, lru_w_a, lru_b_a, lru_w_x, lru_b_x, lru_lambda, rpb, w_lru_up, w_attn_up, b_merge, w_o, norm_ffn_g, w_router_group, b_router_group, w_router_expert, b_router_expert, w_exp_gate, w_exp_up, w_exp_down, norm_final_g):
    assert norm_mix_g.shape[0] == 1, "single-layer encoder"
    p = _prepare_params(norm_mix_g, w_in, conv_w, conv_b, lru_w_a, lru_b_a, lru_w_x, lru_b_x, lru_lambda, rpb,
                        w_lru_up, w_attn_up, b_merge, w_o, norm_ffn_g, w_router_group, b_router_group,
                        w_router_expert, b_router_expert, w_exp_gate, w_exp_up, w_exp_down, norm_final_g)
    h_prompt = _mixer_residual(x_prompt, p)
    h_sample = _mixer_residual(x_sample, p)
    y_prompt, y_sample = _moe_and_final_norm(h_prompt, h_sample, p)
    return (y_prompt.reshape(x_prompt.shape), y_sample.reshape(x_sample.shape))
```

```python
import functools

import numpy as np
import jax
import jax.numpy as jnp
from jax import lax
from jax.experimental import pallas as pl
from jax.experimental.pallas import tpu as pltpu

GRID_W = 64
WIN_R = 8
WIN_C = 16
CONV_W = 4
LRU_C = 8.0
LRU_BLOCK = 128
HEAD_DIM = 128
N_GROUPS = 4
EXPERTS_PER_GROUP = 8
N_EXPERTS = N_GROUPS * EXPERTS_PER_GROUP
TOP_K = 2
EPS = 1e-6
MASK_VALUE = -1e30

MXU_DTYPE = jnp.bfloat16

LANES = 128
SUBLANES = 8
VMEM_LIMIT_BYTES = 56 * 1024 * 1024
EXPERTS_VMEM_LIMIT_BYTES = 62 * 1024 * 1024

PROJ_TM = 1024
PROJ_TN = 1024
MERGE_TM = 1024
MERGE_TN = 512
LRU_TC = 2048
LRU_GROUPS = 4
LRU_UNROLL = 4
ATT_ROWS = 8
ROUTER_TM = 512
MOE_BM = 256
COMBINE_TM = 256


def _cparams(*sem):
    return pltpu.CompilerParams(dimension_semantics=sem, vmem_limit_bytes=VMEM_LIMIT_BYTES)


def _gather_cparams(vmem_limit):
    return pltpu.CompilerParams(dimension_semantics=("arbitrary",), vmem_limit_bytes=vmem_limit)


def _norm_proj_kernel(x_ref, g_ref, w_ref, o_ref, xn_ref):
    @pl.when(pl.program_id(1) == 0)
    def _():
        x = x_ref[...]
        ms = jnp.mean(x * x, axis=-1, keepdims=True)
        xn_ref[...] = ((x * lax.rsqrt(ms + EPS)) * g_ref[...]).astype(MXU_DTYPE)

    o_ref[...] = jnp.dot(xn_ref[...], w_ref[...], preferred_element_type=jnp.float32)


def _norm_proj(x2, g, w_bf16):
    n, d = x2.shape
    d_out = w_bf16.shape[1]
    tm = min(PROJ_TM, n)
    tn = min(PROJ_TN, d_out)
    return pl.pallas_call(
        _norm_proj_kernel,
        grid=(n // tm, d_out // tn),
        in_specs=[
            pl.BlockSpec((tm, d), lambda i, j: (i, 0)),
            pl.BlockSpec((1, d), lambda i, j: (0, 0)),
            pl.BlockSpec((d, tn), lambda i, j: (0, j)),
        ],
        out_specs=pl.BlockSpec((tm, tn), lambda i, j: (i, j)),
        out_shape=jax.ShapeDtypeStruct((n, d_out), jnp.float32),
        scratch_shapes=[pltpu.VMEM((tm, d), MXU_DTYPE)],
        compiler_params=_cparams("parallel", "arbitrary"),
    )(x2, g.reshape(1, d), w_bf16)


def _softplus(y):
    return jnp.maximum(y, 0.0) + jnp.log1p(jnp.exp(-jnp.abs(y)))


def _lru_pitch(tc):
    seg_len = tc // (LRU_GROUPS * SUBLANES)
    assert seg_len * LRU_GROUPS * SUBLANES == tc and seg_len % SUBLANES == 0
    return seg_len + SUBLANES // 2


def _lru_kernel(*refs, reverse, n_chunks, tc):
    if reverse:
        (x_ref, xp_ref, xn_ref, cw_ref, cb_ref, wg_ref, ba_ref, bx_ref, lam_ref,
         hf_ref, gate_ref, o_ref, xpad_ref, a_ref, u_ref, h_ref, carry_ref) = refs
    else:
        (x_ref, xp_ref, xn_ref, cw_ref, cb_ref, wg_ref, ba_ref, bx_ref, lam_ref,
         o_ref, xpad_ref, a_ref, u_ref, h_ref, carry_ref) = refs
    t = pl.program_id(2)
    chunk = (n_chunks - 1 - t) if reverse else t

    @pl.when(t == 0)
    def _():
        carry_ref[...] = jnp.zeros_like(carry_ref)

    prev = jnp.where(chunk == 0, 0.0, xp_ref[...])
    nxt = jnp.where(chunk == n_chunks - 1, 0.0, xn_ref[...])
    xpad_ref[pl.ds(0, SUBLANES), :] = prev
    xpad_ref[pl.ds(SUBLANES, tc), :] = x_ref[...]
    xpad_ref[pl.ds(SUBLANES + tc, SUBLANES), :] = nxt
    left = CONV_W // 2
    xc = cb_ref[...]
    for tap in range(CONV_W):
        xc = xc + xpad_ref[pl.ds(SUBLANES - left + tap, tc), :] * cw_ref[pl.ds(tap, 1), :]

    z = jnp.dot(xc.astype(MXU_DTYPE), wg_ref[...], preferred_element_type=jnp.float32)
    r = jax.nn.sigmoid(z[:, :LRU_BLOCK] + ba_ref[...])
    i = jax.nn.sigmoid(z[:, LRU_BLOCK:] + bx_ref[...])
    log_a = (-LRU_C * r) * _softplus(-lam_ref[...])
    a = jnp.exp(log_a)
    u = jnp.sqrt(jnp.tanh(-log_a) * (1.0 + a * a)) * (i * xc)

    pitch = _lru_pitch(tc)
    slots = LRU_GROUPS * SUBLANES * pitch
    a_ref[pl.ds(0, tc), :] = a
    u_ref[pl.ds(0, tc), :] = u
    a_ref[pl.ds(tc, slots - tc), :] = jnp.ones((slots - tc, LRU_BLOCK), jnp.float32)
    u_ref[pl.ds(tc, slots - tc), :] = jnp.zeros((slots - tc, LRU_BLOCK), jnp.float32)

    def seg_rows(g, k):
        return pl.ds(g * SUBLANES * pitch + k, SUBLANES, stride=pitch)

    def slot_of(kk):
        return (pitch - 1 - kk) if reverse else kk

    def scan_step(kk, carry):
        hs, ps = carry
        k = slot_of(kk)
        new_h, new_p = [], []
        for g in range(LRU_GROUPS):
            a_k = a_ref[seg_rows(g, k), :]
            h = a_k * hs[g] + u_ref[seg_rows(g, k), :]
            h_ref[seg_rows(g, k), :] = h
            new_h.append(h)
            new_p.append(a_k * ps[g])
        return tuple(new_h), tuple(new_p)

    zeros = tuple(jnp.zeros((SUBLANES, LANES), jnp.float32) for _ in range(LRU_GROUPS))
    ones = tuple(jnp.ones((SUBLANES, LANES), jnp.float32) for _ in range(LRU_GROUPS))
    h_end, p_end = lax.fori_loop(0, pitch, scan_step, (zeros, ones), unroll=LRU_UNROLL)

    row = lax.broadcasted_iota(jnp.int32, (SUBLANES, LANES), 0)
    c_in = [None] * LRU_GROUPS
    cg = carry_ref[...]
    for g in (reversed(range(LRU_GROUPS)) if reverse else range(LRU_GROUPS)):
        pa, hb = p_end[g], h_end[g]
        for s in (1, 2, 4):
            keep = (row < SUBLANES - s) if reverse else (row >= s)
            shift = (SUBLANES - s) if reverse else s
            a_sh = jnp.where(keep, pltpu.roll(pa, shift, 0), 1.0)
            b_sh = jnp.where(keep, pltpu.roll(hb, shift, 0), 0.0)
            hb = pa * b_sh + hb
            pa = pa * a_sh
        seg_out = pa * cg + hb
        if reverse:
            c_in[g] = jnp.where(row == SUBLANES - 1, cg, pltpu.roll(seg_out, SUBLANES - 1, 0))
            cg = seg_out[0:1, :]
        else:
            c_in[g] = jnp.where(row == 0, cg, pltpu.roll(seg_out, 1, 0))
            cg = seg_out[SUBLANES - 1:SUBLANES, :]
    carry_ref[...] = cg

    def fix_step(kk, ps):
        k = slot_of(kk)
        new_p = []
        for g in range(LRU_GROUPS):
            p = a_ref[seg_rows(g, k), :] * ps[g]
            u_ref[seg_rows(g, k), :] = h_ref[seg_rows(g, k), :] + p * c_in[g]
            new_p.append(p)
        return tuple(new_p)

    lax.fori_loop(0, pitch, fix_step, ones, unroll=LRU_UNROLL)

    h_all = u_ref[pl.ds(0, tc), :]
    if reverse:
        o_ref[...] = (jax.nn.gelu(gate_ref[...]) * (hf_ref[...] + h_all)).astype(o_ref.dtype)
    else:
        o_ref[...] = h_all


def _lru_direction(proj3, conv_w, conv_b, wg, ba, bx, lam, hf, *, reverse, d_rnn):
    b, s, _ = proj3.shape
    n_cb = d_rnn // LRU_BLOCK
    tc = min(LRU_TC, s)
    n_chunks = s // tc
    tc8 = tc // SUBLANES
    s8 = s // SUBLANES
    slots = LRU_GROUPS * SUBLANES * _lru_pitch(tc)

    def chunk_of(t):
        return (n_chunks - 1 - t) if reverse else t

    vec_spec = pl.BlockSpec((1, LRU_BLOCK), lambda bi, c, t: (0, c))
    in_specs = [
        pl.BlockSpec((None, tc, LRU_BLOCK), lambda bi, c, t: (bi, chunk_of(t), c)),
        pl.BlockSpec((None, SUBLANES, LRU_BLOCK),
                     lambda bi, c, t: (bi, jnp.maximum(chunk_of(t) * tc8 - 1, 0), c)),
        pl.BlockSpec((None, SUBLANES, LRU_BLOCK),
                     lambda bi, c, t: (bi, jnp.minimum((chunk_of(t) + 1) * tc8, s8 - 1), c)),
        pl.BlockSpec((CONV_W, LRU_BLOCK), lambda bi, c, t: (0, c)),
        vec_spec,
        pl.BlockSpec((None, LRU_BLOCK, 2 * LRU_BLOCK), lambda bi, c, t: (c, 0, 0)),
        vec_spec, vec_spec, vec_spec,
    ]
    args = [proj3, proj3, proj3, conv_w, conv_b.reshape(1, d_rnn), wg,
            ba.reshape(1, d_rnn), bx.reshape(1, d_rnn), lam.reshape(1, d_rnn)]
    if reverse:
        in_specs += [
            pl.BlockSpec((None, tc, LRU_BLOCK), lambda bi, c, t: (bi, chunk_of(t), c)),
            pl.BlockSpec((None, tc, LRU_BLOCK), lambda bi, c, t: (bi, chunk_of(t), n_cb + c)),
        ]
        args += [hf, proj3]
        out_dtype = MXU_DTYPE
    else:
        out_dtype = jnp.float32
    return pl.pallas_call(
        functools.partial(_lru_kernel, reverse=reverse, n_chunks=n_chunks, tc=tc),
        grid=(b, n_cb, n_chunks),
        in_specs=in_specs,
        out_specs=pl.BlockSpec((None, tc, LRU_BLOCK), lambda bi, c, t: (bi, chunk_of(t), c)),
        out_shape=jax.ShapeDtypeStruct((b, s, d_rnn), out_dtype),
        scratch_shapes=[
            pltpu.VMEM((tc + 2 * SUBLANES, LRU_BLOCK), jnp.float32),
            pltpu.VMEM((slots, LRU_BLOCK), jnp.float32),
            pltpu.VMEM((slots, LRU_BLOCK), jnp.float32),
            pltpu.VMEM((slots, LRU_BLOCK), jnp.float32),
            pltpu.VMEM((1, LRU_BLOCK), jnp.float32),
        ],
        compiler_params=_cparams("parallel", "parallel", "arbitrary"),
    )(*args)


def _attention_bias_table(rpb):
    n_heads, n_dr, n_dc = rpb.shape
    cq = np.arange(GRID_W)
    cs = np.clip(cq - WIN_C // 2, 0, GRID_W - WIN_C)
    ck = np.arange(GRID_W)
    valid = (ck[None, :] >= cs[:, None]) & (ck[None, :] < cs[:, None] + WIN_C)
    pad = GRID_W - WIN_C
    width = n_dc + 2 * pad + 1
    padded = jnp.pad(rpb.astype(jnp.float32), ((0, 0), (0, 0), (pad, pad + 1)))
    tiled = jnp.tile(padded, (1, 1, GRID_W))[:, :, :GRID_W * (width - 1)]
    skew = tiled.reshape(n_heads, n_dr, GRID_W, width - 1)
    toep = skew[:, :, :, GRID_W - 1:2 * GRID_W - 1]
    toep = jnp.where(valid[None, None], toep, MASK_VALUE)
    per_off = [toep[:, WIN_R - 1 - off:2 * WIN_R - 1 - off] for off in range(WIN_R)]
    tab = jnp.stack(per_off, axis=1)
    tab = tab.transpose(0, 1, 3, 2, 4)
    return tab.reshape(n_heads, WIN_R, GRID_W, WIN_R * GRID_W)


def _attention_kernel(q_ref, k_ref, v_ref, tb_ref, o_ref, kb_ref, vb_ref, *, rows):
    kb_ref[...] = k_ref[...].astype(MXU_DTYPE)
    vb_ref[...] = v_ref[...].astype(MXU_DTYPE)
    scale = HEAD_DIM ** -0.5
    band = WIN_R * GRID_W

    def tile_body(i, _):
        starts, scores = [], []
        for rl in range(ATT_ROWS):
            r = i * ATT_ROWS + rl
            rs = jnp.clip(r - WIN_R // 2, 0, rows - WIN_R)
            start = pl.multiple_of(rs * GRID_W, GRID_W)
            q0 = pl.multiple_of(r * GRID_W, GRID_W)
            q = (q_ref[pl.ds(q0, GRID_W), :] * scale).astype(MXU_DTYPE)
            kband = kb_ref[pl.ds(start, band), :]
            s = lax.dot_general(q, kband, (((1,), (1,)), ((), ())), preferred_element_type=jnp.float32)
            starts.append(start)
            scores.append(s + tb_ref[r - rs])
        probs, denoms = [], []
        for s in scores:
            m = jnp.max(s, axis=-1, keepdims=True)
            p = jnp.exp(s - m)
            denoms.append(jnp.sum(p, axis=-1, keepdims=True))
            probs.append(p.astype(MXU_DTYPE))
        for rl in range(ATT_ROWS):
            vband = vb_ref[pl.ds(starts[rl], band), :]
            o = jnp.dot(probs[rl], vband, preferred_element_type=jnp.float32)
            q0 = pl.multiple_of((i * ATT_ROWS + rl) * GRID_W, GRID_W)
            o_ref[pl.ds(q0, GRID_W), :] = (o / denoms[rl]).astype(o_ref.dtype)
        return 0

    lax.fori_loop(0, rows // ATT_ROWS, tile_body, 0)


def _attention(proj3, table, *, d_rnn, d_attn):
    b, s, _ = proj3.shape
    rows = s // GRID_W
    n_heads = d_attn // HEAD_DIM
    q_blk = 2 * d_rnn // HEAD_DIM
    k_blk = q_blk + n_heads
    v_blk = k_blk + n_heads
    return pl.pallas_call(
        functools.partial(_attention_kernel, rows=rows),
        grid=(b, n_heads),
        in_specs=[
            pl.BlockSpec((None, s, HEAD_DIM), lambda bi, h: (bi, 0, q_blk + h)),
            pl.BlockSpec((None, s, HEAD_DIM), lambda bi, h: (bi, 0, k_blk + h)),
            pl.BlockSpec((None, s, HEAD_DIM), lambda bi, h: (bi, 0, v_blk + h)),
            pl.BlockSpec((None, WIN_R, GRID_W, WIN_R * GRID_W), lambda bi, h: (h, 0, 0, 0)),
        ],
        out_specs=pl.BlockSpec((None, s, HEAD_DIM), lambda bi, h: (bi, 0, h)),
        out_shape=jax.ShapeDtypeStruct((b, s, d_attn), MXU_DTYPE),
        scratch_shapes=[
            pltpu.VMEM((s, HEAD_DIM), MXU_DTYPE),
            pltpu.VMEM((s, HEAD_DIM), MXU_DTYPE),
        ],
        compiler_params=_cparams("parallel", "parallel"),
    )(proj3, proj3, proj3, table)


def _merge_kernel(ya_ref, att_ref, wl_ref, wa_ref, gl_ref, ga_ref, bm_ref, o_ref):
    y_lru = jnp.dot(ya_ref[...], wl_ref[...], preferred_element_type=jnp.float32)
    y_att = jnp.dot(att_ref[...], wa_ref[...], preferred_element_type=jnp.float32)
    s_lru = jax.nn.sigmoid(gl_ref[...] + bm_ref[pl.ds(0, 1), :])
    s_att = jax.nn.sigmoid(ga_ref[...] + bm_ref[pl.ds(1, 1), :])
    o_ref[...] = (s_lru * y_lru + s_att * y_att).astype(o_ref.dtype)


def _merge(ya2, att2, wl, wa, proj2, b_merge, *, gate_col):
    n, d = ya2.shape
    tm = min(MERGE_TM, n)
    tn = min(MERGE_TN, d)
    gl_blk = gate_col // tn
    ga_blk = (gate_col + d) // tn
    return pl.pallas_call(
        _merge_kernel,
        grid=(n // tm, d // tn),
        in_specs=[
            pl.BlockSpec((tm, d), lambda i, j: (i, 0)),
            pl.BlockSpec((tm, d), lambda i, j: (i, 0)),
            pl.BlockSpec((d, tn), lambda i, j: (0, j)),
            pl.BlockSpec((d, tn), lambda i, j: (0, j)),
            pl.BlockSpec((tm, tn), lambda i, j: (i, gl_blk + j)),
            pl.BlockSpec((tm, tn), lambda i, j: (i, ga_blk + j)),
            pl.BlockSpec((2, tn), lambda i, j: (0, j)),
        ],
        out_specs=pl.BlockSpec((tm, tn), lambda i, j: (i, j)),
        out_shape=jax.ShapeDtypeStruct((n, d), MXU_DTYPE),
        compiler_params=_cparams("parallel", "arbitrary"),
    )(ya2, att2, wl, wa, proj2, proj2, b_merge)


def _out_proj_kernel(m_ref, w_ref, x_ref, o_ref):
    o_ref[...] = x_ref[...] + jnp.dot(m_ref[...], w_ref[...], preferred_element_type=jnp.float32)


def _out_proj(m2, w_o, x2):
    n, d = m2.shape
    tm = min(MERGE_TM, n)
    tn = min(MERGE_TN, d)
    return pl.pallas_call(
        _out_proj_kernel,
        grid=(n // tm, d // tn),
        in_specs=[
            pl.BlockSpec((tm, d), lambda i, j: (i, 0)),
            pl.BlockSpec((d, tn), lambda i, j: (0, j)),
            pl.BlockSpec((tm, tn), lambda i, j: (i, j)),
        ],
        out_specs=pl.BlockSpec((tm, tn), lambda i, j: (i, j)),
        out_shape=jax.ShapeDtypeStruct((n, d), jnp.float32),
        compiler_params=_cparams("parallel", "arbitrary"),
    )(m2, w_o, x2)


GROUP_LANE0 = N_EXPERTS
META_E, META_W, META_RANK = 0, 2, 4
GROUP_SHIFT = EXPERTS_PER_GROUP.bit_length() - 1
TOP_K_SHIFT = TOP_K.bit_length() - 1
assert 1 << GROUP_SHIFT == EXPERTS_PER_GROUP and 1 << TOP_K_SHIFT == TOP_K


def _router_kernel(ha_ref, hb_ref, g_ref, w_ref, b_ref, xn_ref, meta_ref, cnt_ref, *, steps_a):
    step = pl.program_id(0)

    @pl.when(step == 0)
    def _():
        cnt_ref[...] = jnp.zeros_like(cnt_ref)

    x = jnp.where(step < steps_a, ha_ref[...], hb_ref[...])
    tm = x.shape[0]
    ms = jnp.mean(x * x, axis=-1, keepdims=True)
    xn = (x * lax.rsqrt(ms + EPS)) * g_ref[...]
    xn_ref[...] = xn
    logits = jnp.dot(xn, w_ref[...], preferred_element_type=jnp.float32,
                     precision=lax.Precision.HIGHEST) + b_ref[...]
    lane = lax.broadcasted_iota(jnp.int32, (tm, LANES), 1)
    neg_inf = -jnp.inf

    def first_argmax(vals, vmax):
        return jnp.min(jnp.where(vals == vmax, lane, LANES), axis=-1, keepdims=True)

    is_group = (lane >= GROUP_LANE0) & (lane < GROUP_LANE0 + N_GROUPS)
    lg = jnp.where(is_group, logits, neg_inf)
    mg = jnp.max(lg, axis=-1, keepdims=True)
    g_idx = first_argmax(lg, mg) - GROUP_LANE0
    g_val = 1.0 / jnp.sum(jnp.exp(lg - mg), axis=-1, keepdims=True)

    in_group = (lane < N_EXPERTS) & (lax.shift_right_logical(lane, GROUP_SHIFT) == g_idx)
    le = jnp.where(in_group, logits, neg_inf)
    m1 = jnp.max(le, axis=-1, keepdims=True)
    e1 = first_argmax(le, m1)
    le2 = jnp.where(lane == e1, neg_inf, le)
    m2 = jnp.max(le2, axis=-1, keepdims=True)
    e2 = first_argmax(le2, m2)
    z = jnp.sum(jnp.exp(le - m1), axis=-1, keepdims=True)
    p1 = 1.0 / z
    p2 = jnp.exp(m2 - m1) / z
    den = p1 + p2
    w1 = g_val * (p1 / den)
    w2 = g_val * (p2 / den)

    hot1 = lane == e1
    hot2 = lane == e2
    hot = hot1.astype(jnp.float32) + hot2.astype(jnp.float32)
    r_i = lax.broadcasted_iota(jnp.int32, (tm, tm), 0)
    c_i = lax.broadcasted_iota(jnp.int32, (tm, tm), 1)
    lower = (c_i < r_i).astype(MXU_DTYPE)
    before = jnp.dot(lower, hot.astype(MXU_DTYPE), preferred_element_type=jnp.float32) + cnt_ref[...]
    rank1 = jnp.sum(jnp.where(hot1, before, 0.0), axis=-1, keepdims=True)
    rank2 = jnp.sum(jnp.where(hot2, before, 0.0), axis=-1, keepdims=True)
    cnt_ref[...] += jnp.sum(hot, axis=0, keepdims=True)

    meta = jnp.where(lane == META_E, e1.astype(jnp.float32), 0.0)
    meta = jnp.where(lane == META_E + 1, e2.astype(jnp.float32), meta)
    meta = jnp.where(lane == META_W, w1, meta)
    meta = jnp.where(lane == META_W + 1, w2, meta)
    meta = jnp.where(lane == META_RANK, rank1, meta)
    meta = jnp.where(lane == META_RANK + 1, rank2, meta)
    meta_ref[...] = meta


def _router(h_a, h_b, g, w_router, b_router):
    n_a, d = h_a.shape
    n_b = h_b.shape[0]
    n = n_a + n_b
    tm = min(ROUTER_TM, n_a, n_b)
    steps_a = n_a // tm
    return pl.pallas_call(
        functools.partial(_router_kernel, steps_a=steps_a),
        grid=(n // tm,),
        in_specs=[
            pl.BlockSpec((tm, d), lambda i: (jnp.minimum(i, steps_a - 1), 0)),
            pl.BlockSpec((tm, d), lambda i: (jnp.maximum(i - steps_a, 0), 0)),
            pl.BlockSpec((1, d), lambda i: (0, 0)),
            pl.BlockSpec((d, LANES), lambda i: (0, 0)),
            pl.BlockSpec((1, LANES), lambda i: (0, 0)),
        ],
        out_specs=[
            pl.BlockSpec((tm, d), lambda i: (i, 0)),
            pl.BlockSpec((tm, LANES), lambda i: (i, 0)),
            pl.BlockSpec((1, LANES), lambda i: (0, 0)),
        ],
        out_shape=[
            jax.ShapeDtypeStruct((n, d), jnp.float32),
            jax.ShapeDtypeStruct((n, LANES), jnp.float32),
            jax.ShapeDtypeStruct((1, LANES), jnp.float32),
        ],
        compiler_params=_cparams("arbitrary"),
    )(h_a, h_b, g.reshape(1, d), w_router, b_router)


def _experts_kernel(pos_ref, start_ref, count_ref, rows_ref, nused_ref, x_hbm, wg_ref, wu_ref, wd_ref, y_hbm,
                    tok_ref, xbuf, obuf, gsem, osem, *, bm, n_blocks):
    e = pl.program_id(0)
    n_used = nused_ref[0]
    b0 = start_ref[e]

    def start_gather(blk, slot):
        def body(i8, _):
            row0 = pl.multiple_of(i8 * SUBLANES, SUBLANES)
            for k in range(SUBLANES):
                tok = tok_ref[blk * bm + row0 + k]
                pltpu.make_async_copy(x_hbm.at[pl.ds(tok, 1), :], xbuf.at[slot, pl.ds(row0 + k, 1), :],
                                      gsem.at[slot]).start()
            return 0
        lax.fori_loop(0, bm // SUBLANES, body, 0)

    def wait_gather(slot):
        pltpu.make_async_copy(x_hbm.at[pl.ds(0, bm), :], xbuf.at[slot], gsem.at[slot]).wait()

    def out_copy(blk, slot):
        return pltpu.make_async_copy(obuf.at[slot], y_hbm.at[pl.ds(blk * bm, bm), :], osem.at[slot])

    @pl.when(e == 0)
    def _():
        n_assign = pos_ref.shape[0]

        def clear_padding(ex, _):
            def clear(i, _):
                tok_ref[i] = 0
                return 0
            row0 = start_ref[ex] * bm
            lax.fori_loop(row0 + rows_ref[ex], row0 + count_ref[ex] * bm, clear, 0)
            return 0

        def invert(i8, _):
            for k in range(SUBLANES):
                a = i8 * SUBLANES + k
                tok_ref[pos_ref[a]] = lax.shift_right_logical(a, TOP_K_SHIFT)
            return 0

        lax.fori_loop(0, pl.num_programs(0), clear_padding, 0)
        lax.fori_loop(0, n_assign // SUBLANES, invert, 0)
        start_gather(0, 0)

    def block_body(j, _):
        b = b0 + j
        slot = b % 2

        @pl.when(b + 1 < n_used)
        def _():
            start_gather(b + 1, 1 - slot)

        wait_gather(slot)

        @pl.when(b >= 2)
        def _():
            out_copy(b - 2, slot).wait()

        x = xbuf[slot].astype(MXU_DTYPE)
        hg = jnp.dot(x, wg_ref[...].astype(MXU_DTYPE), preferred_element_type=jnp.float32)
        hu = jnp.dot(x, wu_ref[...].astype(MXU_DTYPE), preferred_element_type=jnp.float32)
        hid = (jax.nn.silu(hg) * hu).astype(MXU_DTYPE)
        obuf[slot] = jnp.dot(hid, wd_ref[...].astype(MXU_DTYPE), preferred_element_type=jnp.float32)
        out_copy(b, slot).start()
        return 0

    lax.fori_loop(0, count_ref[e], block_body, 0)

    @pl.when(e == pl.num_programs(0) - 1)
    def _():
        @pl.when(n_used >= 2)
        def _():
            out_copy(n_used - 2, n_used % 2).wait()

        out_copy(n_used - 1, (n_used - 1) % 2).wait()
        obuf[0] = jnp.zeros(obuf.shape[1:], jnp.float32)

        def zero_fill(b, _):
            out_copy(b, 0).start()
            out_copy(b, 0).wait()
            return 0

        lax.fori_loop(n_used, n_blocks, zero_fill, 0)


def _experts(xn2, pos, blk_start, blk_count, exp_rows, n_used, w_gate, w_up, w_down, *, n_blocks):
    n, d = xn2.shape
    n_experts, _, f = w_gate.shape
    bm = MOE_BM
    grid_spec = pltpu.PrefetchScalarGridSpec(
        num_scalar_prefetch=5,
        grid=(n_experts,),
        in_specs=[
            pl.BlockSpec(memory_space=pl.ANY),
            pl.BlockSpec((None, d, f), lambda e, *_: (e, 0, 0)),
            pl.BlockSpec((None, d, f), lambda e, *_: (e, 0, 0)),
            pl.BlockSpec((None, f, d), lambda e, *_: (e, 0, 0)),
        ],
        out_specs=pl.BlockSpec(memory_space=pl.ANY),
        scratch_shapes=[
            pltpu.SMEM((n_blocks * bm,), jnp.int32),
            pltpu.VMEM((2, bm, d), jnp.float32),
            pltpu.VMEM((2, bm, d), jnp.float32),
            pltpu.SemaphoreType.DMA((2,)),
            pltpu.SemaphoreType.DMA((2,)),
        ],
    )
    return pl.pallas_call(
        functools.partial(_experts_kernel, bm=bm, n_blocks=n_blocks),
        grid_spec=grid_spec,
        out_shape=jax.ShapeDtypeStruct((n_blocks * bm, d), jnp.float32),
        compiler_params=_gather_cparams(EXPERTS_VMEM_LIMIT_BYTES),
    )(pos, blk_start, blk_count, exp_rows, n_used, xn2, w_gate, w_up, w_down)


def _combine_kernel(pos_ref, y_hbm, h_ref, meta_ref, g_ref, o_ref, ybuf, sem, *, tm, n_steps):
    step = pl.program_id(0)

    def start_gather(blk, slot):
        def body(i8, _):
            row0 = pl.multiple_of(i8 * SUBLANES, SUBLANES)
            for j in range(SUBLANES):
                for k in range(TOP_K):
                    pos = pos_ref[(blk * tm + row0 + j) * TOP_K + k]
                    pltpu.make_async_copy(y_hbm.at[pl.ds(pos, 1), :], ybuf.at[slot, k, pl.ds(row0 + j, 1), :],
                                          sem.at[slot]).start()
            return 0
        lax.fori_loop(0, tm // SUBLANES, body, 0)

    def wait_gather(slot):
        for k in range(TOP_K):
            pltpu.make_async_copy(y_hbm.at[pl.ds(0, tm), :], ybuf.at[slot, k], sem.at[slot]).wait()

    slot = step % 2

    @pl.when(step == 0)
    def _():
        start_gather(0, 0)

    @pl.when(step + 1 < n_steps)
    def _():
        start_gather(step + 1, 1 - slot)

    wait_gather(slot)
    w1 =meta_ref[:, META_W:META_W + 1]
    w2 = meta_ref[:, META_W + 1:META_W + 2]
    h = h_ref[...] + (ybuf[slot, 0] * w1 + ybuf[slot, 1] * w2)
    ms = jnp.mean(h * h, axis=-1, keepdims=True)
    o_ref[...] = (h * lax.rsqrt(ms + EPS)) * g_ref[...]


def _combine(pos, yb, h2, meta, g):
    n, d = h2.shape
    tm = min(COMBINE_TM, n)
    n_steps = n // tm
    grid_spec = pltpu.PrefetchScalarGridSpec(
        num_scalar_prefetch=1,
        grid=(n_steps,),
        in_specs=[
            pl.BlockSpec(memory_space=pl.ANY),
            pl.BlockSpec((tm, d), lambda i, pos: (i, 0)),
            pl.BlockSpec((tm, LANES), lambda i, pos: (i, 0)),
            pl.BlockSpec((1, d), lambda i, pos: (0, 0)),
        ],
        out_specs=pl.BlockSpec((tm, d), lambda i, pos: (i, 0)),
        scratch_shapes=[
            pltpu.VMEM((2, TOP_K, tm, d), jnp.float32),
            pltpu.SemaphoreType.DMA((2,)),
        ],
    )
    return pl.pallas_call(
        functools.partial(_combine_kernel, tm=tm, n_steps=n_steps),
        grid_spec=grid_spec,
        out_shape=jax.ShapeDtypeStruct((n, d), jnp.float32),
        compiler_params=_gather_cparams(VMEM_LIMIT_BYTES),
    )(pos, yb, h2, meta, g.reshape(1, d))


def _dispatch_plan(meta, counts_f, n):
    bm = MOE_BM
    a = n * TOP_K
    expert = meta[:, META_E:META_E + TOP_K].astype(jnp.int32)
    rank = meta[:, META_RANK:META_RANK + TOP_K].astype(jnp.int32)
    counts = counts_f[0, :N_EXPERTS].astype(jnp.int32)
    padded = (counts + bm - 1) // bm * bm
    pad_end = jnp.cumsum(padded)
    pad_start = pad_end - padded
    pos = (pad_start[expert] + rank).reshape(a)
    n_blocks = -(-a // bm) + N_EXPERTS
    blk_start = (pad_start // bm).astype(jnp.int32)
    blk_count = (padded // bm).astype(jnp.int32)
    n_used = (pad_end[-1] // bm).astype(jnp.int32).reshape(1)
    return pos, blk_start, blk_count, counts, n_used, n_blocks


def _mixer_residual(x, p):
    b, s, d = x.shape
    n = b * s
    d_rnn = p["d_rnn"]
    d_attn = p["d_attn"]
    x2 = x.reshape(n, d)
    proj2 = _norm_proj(x2, p["norm_mix_g"], p["w_in"])
    d_in = proj2.shape[1]
    proj3 = proj2.reshape(b, s, d_in)
    hf = _lru_direction(proj3, p["conv_w"], p["conv_b"], p["lru_wg"][0], p["lru_b_a"][0], p["lru_b_x"][0],
                        p["lru_lambda"][0], None, reverse=False, d_rnn=d_rnn)
    ya = _lru_direction(proj3, p["conv_w"], p["conv_b"], p["lru_wg"][1], p["lru_b_a"][1], p["lru_b_x"][1],
                        p["lru_lambda"][1], hf, reverse=True, d_rnn=d_rnn)
    att = _attention(proj3, p["att_table"], d_rnn=d_rnn, d_attn=d_attn)
    m2 = _merge(ya.reshape(n, d_rnn), att.reshape(n, d_attn), p["w_lru_up"], p["w_attn_up"], proj2,
                p["b_merge"], gate_col=2 * d_rnn + 3 * d_attn)
    return _out_proj(m2, p["w_o"], x2)


def _moe_and_final_norm(h_a, h_b, p):
    n_a, n_b = h_a.shape[0], h_b.shape[0]
    xn2, meta, counts = _router(h_a, h_b, p["norm_ffn_g"], p["w_router"], p["b_router"])
    pos, blk_start, blk_count, exp_rows, n_used, n_blocks = _dispatch_plan(meta, counts, n_a + n_b)
    yb = _experts(xn2, pos, blk_start, blk_count, exp_rows, n_used, p["w_exp_gate"], p["w_exp_up"], p["w_exp_down"],
                  n_blocks=n_blocks)
    y_a = _combine(pos[:n_a * TOP_K], yb, h_a, meta[:n_a], p["norm_final_g"])
    y_b = _combine(pos[n_a * TOP_K:], yb, h_b, meta[n_a:], p["norm_final_g"])
    return y_a, y_b


def _prepare_params(norm_mix_g, w_in, conv_w, conv_b, lru_w_a, lru_b_a, lru_w_x, lru_b_x, lru_lambda, rpb,
                    w_lru_up, w_attn_up, b_merge, w_o, norm_ffn_g, w_router_group, b_router_group,
                    w_router_expert, b_router_expert, w_exp_gate, w_exp_up, w_exp_down, norm_final_g):
    bf16 = MXU_DTYPE
    d = w_in.shape[1]
    d_rnn = conv_w.shape[2]
    d_attn = w_attn_up.shape[1]
    lru_wg = jnp.concatenate([lru_w_a[0], lru_w_x[0]], axis=-1).astype(bf16)
    w_router = jnp.zeros((d, LANES), jnp.float32)
    w_router = w_router.at[:, :N_EXPERTS].set(w_router_expert[0])
    w_router = w_router.at[:, GROUP_LANE0:GROUP_LANE0 + N_GROUPS].set(w_router_group[0])
    b_router = jnp.zeros((1, LANES), jnp.float32)
    b_router = b_router.at[0, :N_EXPERTS].set(b_router_expert[0])
    b_router = b_router.at[0, GROUP_LANE0:GROUP_LANE0 + N_GROUPS].set(b_router_group[0])
    return dict(
        d_rnn=d_rnn, d_attn=d_attn,
        norm_mix_g=norm_mix_g[0], w_in=w_in[0].astype(bf16),
        conv_w=conv_w[0], conv_b=conv_b[0], lru_wg=lru_wg,
        lru_b_a=lru_b_a[0], lru_b_x=lru_b_x[0], lru_lambda=lru_lambda[0],
        att_table=_attention_bias_table(rpb[0]),
        w_lru_up=w_lru_up[0].astype(bf16), w_attn_up=w_attn_up[0].astype(bf16),
        b_merge=b_merge[0], w_o=w_o[0].astype(bf16),
        norm_ffn_g=norm_ffn_g[0], w_router=w_router, b_router=b_router,
        w_exp_gate=w_exp_gate[0], w_exp_up=w_exp_up[0], w_exp_down=w_exp_down[0],
        norm_final_g=norm_final_g,
    )


def kernel(x_prompt, x_sample, norm_mix_g, w_in, conv_w, conv_b, lru_w_a, lru_b_a, lru_w_x, lru_b_x, lru_lambda, rpb, w_lru_up, w_attn_up, b_merge, w_o, norm_ffn_g, w_router_group, b_router_group, w_router_expert, b_router_expert, w_exp_gate, w_exp_up, w_exp_down, norm_final_g):
    assert norm_mix_g.shape[0] == 1, "single-layer encoder"
    p = _prepare_params(norm_mix_g, w_in, conv_w, conv_b, lru_w_a, lru_b_a, lru_w_x, lru_b_x, lru_lambda, rpb,
                        w_lru_up, w_attn_up, b_merge, w_o, norm_ffn_g, w_router_group, b_router_group,
                        w_router_expert, b_router_expert, w_exp_gate, w_exp_up, w_exp_down, norm_final_g)
    h_prompt = _mixer_residual(x_prompt, p)
    h_sample = _mixer_residual(x_sample, p)
    y_prompt, y_sample = _moe_and_final_norm(h_prompt, h_sample, p)
    return (y_prompt.reshape(x_prompt.shape), y_sample.reshape(x_sample.shape))
```

```python
import functools

import numpy as np
import jax
import jax.numpy as jnp
from jax import lax
from jax.experimental import pallas as pl
from jax.experimental.pallas import tpu as pltpu

GRID_W = 64
WIN_R = 8
WIN_C = 16
CONV_W = 4
LRU_C = 8.0
LRU_BLOCK = 128
HEAD_DIM = 128
N_GROUPS = 4
EXPERTS_PER_GROUP = 8
N_EXPERTS = N_GROUPS * EXPERTS_PER_GROUP
TOP_K = 2
EPS = 1e-6
MASK_VALUE = -1e30

MXU_DTYPE = jnp.bfloat16

LANES = 128
SUBLANES = 8
VMEM_LIMIT_BYTES = 56 * 1024 * 1024
EXPERTS_VMEM_LIMIT_BYTES = 62 * 1024 * 1024

PROJ_TM = 1024
PROJ_TN = 1024
MERGE_TM = 1024
MERGE_TN = 512
LRU_TC = 2048
LRU_GROUPS = 4
LRU_UNROLL = 4
ATT_ROWS = 32
ROUTER_TM = 512
MOE_BM = 256
COMBINE_TM = 256


def _cparams(*sem):
    return pltpu.CompilerParams(dimension_semantics=sem, vmem_limit_bytes=VMEM_LIMIT_BYTES)


def _gather_cparams(vmem_limit):
    return pltpu.CompilerParams(dimension_semantics=("arbitrary",), vmem_limit_bytes=vmem_limit)


def _norm_proj_kernel(x_ref, g_ref, w_ref, o_ref, xn_ref):
    @pl.when(pl.program_id(1) == 0)
    def _():
        x = x_ref[...]
        ms = jnp.mean(x * x, axis=-1, keepdims=True)
        xn_ref[...] = ((x * lax.rsqrt(ms + EPS)) * g_ref[...]).astype(MXU_DTYPE)

    o_ref[...] = jnp.dot(xn_ref[...], w_ref[...], preferred_element_type=jnp.float32)


def _norm_proj(x2, g, w_bf16):
    n, d = x2.shape
    d_out = w_bf16.shape[1]
    tm = min(PROJ_TM, n)
    tn = min(PROJ_TN, d_out)
    return pl.pallas_call(
        _norm_proj_kernel,
        grid=(n // tm, d_out // tn),
        in_specs=[
            pl.BlockSpec((tm, d), lambda i, j: (i, 0)),
            pl.BlockSpec((1, d), lambda i, j: (0, 0)),
            pl.BlockSpec((d, tn), lambda i, j: (0, j)),
        ],
        out_specs=pl.BlockSpec((tm, tn), lambda i, j: (i, j)),
        out_shape=jax.ShapeDtypeStruct((n, d_out), jnp.float32),
        scratch_shapes=[pltpu.VMEM((tm, d), MXU_DTYPE)],
        compiler_params=_cparams("parallel", "arbitrary"),
    )(x2, g.reshape(1, d), w_bf16)


def _softplus(y):
    return jnp.maximum(y, 0.0) + jnp.log1p(jnp.exp(-jnp.abs(y)))


def _lru_pitch(tc):
    seg_len = tc // (LRU_GROUPS * SUBLANES)
    assert seg_len * LRU_GROUPS * SUBLANES == tc and seg_len % SUBLANES == 0
    return seg_len + SUBLANES // 2


def _lru_kernel(*refs, reverse, n_chunks, tc):
    if reverse:
        (x_ref, xp_ref, xn_ref, cw_ref, cb_ref, wg_ref, ba_ref, bx_ref, lam_ref,
         hf_ref, gate_ref, o_ref, xpad_ref, a_ref, u_ref, h_ref, carry_ref) = refs
    else:
        (x_ref, xp_ref, xn_ref, cw_ref, cb_ref, wg_ref, ba_ref, bx_ref, lam_ref,
         o_ref, xpad_ref, a_ref, u_ref, h_ref, carry_ref) = refs
    t = pl.program_id(2)
    chunk = (n_chunks - 1 - t) if reverse else t

    @pl.when(t == 0)
    def _():
        carry_ref[...] = jnp.zeros_like(carry_ref)

    prev = jnp.where(chunk == 0, 0.0, xp_ref[...])
    nxt = jnp.where(chunk == n_chunks - 1, 0.0, xn_ref[...])
    xpad_ref[pl.ds(0, SUBLANES), :] = prev
    xpad_ref[pl.ds(SUBLANES, tc), :] = x_ref[...]
    xpad_ref[pl.ds(SUBLANES + tc, SUBLANES), :] = nxt
    left = CONV_W // 2
    xc = cb_ref[...]
    for tap in range(CONV_W):
        xc = xc + xpad_ref[pl.ds(SUBLANES - left + tap, tc), :] * cw_ref[pl.ds(tap, 1), :]

    z = jnp.dot(xc.astype(MXU_DTYPE), wg_ref[...], preferred_element_type=jnp.float32)
    r = jax.nn.sigmoid(z[:, :LRU_BLOCK] + ba_ref[...])
    i = jax.nn.sigmoid(z[:, LRU_BLOCK:] + bx_ref[...])
    log_a = (-LRU_C * r) * _softplus(-lam_ref[...])
    a = jnp.exp(log_a)
    u = jnp.sqrt(jnp.tanh(-log_a) * (1.0 + a * a)) * (i * xc)

    pitch = _lru_pitch(tc)
    slots = LRU_GROUPS * SUBLANES * pitch
    a_ref[pl.ds(0, tc), :] = a
    u_ref[pl.ds(0, tc), :] = u
    a_ref[pl.ds(tc, slots - tc), :] = jnp.ones((slots - tc, LRU_BLOCK), jnp.float32)
    u_ref[pl.ds(tc, slots - tc), :] = jnp.zeros((slots - tc, LRU_BLOCK), jnp.float32)

    def seg_rows(g, k):
        return pl.ds(g * SUBLANES * pitch + k, SUBLANES, stride=pitch)

    def slot_of(kk):
        return (pitch - 1 - kk) if reverse else kk

    def scan_step(kk, carry):
        hs, ps = carry
        k = slot_of(kk)
        new_h, new_p = [], []
        for g in range(LRU_GROUPS):
            a_k = a_ref[seg_rows(g, k), :]
            h = a_k * hs[g] + u_ref[seg_rows(g, k), :]
            h_ref[seg_rows(g, k), :] = h
            new_h.append(h)
            new_p.append(a_k * ps[g])
        return tuple(new_h), tuple(new_p)

    zeros = tuple(jnp.zeros((SUBLANES, LANES), jnp.float32) for _ in range(LRU_GROUPS))
    ones = tuple(jnp.ones((SUBLANES, LANES), jnp.float32) for _ in range(LRU_GROUPS))
    h_end, p_end = lax.fori_loop(0, pitch, scan_step, (zeros, ones), unroll=LRU_UNROLL)

    row = lax.broadcasted_iota(jnp.int32, (SUBLANES, LANES), 0)
    c_in = [None] * LRU_GROUPS
    cg = carry_ref[...]
    for g in (reversed(range(LRU_GROUPS)) if reverse else range(LRU_GROUPS)):
        pa, hb = p_end[g], h_end[g]
        for s in (1, 2, 4):
            keep = (row < SUBLANES - s) if reverse else (row >= s)
            shift = (SUBLANES - s) if reverse else s
            a_sh = jnp.where(keep, pltpu.roll(pa, shift, 0), 1.0)
            b_sh = jnp.where(keep, pltpu.roll(hb, shift, 0), 0.0)
            hb = pa * b_sh + hb
            pa = pa * a_sh
        seg_out = pa * cg + hb
        if reverse:
            c_in[g] = jnp.where(row == SUBLANES - 1, cg, pltpu.roll(seg_out, SUBLANES - 1, 0))
            cg = seg_out[0:1, :]
        else:
            c_in[g] = jnp.where(row == 0, cg, pltpu.roll(seg_out, 1, 0))
            cg = seg_out[SUBLANES - 1:SUBLANES, :]
    carry_ref[...] = cg

    def fix_step(kk, ps):
        k = slot_of(kk)
        new_p = []
        for g in range(LRU_GROUPS):
            p = a_ref[seg_rows(g, k), :] * ps[g]
            u_ref[seg_rows(g, k), :] = h_ref[seg_rows(g, k), :] + p * c_in[g]
            new_p.append(p)
        return tuple(new_p)

    lax.fori_loop(0, pitch, fix_step, ones, unroll=LRU_UNROLL)

    h_all = u_ref[pl.ds(0, tc), :]
    if reverse:
        o_ref[...] = (jax.nn.gelu(gate_ref[...]) * (hf_ref[...] + h_all)).astype(o_ref.dtype)
    else:
        o_ref[...] = h_all


def _lru_direction(proj3, conv_w, conv_b, wg, ba, bx, lam, hf, *, reverse, d_rnn):
    b, s, _ = proj3.shape
    n_cb = d_rnn // LRU_BLOCK
    tc = min(LRU_TC, s)
    n_chunks = s // tc
    tc8 = tc // SUBLANES
    s8 = s // SUBLANES
    slots = LRU_GROUPS * SUBLANES * _lru_pitch(tc)

    def chunk_of(t):
        return (n_chunks - 1 - t) if reverse else t

    vec_spec = pl.BlockSpec((1, LRU_BLOCK), lambda bi, c, t: (0, c))
    in_specs = [
        pl.BlockSpec((None, tc, LRU_BLOCK), lambda bi, c, t: (bi, chunk_of(t), c)),
        pl.BlockSpec((None, SUBLANES, LRU_BLOCK),
                     lambda bi, c, t: (bi, jnp.maximum(chunk_of(t) * tc8 - 1, 0), c)),
        pl.BlockSpec((None, SUBLANES, LRU_BLOCK),
                     lambda bi, c, t: (bi, jnp.minimum((chunk_of(t) + 1) * tc8, s8 - 1), c)),
        pl.BlockSpec((CONV_W, LRU_BLOCK), lambda bi, c, t: (0, c)),
        vec_spec,
        pl.BlockSpec((None, LRU_BLOCK, 2 * LRU_BLOCK), lambda bi, c, t: (c, 0, 0)),
        vec_spec, vec_spec, vec_spec,
    ]
    args = [proj3, proj3, proj3, conv_w, conv_b.reshape(1, d_rnn), wg,
            ba.reshape(1, d_rnn), bx.reshape(1, d_rnn), lam.reshape(1, d_rnn)]
    if reverse:
        in_specs += [
            pl.BlockSpec((None, tc, LRU_BLOCK), lambda bi, c, t: (bi, chunk_of(t), c)),
            pl.BlockSpec((None, tc, LRU_BLOCK), lambda bi, c, t: (bi, chunk_of(t), n_cb + c)),
        ]
        args += [hf, proj3]
        out_dtype = MXU_DTYPE
    else:
        out_dtype = jnp.float32
    return pl.pallas_call(
        functools.partial(_lru_kernel, reverse=reverse, n_chunks=n_chunks, tc=tc),
        grid=(b, n_cb, n_chunks),
        in_specs=in_specs,
        out_specs=pl.BlockSpec((None, tc, LRU_BLOCK), lambda bi, c, t: (bi, chunk_of(t), c)),
        out_shape=jax.ShapeDtypeStruct((b, s, d_rnn), out_dtype),
        scratch_shapes=[
            pltpu.VMEM((tc + 2 * SUBLANES, LRU_BLOCK), jnp.float32),
            pltpu.VMEM((slots, LRU_BLOCK), jnp.float32),
            pltpu.VMEM((slots, LRU_BLOCK), jnp.float32),
            pltpu.VMEM((slots, LRU_BLOCK), jnp.float32),
            pltpu.VMEM((1, LRU_BLOCK), jnp.float32),
        ],
        compiler_params=_cparams("parallel", "parallel", "arbitrary"),
    )(*args)


def _attention_bias_table(rpb):
    n_heads, n_dr, n_dc = rpb.shape
    cq = np.arange(GRID_W)
    cs = np.clip(cq - WIN_C // 2, 0, GRID_W - WIN_C)
    ck = np.arange(GRID_W)
    valid = (ck[None, :] >= cs[:, None]) & (ck[None, :] < cs[:, None] + WIN_C)
    pad = GRID_W - WIN_C
    width = n_dc + 2 * pad + 1
    padded = jnp.pad(rpb.astype(jnp.float32), ((0, 0), (0, 0), (pad, pad + 1)))
    tiled = jnp.tile(padded, (1, 1, GRID_W))[:, :, :GRID_W * (width - 1)]
    skew = tiled.reshape(n_heads, n_dr, GRID_W, width - 1)
    toep = skew[:, :, :, GRID_W - 1:2 * GRID_W - 1]
    toep = jnp.where(valid[None, None], toep, MASK_VALUE)
    per_off = [toep[:, WIN_R - 1 - off:2 * WIN_R - 1 - off] for off in range(WIN_R)]
    tab = jnp.stack(per_off, axis=1)
    tab = tab.transpose(0, 1, 3, 2, 4)
    return tab.reshape(n_heads, WIN_R, GRID_W, WIN_R * GRID_W)


def _attention_kernel(q_ref, k_ref, v_ref, tb_ref, o_ref, kb_ref, vb_ref, *, rows, tile_rows):
    kb_ref[...] = k_ref[...].astype(MXU_DTYPE)
    vb_ref[...] = v_ref[...].astype(MXU_DTYPE)
    scale = HEAD_DIM ** -0.5
    band = WIN_R * GRID_W

    def tile_body(i, _):
        starts, scores = [], []
        for rl in range(tile_rows):
            r = i * tile_rows + rl
            rs = jnp.clip(r - WIN_R // 2, 0, rows - WIN_R)
            start = pl.multiple_of(rs * GRID_W, GRID_W)
            q0 = pl.multiple_of(r * GRID_W, GRID_W)
            q = (q_ref[pl.ds(q0, GRID_W), :] * scale).astype(MXU_DTYPE)
            kband = kb_ref[pl.ds(start, band), :]
            s = lax.dot_general(q, kband, (((1,), (1,)), ((), ())), preferred_element_type=jnp.float32)
            starts.append(start)
            scores.append(s + tb_ref[r - rs])
        probs, denoms = [], []
        for s in scores:
            m = jnp.max(s, axis=-1, keepdims=True)
            p = jnp.exp(s - m)
            denoms.append(jnp.sum(p, axis=-1, keepdims=True))
            probs.append(p.astype(MXU_DTYPE))
        for rl in range(tile_rows):
            vband = vb_ref[pl.ds(starts[rl], band), :]
            o = jnp.dot(probs[rl], vband, preferred_element_type=jnp.float32)
            q0 = pl.multiple_of((i * tile_rows + rl) * GRID_W, GRID_W)
            o_ref[pl.ds(q0, GRID_W), :] = (o / denoms[rl]).astype(o_ref.dtype)
        return 0

    lax.fori_loop(0, rows // tile_rows, tile_body, 0)


def _attention(proj3, table, *, d_rnn, d_attn):
    b, s, _ = proj3.shape
    rows = s // GRID_W
    n_heads = d_attn // HEAD_DIM
    q_blk = 2 * d_rnn // HEAD_DIM
    k_blk = q_blk + n_heads
    v_blk = k_blk + n_heads
    tile_rows = min(ATT_ROWS, rows)
    assert rows % tile_rows == 0 and rows >= WIN_R
    return pl.pallas_call(
        functools.partial(_attention_kernel, rows=rows, tile_rows=tile_rows),
        grid=(b, n_heads),
        in_specs=[
            pl.BlockSpec((None, s, HEAD_DIM), lambda bi, h: (bi, 0, q_blk + h)),
            pl.BlockSpec((None, s, HEAD_DIM), lambda bi, h: (bi, 0, k_blk + h)),
            pl.BlockSpec((None, s, HEAD_DIM), lambda bi, h: (bi, 0, v_blk + h)),
            pl.BlockSpec((None, WIN_R, GRID_W, WIN_R * GRID_W), lambda bi, h: (h, 0, 0, 0)),
        ],
        out_specs=pl.BlockSpec((None, s, HEAD_DIM), lambda bi, h: (bi, 0, h)),
        out_shape=jax.ShapeDtypeStruct((b, s, d_attn), MXU_DTYPE),
        scratch_shapes=[
            pltpu.VMEM((s, HEAD_DIM), MXU_DTYPE),
            pltpu.VMEM((s, HEAD_DIM), MXU_DTYPE),
        ],
        compiler_params=_cparams("parallel", "parallel"),
    )(proj3, proj3, proj3, table)


def _merge_kernel(ya_ref, att_ref, wl_ref, wa_ref, gl_ref, ga_ref, bm_ref, o_ref):
    y_lru = jnp.dot(ya_ref[...], wl_ref[...], preferred_element_type=jnp.float32)
    y_att = jnp.dot(att_ref[...], wa_ref[...], preferred_element_type=jnp.float32)
    s_lru = jax.nn.sigmoid(gl_ref[...] + bm_ref[pl.ds(0, 1), :])
    s_att = jax.nn.sigmoid(ga_ref[...] + bm_ref[pl.ds(1, 1), :])
    o_ref[...] = (s_lru * y_lru + s_att * y_att).astype(o_ref.dtype)


def _merge(ya2, att2, wl, wa, proj2, b_merge, *, gate_col):
    n, d = ya2.shape
    tm = min(MERGE_TM, n)
    tn = min(MERGE_TN, d)
    gl_blk = gate_col // tn
    ga_blk = (gate_col + d) // tn
    return pl.pallas_call(
        _merge_kernel,
        grid=(n // tm, d // tn),
        in_specs=[
            pl.BlockSpec((tm, d), lambda i, j: (i, 0)),
            pl.BlockSpec((tm, d), lambda i, j: (i, 0)),
            pl.BlockSpec((d, tn), lambda i, j: (0, j)),
            pl.BlockSpec((d, tn), lambda i, j: (0, j)),
            pl.BlockSpec((tm, tn), lambda i, j: (i, gl_blk + j)),
            pl.BlockSpec((tm, tn), lambda i, j: (i, ga_blk + j)),
            pl.BlockSpec((2, tn), lambda i, j: (0, j)),
        ],
        out_specs=pl.BlockSpec((tm, tn), lambda i, j: (i, j)),
        out_shape=jax.ShapeDtypeStruct((n, d), MXU_DTYPE),
        compiler_params=_cparams("parallel", "arbitrary"),
    )(ya2, att2, wl, wa, proj2, proj2, b_merge)


def _out_proj_kernel(m_ref, w_ref, x_ref, o_ref):
    o_ref[...] = x_ref[...] + jnp.dot(m_ref[...], w_ref[...], preferred_element_type=jnp.float32)


def _out_proj(m2, w_o, x2):
    n, d = m2.shape
    tm = min(MERGE_TM, n)
    tn = min(MERGE_TN, d)
    return pl.pallas_call(
        _out_proj_kernel,
        grid=(n // tm, d // tn),
        in_specs=[
            pl.BlockSpec((tm, d), lambda i, j: (i, 0)),
            pl.BlockSpec((d, tn), lambda i, j: (0, j)),
            pl.BlockSpec((tm, tn), lambda i, j: (i, j)),
        ],
        out_specs=pl.BlockSpec((tm, tn), lambda i, j: (i, j)),
        out_shape=jax.ShapeDtypeStruct((n, d), jnp.float32),
        compiler_params=_cparams("parallel", "arbitrary"),
    )(m2, w_o, x2)


GROUP_LANE0 = N_EXPERTS
META_E, META_W, META_RANK = 0, 2, 4
GROUP_SHIFT = EXPERTS_PER_GROUP.bit_length() - 1
TOP_K_SHIFT = TOP_K.bit_length() - 1
assert 1 << GROUP_SHIFT == EXPERTS_PER_GROUP and 1 << TOP_K_SHIFT == TOP_K


def _router_kernel(ha_ref, hb_ref, g_ref, w_ref, b_ref, xn_ref, meta_ref, cnt_ref, *, steps_a):
    step = pl.program_id(0)

    @pl.when(step == 0)
    def _():
        cnt_ref[...] = jnp.zeros_like(cnt_ref)

    x = jnp.where(step < steps_a, ha_ref[...], hb_ref[...])
    tm = x.shape[0]
    ms = jnp.mean(x * x, axis=-1, keepdims=True)
    xn = (x * lax.rsqrt(ms + EPS)) * g_ref[...]
    xn_ref[...] = xn
    logits = jnp.dot(xn, w_ref[...], preferred_element_type=jnp.float32,
                     precision=lax.Precision.HIGHEST) + b_ref[...]
    lane = lax.broadcasted_iota(jnp.int32, (tm, LANES), 1)
    neg_inf = -jnp.inf

    def first_argmax(vals, vmax):
        return jnp.min(jnp.where(vals == vmax, lane, LANES), axis=-1, keepdims=True)

    is_group = (lane >= GROUP_LANE0) & (lane < GROUP_LANE0 + N_GROUPS)
    lg = jnp.where(is_group, logits, neg_inf)
    mg = jnp.max(lg, axis=-1, keepdims=True)
    g_idx = first_argmax(lg, mg) - GROUP_LANE0
    g_val = 1.0 / jnp.sum(jnp.exp(lg - mg), axis=-1, keepdims=True)

    in_group = (lane < N_EXPERTS) & (lax.shift_right_logical(lane, GROUP_SHIFT) == g_idx)
    le = jnp.where(in_group, logits, neg_inf)
    m1 = jnp.max(le, axis=-1, keepdims=True)
    e1 = first_argmax(le, m1)
    le2 = jnp.where(lane == e1, neg_inf, le)
    m2 = jnp.max(le2, axis=-1, keepdims=True)
    e2 = first_argmax(le2, m2)
    z = jnp.sum(jnp.exp(le - m1), axis=-1, keepdims=True)
    p1 = 1.0 / z
    p2 = jnp.exp(m2 - m1) / z
    den = p1 + p2
    w1 = g_val * (p1 / den)
    w2 = g_val * (p2 / den)

    hot1 = lane == e1
    hot2 = lane == e2
    hot = hot1.astype(jnp.float32) + hot2.astype(jnp.float32)
    r_i = lax.broadcasted_iota(jnp.int32, (tm, tm), 0)
    c_i = lax.broadcasted_iota(jnp.int32, (tm, tm), 1)
    lower = (c_i < r_i).astype(MXU_DTYPE)
    before = jnp.dot(lower, hot.astype(MXU_DTYPE), preferred_element_type=jnp.float32) + cnt_ref[...]
    rank1 = jnp.sum(jnp.where(hot1, before, 0.0), axis=-1, keepdims=True)
    rank2 = jnp.sum(jnp.where(hot2, before, 0.0), axis=-1, keepdims=True)
    cnt_ref[...] += jnp.sum(hot, axis=0, keepdims=True)

    meta = jnp.where(lane == META_E, e1.astype(jnp.float32), 0.0)
    meta = jnp.where(lane == META_E + 1, e2.astype(jnp.float32), meta)
    meta = jnp.where(lane == META_W, w1, meta)
    meta = jnp.where(lane == META_W + 1, w2, meta)
    meta = jnp.where(lane == META_RANK, rank1, meta)
    meta = jnp.where(lane == META_RANK + 1, rank2, meta)
    meta_ref[...] = meta


def _router(h_a, h_b, g, w_router, b_router):
    n_a, d = h_a.shape
    n_b = h_b.shape[0]
    n = n_a + n_b
    tm = min(ROUTER_TM, n_a, n_b)
    steps_a = n_a // tm
    return pl.pallas_call(
        functools.partial(_router_kernel, steps_a=steps_a),
        grid=(n // tm,),
        in_specs=[
            pl.BlockSpec((tm, d), lambda i: (jnp.minimum(i, steps_a - 1), 0)),
            pl.BlockSpec((tm, d), lambda i: (jnp.maximum(i - steps_a, 0), 0)),
            pl.BlockSpec((1, d), lambda i: (0, 0)),
            pl.BlockSpec((d, LANES), lambda i: (0, 0)),
            pl.BlockSpec((1, LANES), lambda i: (0, 0)),
        ],
        out_specs=[
            pl.BlockSpec((tm, d), lambda i: (i, 0)),
            pl.BlockSpec((tm, LANES), lambda i: (i, 0)),
            pl.BlockSpec((1, LANES), lambda i: (0, 0)),
        ],
        out_shape=[
            jax.ShapeDtypeStruct((n, d), jnp.float32),
            jax.ShapeDtypeStruct((n, LANES), jnp.float32),
            jax.ShapeDtypeStruct((1, LANES), jnp.float32),
        ],
        compiler_params=_cparams("arbitrary"),
    )(h_a, h_b, g.reshape(1, d), w_router, b_router)


def _experts_kernel(pos_ref, start_ref, count_ref, rows_ref, nused_ref, x_hbm, wg_ref, wu_ref, wd_ref, y_hbm,
                    tok_ref, xbuf, obuf, gsem, osem, *, bm, n_blocks):
    e = pl.program_id(0)
    n_used = nused_ref[0]
    b0 = start_ref[e]

    def start_gather(blk, slot):
        base = blk * bm
        for r in range(bm):
            tok = tok_ref[base + r]
            pltpu.make_async_copy(x_hbm.at[pl.ds(tok, 1), :], xbuf.at[slot, pl.ds(r, 1), :],
                                  gsem.at[slot]).start()

    def wait_gather(slot):
        pltpu.make_async_copy(x_hbm.at[pl.ds(0, bm), :], xbuf.at[slot], gsem.at[slot]).wait()

    def out_copy(blk, slot):
        return pltpu.make_async_copy(obuf.at[slot], y_hbm.at[pl.ds(blk * bm, bm), :], osem.at[slot])

    @pl.when(e == 0)
    def _():
        n_assign = pos_ref.shape[0]

        def clear_padding(ex, _):
            def clear(i, _):
                tok_ref[i] = 0
                return 0
            row0 = start_ref[ex] * bm
            lax.fori_loop(row0 + rows_ref[ex], row0 + count_ref[ex] * bm, clear, 0)
            return 0

        def invert(i8, _):
            for k in range(SUBLANES):
                a = i8 * SUBLANES + k
                tok_ref[pos_ref[a]] = lax.shift_right_logical(a, TOP_K_SHIFT)
            return 0

        lax.fori_loop(0, pl.num_programs(0), clear_padding, 0)
        lax.fori_loop(0, n_assign // SUBLANES, invert, 0)
        start_gather(0, 0)

    def block_body(j, _):
        b = b0 + j
        slot = b % 2

        @pl.when(b + 1 < n_used)
        def _():
            start_gather(b + 1, 1 - slot)

        wait_gather(slot)

        @pl.when(b >= 2)
        def _():
            out_copy(b - 2, slot).wait()

        x = xbuf[slot].astype(MXU_DTYPE)
        hg = jnp.dot(x, wg_ref[...].astype(MXU_DTYPE), preferred_element_type=jnp.float32)
        hu = jnp.dot(x, wu_ref[...].astype(MXU_DTYPE), preferred_element_type=jnp.float32)
        hid = (jax.nn.silu(hg) * hu).astype(MXU_DTYPE)
        obuf[slot] = jnp.dot(hid, wd_ref[...].astype(MXU_DTYPE), preferred_element_type=jnp.float32)
        out_copy(b, slot).start()
        return 0

    lax.fori_loop(0, count_ref[e], block_body, 0)

    @pl.when(e == pl.num_programs(0) - 1)
    def _():
        @pl.when(n_used >= 2)
        def _():
            out_copy(n_used - 2, n_used % 2).wait()

        out_copy(n_used - 1, (n_used - 1) % 2).wait()
        obuf[0] = jnp.zeros(obuf.shape[1:], jnp.float32)

        def zero_fill(b, _):
            out_copy(b, 0).start()
            out_copy(b, 0).wait()
            return 0

        lax.fori_loop(n_used, n_blocks, zero_fill, 0)


def _experts(xn2, pos, blk_start, blk_count, exp_rows, n_used, w_gate, w_up, w_down, *, n_blocks):
    n, d = xn2.shape
    n_experts, _, f = w_gate.shape
    bm = MOE_BM
    grid_spec = pltpu.PrefetchScalarGridSpec(
        num_scalar_prefetch=5,
        grid=(n_experts,),
        in_specs=[
            pl.BlockSpec(memory_space=pl.ANY),
            pl.BlockSpec((None, d, f), lambda e, *_: (e, 0, 0)),
            pl.BlockSpec((None, d, f), lambda e, *_: (e, 0, 0)),
            pl.BlockSpec((None, f, d), lambda e, *_: (e, 0, 0)),
        ],
        out_specs=pl.BlockSpec(memory_space=pl.ANY),
        scratch_shapes=[
            pltpu.SMEM((n_blocks * bm,), jnp.int32),
            pltpu.VMEM((2, bm, d), jnp.float32),
            pltpu.VMEM((2, bm, d), jnp.float32),
            pltpu.SemaphoreType.DMA((2,)),
            pltpu.SemaphoreType.DMA((2,)),
        ],
    )
    return pl.pallas_call(
        functools.partial(_experts_kernel, bm=bm, n_blocks=n_blocks),
        grid_spec=grid_spec,
        out_shape=jax.ShapeDtypeStruct((n_blocks * bm, d), jnp.float32),
        compiler_params=_gather_cparams(EXPERTS_VMEM_LIMIT_BYTES),
    )(pos, blk_start, blk_count, exp_rows, n_used, xn2, w_gate, w_up, w_down)


def _combine_kernel(pos_ref, y_hbm, h_ref, meta_ref, g_ref, o_ref, ybuf, sem, *, tm, n_steps):
    step = pl.program_id(0)

    def start_gather(blk, slot):
        base = blk * tm * TOP_K
        for r in range(tm):
            for k in range(TOP_K):
                pos = pos_ref[base + r * TOP_K + k]
                pltpu.make_async_copy(y_hbm.at[pl.ds(pos, 1), :], ybuf.at[slot, k, pl.ds(r, 1), :],
                                      sem.at[slot]).start()

    def wait_gather(slot):
        for k in range(TOP_K):
            pltpu.make_async_copy(y_hbm.at[pl.ds(0, tm), :], ybuf.at[slot, k], sem.at[slot]).wait()

    slot = step % 2

    @pl.when(step == 0)
    def _():
        start_gather(0, 0)

    @pl.when(step + 1 < n_steps)
    def _():
        start_gather(step + 1, 1 - slot)

    wait_gather(slot)
    w1 =meta_ref[:, META_W:META_W + 1]
    w2 = meta_ref[:, META_W + 1:META_W + 2]
    h = h_ref[...] + (ybuf[slot, 0] * w1 + ybuf[slot, 1] * w2)
    ms = jnp.mean(h * h, axis=-1, keepdims=True)
    o_ref[...] = (h * lax.rsqrt(ms + EPS)) * g_ref[...]


def _combine(pos, yb, h2, meta, g):
    n, d = h2.shape
    tm = min(COMBINE_TM, n)
    n_steps = n // tm
    grid_spec = pltpu.PrefetchScalarGridSpec(
        num_scalar_prefetch=1,
        grid=(n_steps,),
        in_specs=[
            pl.BlockSpec(memory_space=pl.ANY),
            pl.BlockSpec((tm, d), lambda i, pos: (i, 0)),
            pl.BlockSpec((tm, LANES), lambda i, pos: (i, 0)),
            pl.BlockSpec((1, d), lambda i, pos: (0, 0)),
        ],
        out_specs=pl.BlockSpec((tm, d), lambda i, pos: (i, 0)),
        scratch_shapes=[
            pltpu.VMEM((2, TOP_K, tm, d), jnp.float32),
            pltpu.SemaphoreType.DMA((2,)),
        ],
    )
    return pl.pallas_call(
        functools.partial(_combine_kernel, tm=tm, n_steps=n_steps),
        grid_spec=grid_spec,
        out_shape=jax.ShapeDtypeStruct((n, d), jnp.float32),
        compiler_params=_gather_cparams(VMEM_LIMIT_BYTES),
    )(pos, yb, h2, meta, g.reshape(1, d))


def _dispatch_plan(meta, counts_f, n):
    bm = MOE_BM
    a = n * TOP_K
    expert = meta[:, META_E:META_E + TOP_K].astype(jnp.int32)
    rank = meta[:, META_RANK:META_RANK + TOP_K].astype(jnp.int32)
    counts = counts_f[0, :N_EXPERTS].astype(jnp.int32)
    padded = (counts + bm - 1) // bm * bm
    pad_end = jnp.cumsum(padded)
    pad_start = pad_end - padded
    pos = (pad_start[expert] + rank).reshape(a)
    n_blocks = -(-a // bm) + N_EXPERTS
    blk_start = (pad_start // bm).astype(jnp.int32)
    blk_count = (padded // bm).astype(jnp.int32)
    n_used = (pad_end[-1] // bm).astype(jnp.int32).reshape(1)
    return pos, blk_start, blk_count, counts, n_used, n_blocks


def _mixer_residual(x, p):
    b, s, d = x.shape
    n = b * s
    d_rnn = p["d_rnn"]
    d_attn = p["d_attn"]
    x2 = x.reshape(n, d)
    proj2 = _norm_proj(x2, p["norm_mix_g"], p["w_in"])
    d_in = proj2.shape[1]
    proj3 = proj2.reshape(b, s, d_in)
    hf = _lru_direction(proj3, p["conv_w"], p["conv_b"], p["lru_wg"][0], p["lru_b_a"][0], p["lru_b_x"][0],
                        p["lru_lambda"][0], None, reverse=False, d_rnn=d_rnn)
    ya = _lru_direction(proj3, p["conv_w"], p["conv_b"], p["lru_wg"][1], p["lru_b_a"][1], p["lru_b_x"][1],
                        p["lru_lambda"][1], hf, reverse=True, d_rnn=d_rnn)
    att = _attention(proj3, p["att_table"], d_rnn=d_rnn, d_attn=d_attn)
    m2 = _merge(ya.reshape(n, d_rnn), att.reshape(n, d_attn), p["w_lru_up"], p["w_attn_up"], proj2,
                p["b_merge"], gate_col=2 * d_rnn + 3 * d_attn)
    return _out_proj(m2, p["w_o"], x2)


def _moe_and_final_norm(h_a, h_b, p):
    n_a, n_b = h_a.shape[0], h_b.shape[0]
    xn2, meta, counts = _router(h_a, h_b, p["norm_ffn_g"], p["w_router"], p["b_router"])
    pos, blk_start, blk_count, exp_rows, n_used, n_blocks = _dispatch_plan(meta, counts, n_a + n_b)
    yb = _experts(xn2, pos, blk_start, blk_count, exp_rows, n_used, p["w_exp_gate"], p["w_exp_up"], p["w_exp_down"],
                  n_blocks=n_blocks)
    y_a = _combine(pos[:n_a * TOP_K], yb, h_a, meta[:n_a], p["norm_final_g"])
    y_b = _combine(pos[n_a * TOP_K:], yb, h_b, meta[n_a:], p["norm_final_g"])
    return y_a, y_b


def _prepare_params(norm_mix_g, w_in, conv_w, conv_b, lru_w_a, lru_b_a, lru_w_x, lru_b_x, lru_lambda, rpb,
                    w_lru_up, w_attn_up, b_merge, w_o, norm_ffn_g, w_router_group, b_router_group,
                    w_router_expert, b_router_expert, w_exp_gate, w_exp_up, w_exp_down, norm_final_g):
    bf16 = MXU_DTYPE
    d = w_in.shape[1]
    d_rnn = conv_w.shape[2]
    d_attn = w_attn_up.shape[1]
    lru_wg = jnp.concatenate([lru_w_a[0], lru_w_x[0]], axis=-1).astype(bf16)
    w_router = jnp.zeros((d, LANES), jnp.float32)
    w_router = w_router.at[:, :N_EXPERTS].set(w_router_expert[0])
    w_router = w_router.at[:, GROUP_LANE0:GROUP_LANE0 + N_GROUPS].set(w_router_group[0])
    b_router = jnp.zeros((1, LANES), jnp.float32)
    b_router = b_router.at[0, :N_EXPERTS].set(b_router_expert[0])
    b_router = b_router.at[0, GROUP_LANE0:GROUP_LANE0 + N_GROUPS].set(b_router_group[0])
    return dict(
        d_rnn=d_rnn, d_attn=d_attn,
        norm_mix_g=norm_mix_g[0], w_in=w_in[0].astype(bf16),
        conv_w=conv_w[0], conv_b=conv_b[0], lru_wg=lru_wg,
        lru_b_a=lru_b_a[0], lru_b_x=lru_b_x[0], lru_lambda=lru_lambda[0],
        att_table=_attention_bias_table(rpb[0]),
        w_lru_up=w_lru_up[0].astype(bf16), w_attn_up=w_attn_up[0].astype(bf16),
        b_merge=b_merge[0], w_o=w_o[0].astype(bf16),
        norm_ffn_g=norm_ffn_g[0], w_router=w_router, b_router=b_router,
        w_exp_gate=w_exp_gate[0], w_exp_up=w_exp_up[0], w_exp_down=w_exp_down[0],
        norm_final_g=norm_final_g,
    )


def kernel(x_prompt, x_sample, norm_mix_g, w_in, conv_w, conv_b, lru_w_a, lru_b_a, lru_w_x, lru_b_x, lru_lambda, rpb, w_lru_up, w_attn_up, b_merge, w_o, norm_ffn_g, w_router_group, b_router_group, w_router_expert, b_router_expert, w_exp_gate, w_exp_up, w_exp_down, norm_final_g):
    assert norm_mix_g.shape[0] == 1, "single-layer encoder"
    p = _prepare_params(norm_mix_g, w_in, conv_w, conv_b, lru_w_a, lru_b_a, lru_w_x, lru_b_x, lru_lambda, rpb,
                        w_lru_up, w_attn_up, b_merge, w_o, norm_ffn_g, w_router_group, b_router_group,
                        w_router_expert, b_router_expert, w_exp_gate, w_exp_up, w_exp_down, norm_final_g)
    h_prompt = _mixer_residual(x_prompt, p)
    h_sample = _mixer_residual(x_sample, p)
    y_prompt, y_sample = _moe_and_final_norm(h_prompt, h_sample, p)
    return (y_prompt.reshape(x_prompt.shape), y_sample.reshape(x_sample.shape))
```

```python
import functools

import numpy as np
import jax
import jax.numpy as jnp
from jax import lax
from jax.experimental import pallas as pl
from jax.experimental.pallas import tpu as pltpu

GRID_W = 64
WIN_R = 8
WIN_C = 16
CONV_W = 4
LRU_C = 8.0
LRU_BLOCK = 128
HEAD_DIM = 128
N_GROUPS = 4
EXPERTS_PER_GROUP = 8
N_EXPERTS = N_GROUPS * EXPERTS_PER_GROUP
TOP_K = 2
EPS = 1e-6
MASK_VALUE = -1e30

MXU_DTYPE = jnp.bfloat16

LANES = 128
SUBLANES = 8
VMEM_LIMIT_BYTES = 56 * 1024 * 1024
EXPERTS_VMEM_LIMIT_BYTES = 62 * 1024 * 1024

PROJ_TM = 1024
PROJ_TN = 1024
MERGE_TM = 1024
MERGE_TN = 512
LRU_TC = 4096
LRU_GROUPS = 4
LRU_UNROLL = 4
ATT_ROWS = 32
ROUTER_TM = 512
MOE_BM = 256
COMBINE_TM = 256
GATHER_SLOTS = 3


def _cparams(*sem):
    return pltpu.CompilerParams(dimension_semantics=sem, vmem_limit_bytes=VMEM_LIMIT_BYTES)


def _gather_cparams(vmem_limit):
    return pltpu.CompilerParams(dimension_semantics=("arbitrary",), vmem_limit_bytes=vmem_limit)


def _norm_proj_kernel(x_ref, g_ref, w_ref, o_ref, xn_ref):
    @pl.when(pl.program_id(1) == 0)
    def _():
        x = x_ref[...]
        ms = jnp.mean(x * x, axis=-1, keepdims=True)
        xn_ref[...] = ((x * lax.rsqrt(ms + EPS)) * g_ref[...]).astype(MXU_DTYPE)

    o_ref[...] = jnp.dot(xn_ref[...], w_ref[...], preferred_element_type=jnp.float32)


def _norm_proj(x2, g, w_bf16):
    n, d = x2.shape
    d_out = w_bf16.shape[1]
    tm = min(PROJ_TM, n)
    tn = min(PROJ_TN, d_out)
    return pl.pallas_call(
        _norm_proj_kernel,
        grid=(n // tm, d_out // tn),
        in_specs=[
            pl.BlockSpec((tm, d), lambda i, j: (i, 0)),
            pl.BlockSpec((1, d), lambda i, j: (0, 0)),
            pl.BlockSpec((d, tn), lambda i, j: (0, j)),
        ],
        out_specs=pl.BlockSpec((tm, tn), lambda i, j: (i, j)),
        out_shape=jax.ShapeDtypeStruct((n, d_out), jnp.float32),
        scratch_shapes=[pltpu.VMEM((tm, d), MXU_DTYPE)],
        compiler_params=_cparams("parallel", "arbitrary"),
    )(x2, g.reshape(1, d), w_bf16)


def _softplus(y):
    return jnp.maximum(y, 0.0) + jnp.log1p(jnp.exp(-jnp.abs(y)))


def _lru_pitch(tc):
    seg_len = tc // (LRU_GROUPS * SUBLANES)
    assert seg_len * LRU_GROUPS * SUBLANES == tc and seg_len % SUBLANES == 0
    return seg_len + SUBLANES // 2


def _lru_kernel(*refs, reverse, n_chunks, tc):
    if reverse:
        (x_ref, xp_ref, xn_ref, cw_ref, cb_ref, wg_ref, ba_ref, bx_ref, lam_ref,
         hf_ref, gate_ref, o_ref, xpad_ref, a_ref, u_ref, h_ref, carry_ref) = refs
    else:
        (x_ref, xp_ref, xn_ref, cw_ref, cb_ref, wg_ref, ba_ref, bx_ref, lam_ref,
         o_ref, xpad_ref, a_ref, u_ref, h_ref, carry_ref) = refs
    t = pl.program_id(2)
    chunk = (n_chunks - 1 - t) if reverse else t

    @pl.when(t == 0)
    def _():
        carry_ref[...] = jnp.zeros_like(carry_ref)

    prev = jnp.where(chunk == 0, 0.0, xp_ref[...])
    nxt = jnp.where(chunk == n_chunks - 1, 0.0, xn_ref[...])
    xpad_ref[pl.ds(0, SUBLANES), :] = prev
    xpad_ref[pl.ds(SUBLANES, tc), :] = x_ref[...]
    xpad_ref[pl.ds(SUBLANES + tc, SUBLANES), :] = nxt
    left = CONV_W // 2
    xc = cb_ref[...]
    for tap in range(CONV_W):
        xc = xc + xpad_ref[pl.ds(SUBLANES - left + tap, tc), :] * cw_ref[pl.ds(tap, 1), :]

    z = jnp.dot(xc.astype(MXU_DTYPE), wg_ref[...], preferred_element_type=jnp.float32)
    r = jax.nn.sigmoid(z[:, :LRU_BLOCK] + ba_ref[...])
    i = jax.nn.sigmoid(z[:, LRU_BLOCK:] + bx_ref[...])
    log_a = (-LRU_C * r) * _softplus(-lam_ref[...])
    a = jnp.exp(log_a)
    u = jnp.sqrt(jnp.tanh(-log_a) * (1.0 + a * a)) * (i * xc)

    pitch = _lru_pitch(tc)
    slots = LRU_GROUPS * SUBLANES * pitch
    a_ref[pl.ds(0, tc), :] = a
    u_ref[pl.ds(0, tc), :] = u
    a_ref[pl.ds(tc, slots - tc), :] = jnp.ones((slots - tc, LRU_BLOCK), jnp.float32)
    u_ref[pl.ds(tc, slots - tc), :] = jnp.zeros((slots - tc, LRU_BLOCK), jnp.float32)

    def seg_rows(g, k):
        return pl.ds(g * SUBLANES * pitch + k, SUBLANES, stride=pitch)

    def slot_of(kk):
        return (pitch - 1 - kk) if reverse else kk

    def scan_step(kk, carry):
        hs, ps = carry
        k = slot_of(kk)
        new_h, new_p = [], []
        for g in range(LRU_GROUPS):
            a_k = a_ref[seg_rows(g, k), :]
            h = a_k * hs[g] + u_ref[seg_rows(g, k), :]
            h_ref[seg_rows(g, k), :] = h
            new_h.append(h)
            new_p.append(a_k * ps[g])
        return tuple(new_h), tuple(new_p)

    zeros = tuple(jnp.zeros((SUBLANES, LANES), jnp.float32) for _ in range(LRU_GROUPS))
    ones = tuple(jnp.ones((SUBLANES, LANES), jnp.float32) for _ in range(LRU_GROUPS))
    h_end, p_end = lax.fori_loop(0, pitch, scan_step, (zeros, ones), unroll=LRU_UNROLL)

    row = lax.broadcasted_iota(jnp.int32, (SUBLANES, LANES), 0)
    c_in = [None] * LRU_GROUPS
    cg = carry_ref[...]
    for g in (reversed(range(LRU_GROUPS)) if reverse else range(LRU_GROUPS)):
        pa, hb = p_end[g], h_end[g]
        for s in (1, 2, 4):
            keep = (row < SUBLANES - s) if reverse else (row >= s)
            shift = (SUBLANES - s) if reverse else s
            a_sh = jnp.where(keep, pltpu.roll(pa, shift, 0), 1.0)
            b_sh = jnp.where(keep, pltpu.roll(hb, shift, 0), 0.0)
            hb = pa * b_sh + hb
            pa = pa * a_sh
        seg_out = pa * cg + hb
        if reverse:
            c_in[g] = jnp.where(row == SUBLANES - 1, cg, pltpu.roll(seg_out, SUBLANES - 1, 0))
            cg = seg_out[0:1, :]
        else:
            c_in[g] = jnp.where(row == 0, cg, pltpu.roll(seg_out, 1, 0))
            cg = seg_out[SUBLANES - 1:SUBLANES, :]
    carry_ref[...] = cg

    def fix_step(kk, ps):
        k = slot_of(kk)
        new_p = []
        for g in range(LRU_GROUPS):
            p = a_ref[seg_rows(g, k), :] * ps[g]
            u_ref[seg_rows(g, k), :] = h_ref[seg_rows(g, k), :] + p * c_in[g]
            new_p.append(p)
        return tuple(new_p)

    lax.fori_loop(0, pitch, fix_step, ones, unroll=LRU_UNROLL)

    h_all = u_ref[pl.ds(0, tc), :]
    if reverse:
        o_ref[...] = (jax.nn.gelu(gate_ref[...]) * (hf_ref[...] + h_all)).astype(o_ref.dtype)
    else:
        o_ref[...] = h_all


def _lru_direction(proj3, conv_w, conv_b, wg, ba, bx, lam, hf, *, reverse, d_rnn):
    b, s, _ = proj3.shape
    n_cb = d_rnn // LRU_BLOCK
    tc = min(LRU_TC, s)
    n_chunks = s // tc
    tc8 = tc // SUBLANES
    s8 = s // SUBLANES
    slots = LRU_GROUPS * SUBLANES * _lru_pitch(tc)

    def chunk_of(t):
        return (n_chunks - 1 - t) if reverse else t

    vec_spec = pl.BlockSpec((1, LRU_BLOCK), lambda bi, c, t: (0, c))
    in_specs = [
        pl.BlockSpec((None, tc, LRU_BLOCK), lambda bi, c, t: (bi, chunk_of(t), c)),
        pl.BlockSpec((None, SUBLANES, LRU_BLOCK),
                     lambda bi, c, t: (bi, jnp.maximum(chunk_of(t) * tc8 - 1, 0), c)),
        pl.BlockSpec((None, SUBLANES, LRU_BLOCK),
                     lambda bi, c, t: (bi, jnp.minimum((chunk_of(t) + 1) * tc8, s8 - 1), c)),
        pl.BlockSpec((CONV_W, LRU_BLOCK), lambda bi, c, t: (0, c)),
        vec_spec,
        pl.BlockSpec((None, LRU_BLOCK, 2 * LRU_BLOCK), lambda bi, c, t: (c, 0, 0)),
        vec_spec, vec_spec, vec_spec,
    ]
    args = [proj3, proj3, proj3, conv_w, conv_b.reshape(1, d_rnn), wg,
            ba.reshape(1, d_rnn), bx.reshape(1, d_rnn), lam.reshape(1, d_rnn)]
    if reverse:
        in_specs += [
            pl.BlockSpec((None, tc, LRU_BLOCK), lambda bi, c, t: (bi, chunk_of(t), c)),
            pl.BlockSpec((None, tc, LRU_BLOCK), lambda bi, c, t: (bi, chunk_of(t), n_cb + c)),
        ]
        args += [hf, proj3]
        out_dtype = MXU_DTYPE
    else:
        out_dtype = jnp.float32
    return pl.pallas_call(
        functools.partial(_lru_kernel, reverse=reverse, n_chunks=n_chunks, tc=tc),
        grid=(b, n_cb, n_chunks),
        in_specs=in_specs,
        out_specs=pl.BlockSpec((None, tc, LRU_BLOCK), lambda bi, c, t: (bi, chunk_of(t), c)),
        out_shape=jax.ShapeDtypeStruct((b, s, d_rnn), out_dtype),
        scratch_shapes=[
            pltpu.VMEM((tc + 2 * SUBLANES, LRU_BLOCK), jnp.float32),
            pltpu.VMEM((slots, LRU_BLOCK), jnp.float32),
            pltpu.VMEM((slots, LRU_BLOCK), jnp.float32),
            pltpu.VMEM((slots, LRU_BLOCK), jnp.float32),
            pltpu.VMEM((1, LRU_BLOCK), jnp.float32),
        ],
        compiler_params=_cparams("parallel", "parallel", "arbitrary"),
    )(*args)


def _attention_bias_table(rpb):
    n_heads, n_dr, n_dc = rpb.shape
    cq = np.arange(GRID_W)
    cs = np.clip(cq - WIN_C // 2, 0, GRID_W - WIN_C)
    ck = np.arange(GRID_W)
    valid = (ck[None, :] >= cs[:, None]) & (ck[None, :] < cs[:, None] + WIN_C)
    pad = GRID_W - WIN_C
    width = n_dc + 2 * pad + 1
    padded = jnp.pad(rpb.astype(jnp.float32), ((0, 0), (0, 0), (pad, pad + 1)))
    tiled = jnp.tile(padded, (1, 1, GRID_W))[:, :, :GRID_W * (width - 1)]
    skew = tiled.reshape(n_heads, n_dr, GRID_W, width - 1)
    toep = skew[:, :, :, GRID_W - 1:2 * GRID_W - 1]
    toep = jnp.where(valid[None, None], toep, MASK_VALUE)
    per_off = [toep[:, WIN_R - 1 - off:2 * WIN_R - 1 - off] for off in range(WIN_R)]
    tab = jnp.stack(per_off, axis=1)
    tab = tab.transpose(0, 1, 3, 2, 4)
    return tab.reshape(n_heads, WIN_R, GRID_W, WIN_R * GRID_W)


def _attention_kernel(q_ref, k_ref, v_ref, tb_ref, o_ref, kb_ref, vb_ref, *, rows, tile_rows):
    kb_ref[...] = k_ref[...].astype(MXU_DTYPE)
    vb_ref[...] = v_ref[...].astype(MXU_DTYPE)
    scale = HEAD_DIM ** -0.5
    band = WIN_R * GRID_W

    def tile_body(i, _):
        starts, scores = [], []
        for rl in range(tile_rows):
            r = i * tile_rows + rl
            rs = jnp.clip(r - WIN_R // 2, 0, rows - WIN_R)
            start = pl.multiple_of(rs * GRID_W, GRID_W)
            q0 = pl.multiple_of(r * GRID_W, GRID_W)
            q = (q_ref[pl.ds(q0, GRID_W), :] * scale).astype(MXU_DTYPE)
            kband = kb_ref[pl.ds(start, band), :]
            s = lax.dot_general(q, kband, (((1,), (1,)), ((), ())), preferred_element_type=jnp.float32)
            starts.append(start)
            scores.append(s + tb_ref[r - rs])
        probs, denoms = [], []
        for s in scores:
            m = jnp.max(s, axis=-1, keepdims=True)
            p = jnp.exp(s - m)
            denoms.append(jnp.sum(p, axis=-1, keepdims=True))
            probs.append(p.astype(MXU_DTYPE))
        for rl in range(tile_rows):
            vband = vb_ref[pl.ds(starts[rl], band), :]
            o = jnp.dot(probs[rl], vband, preferred_element_type=jnp.float32)
            q0 = pl.multiple_of((i * tile_rows + rl) * GRID_W, GRID_W)
            o_ref[pl.ds(q0, GRID_W), :] = (o / denoms[rl]).astype(o_ref.dtype)
        return 0

    lax.fori_loop(0, rows // tile_rows, tile_body, 0)


def _attention(proj3, table, *, d_rnn, d_attn):
    b, s, _ = proj3.shape
    rows = s // GRID_W
    n_heads = d_attn // HEAD_DIM
    q_blk = 2 * d_rnn // HEAD_DIM
    k_blk = q_blk + n_heads
    v_blk = k_blk + n_heads
    tile_rows = min(ATT_ROWS, rows)
    assert rows % tile_rows == 0 and rows >= WIN_R
    return pl.pallas_call(
        functools.partial(_attention_kernel, rows=rows, tile_rows=tile_rows),
        grid=(b, n_heads),
        in_specs=[
            pl.BlockSpec((None, s, HEAD_DIM), lambda bi, h: (bi, 0, q_blk + h)),
            pl.BlockSpec((None, s, HEAD_DIM), lambda bi, h: (bi, 0, k_blk + h)),
            pl.BlockSpec((None, s, HEAD_DIM), lambda bi, h: (bi, 0, v_blk + h)),
            pl.BlockSpec((None, WIN_R, GRID_W, WIN_R * GRID_W), lambda bi, h: (h, 0, 0, 0)),
        ],
        out_specs=pl.BlockSpec((None, s, HEAD_DIM), lambda bi, h: (bi, 0, h)),
        out_shape=jax.ShapeDtypeStruct((b, s, d_attn), MXU_DTYPE),
        scratch_shapes=[
            pltpu.VMEM((s, HEAD_DIM), MXU_DTYPE),
            pltpu.VMEM((s, HEAD_DIM), MXU_DTYPE),
        ],
        compiler_params=_cparams("parallel", "parallel"),
    )(proj3, proj3, proj3, table)


def _merge_kernel(ya_ref, att_ref, wl_ref, wa_ref, gl_ref, ga_ref, bm_ref, o_ref):
    y_lru = jnp.dot(ya_ref[...], wl_ref[...], preferred_element_type=jnp.float32)
    y_att = jnp.dot(att_ref[...], wa_ref[...], preferred_element_type=jnp.float32)
    s_lru = jax.nn.sigmoid(gl_ref[...] + bm_ref[pl.ds(0, 1), :])
    s_att = jax.nn.sigmoid(ga_ref[...] + bm_ref[pl.ds(1, 1), :])
    o_ref[...] = (s_lru * y_lru + s_att * y_att).astype(o_ref.dtype)


def _merge(ya2, att2, wl, wa, proj2, b_merge, *, gate_col):
    n, d = ya2.shape
    tm = min(MERGE_TM, n)
    tn = min(MERGE_TN, d)
    gl_blk = gate_col // tn
    ga_blk = (gate_col + d) // tn
    return pl.pallas_call(
        _merge_kernel,
        grid=(n // tm, d // tn),
        in_specs=[
            pl.BlockSpec((tm, d), lambda i, j: (i, 0)),
            pl.BlockSpec((tm, d), lambda i, j: (i, 0)),
            pl.BlockSpec((d, tn), lambda i, j: (0, j)),
            pl.BlockSpec((d, tn), lambda i, j: (0, j)),
            pl.BlockSpec((tm, tn), lambda i, j: (i, gl_blk + j)),
            pl.BlockSpec((tm, tn), lambda i, j: (i, ga_blk + j)),
            pl.BlockSpec((2, tn), lambda i, j: (0, j)),
        ],
        out_specs=pl.BlockSpec((tm, tn), lambda i, j: (i, j)),
        out_shape=jax.ShapeDtypeStruct((n, d), MXU_DTYPE),
        compiler_params=_cparams("parallel", "arbitrary"),
    )(ya2, att2, wl, wa, proj2, proj2, b_merge)


def _out_proj_kernel(m_ref, w_ref, x_ref, o_ref):
    o_ref[...] = x_ref[...] + jnp.dot(m_ref[...], w_ref[...], preferred_element_type=jnp.float32)


def _out_proj(m2, w_o, x2):
    n, d = m2.shape
    tm = min(MERGE_TM, n)
    tn = min(MERGE_TN, d)
    return pl.pallas_call(
        _out_proj_kernel,
        grid=(n // tm, d // tn),
        in_specs=[
            pl.BlockSpec((tm, d), lambda i, j: (i, 0)),
            pl.BlockSpec((d, tn), lambda i, j: (0, j)),
            pl.BlockSpec((tm, tn), lambda i, j: (i, j)),
        ],
        out_specs=pl.BlockSpec((tm, tn), lambda i, j: (i, j)),
        out_shape=jax.ShapeDtypeStruct((n, d), jnp.float32),
        compiler_params=_cparams("parallel", "arbitrary"),
    )(m2, w_o, x2)


GROUP_LANE0 = N_EXPERTS
META_E, META_W, META_RANK = 0, 2, 4
GROUP_SHIFT = EXPERTS_PER_GROUP.bit_length() - 1
TOP_K_SHIFT = TOP_K.bit_length() - 1
assert 1 << GROUP_SHIFT == EXPERTS_PER_GROUP and 1 << TOP_K_SHIFT == TOP_K


def _router_kernel(ha_ref, hb_ref, g_ref, w_ref, b_ref, xn_ref, meta_ref, cnt_ref, *, steps_a):
    step = pl.program_id(0)

    @pl.when(step == 0)
    def _():
        cnt_ref[...] = jnp.zeros_like(cnt_ref)

    x = jnp.where(step < steps_a, ha_ref[...], hb_ref[...])
    tm = x.shape[0]
    ms = jnp.mean(x * x, axis=-1, keepdims=True)
    xn = (x * lax.rsqrt(ms + EPS)) * g_ref[...]
    xn_ref[...] = xn
    logits = jnp.dot(xn, w_ref[...], preferred_element_type=jnp.float32,
                     precision=lax.Precision.HIGHEST) + b_ref[...]
    lane = lax.broadcasted_iota(jnp.int32, (tm, LANES), 1)
    neg_inf = -jnp.inf

    def first_argmax(vals, vmax):
        return jnp.min(jnp.where(vals == vmax, lane, LANES), axis=-1, keepdims=True)

    is_group = (lane >= GROUP_LANE0) & (lane < GROUP_LANE0 + N_GROUPS)
    lg = jnp.where(is_group, logits, neg_inf)
    mg = jnp.max(lg, axis=-1, keepdims=True)
    g_idx = first_argmax(lg, mg) - GROUP_LANE0
    g_val = 1.0 / jnp.sum(jnp.exp(lg - mg), axis=-1, keepdims=True)

    in_group = (lane < N_EXPERTS) & (lax.shift_right_logical(lane, GROUP_SHIFT) == g_idx)
    le = jnp.where(in_group, logits, neg_inf)
    m1 = jnp.max(le, axis=-1, keepdims=True)
    e1 = first_argmax(le, m1)
    le2 = jnp.where(lane == e1, neg_inf, le)
    m2 = jnp.max(le2, axis=-1, keepdims=True)
    e2 = first_argmax(le2, m2)
    z = jnp.sum(jnp.exp(le - m1), axis=-1, keepdims=True)
    p1 = 1.0 / z
    p2 = jnp.exp(m2 - m1) / z
    den = p1 + p2
    w1 = g_val * (p1 / den)
    w2 = g_val * (p2 / den)

    hot1 = lane == e1
    hot2 = lane == e2
    hot = hot1.astype(jnp.float32) + hot2.astype(jnp.float32)
    r_i = lax.broadcasted_iota(jnp.int32, (tm, tm), 0)
    c_i = lax.broadcasted_iota(jnp.int32, (tm, tm), 1)
    lower = (c_i < r_i).astype(MXU_DTYPE)
    before = jnp.dot(lower, hot.astype(MXU_DTYPE), preferred_element_type=jnp.float32) + cnt_ref[...]
    rank1 = jnp.sum(jnp.where(hot1, before, 0.0), axis=-1, keepdims=True)
    rank2 = jnp.sum(jnp.where(hot2, before, 0.0), axis=-1, keepdims=True)
    cnt_ref[...] += jnp.sum(hot, axis=0, keepdims=True)

    meta = jnp.where(lane == META_E, e1.astype(jnp.float32), 0.0)
    meta = jnp.where(lane == META_E + 1, e2.astype(jnp.float32), meta)
    meta = jnp.where(lane == META_W, w1, meta)
    meta = jnp.where(lane == META_W + 1, w2, meta)
    meta = jnp.where(lane == META_RANK, rank1, meta)
    meta = jnp.where(lane == META_RANK + 1, rank2, meta)
    meta_ref[...] = meta


def _router(h_a, h_b, g, w_router, b_router):
    n_a, d = h_a.shape
    n_b = h_b.shape[0]
    n = n_a + n_b
    tm = min(ROUTER_TM, n_a, n_b)
    steps_a = n_a // tm
    return pl.pallas_call(
        functools.partial(_router_kernel, steps_a=steps_a),
        grid=(n // tm,),
        in_specs=[
            pl.BlockSpec((tm, d), lambda i: (jnp.minimum(i, steps_a - 1), 0)),
            pl.BlockSpec((tm, d), lambda i: (jnp.maximum(i - steps_a, 0), 0)),
            pl.BlockSpec((1, d), lambda i: (0, 0)),
            pl.BlockSpec((d, LANES), lambda i: (0, 0)),
            pl.BlockSpec((1, LANES), lambda i: (0, 0)),
        ],
        out_specs=[
            pl.BlockSpec((tm, d), lambda i: (i, 0)),
            pl.BlockSpec((tm, LANES), lambda i: (i, 0)),
            pl.BlockSpec((1, LANES), lambda i: (0, 0)),
        ],
        out_shape=[
            jax.ShapeDtypeStruct((n, d), jnp.float32),
            jax.ShapeDtypeStruct((n, LANES), jnp.float32),
            jax.ShapeDtypeStruct((1, LANES), jnp.float32),
        ],
        compiler_params=_cparams("arbitrary"),
    )(h_a, h_b, g.reshape(1, d), w_router, b_router)


def _experts_kernel(pos_ref, start_ref, count_ref, rows_ref, nused_ref, x_hbm, wg_ref, wu_ref, wd_ref, y_hbm,
                    tok_ref, xbuf, obuf, gsem, osem, *, bm, n_blocks):
    e = pl.program_id(0)
    n_used = nused_ref[0]
    b0 = start_ref[e]

    def start_gather(blk, slot):
        base = blk * bm
        for r in range(bm):
            tok = tok_ref[base + r]
            pltpu.make_async_copy(x_hbm.at[pl.ds(tok, 1), :], xbuf.at[slot, pl.ds(r, 1), :],
                                  gsem.at[slot]).start()

    def wait_gather(slot):
        pltpu.make_async_copy(x_hbm.at[pl.ds(0, bm), :], xbuf.at[slot], gsem.at[slot]).wait()

    def out_copy(blk, slot):
        return pltpu.make_async_copy(obuf.at[slot], y_hbm.at[pl.ds(blk * bm, bm), :], osem.at[slot])

    @pl.when(e == 0)
    def _():
        n_assign = pos_ref.shape[0]

        def clear_padding(ex, _):
            def clear(i, _):
                tok_ref[i] = 0
                return 0
            row0 = start_ref[ex] * bm
            lax.fori_loop(row0 + rows_ref[ex], row0 + count_ref[ex] * bm, clear, 0)
            return 0

        def invert(i8, _):
            for k in range(SUBLANES):
                a = i8 * SUBLANES + k
                tok_ref[pos_ref[a]] = lax.shift_right_logical(a, TOP_K_SHIFT)
            return 0

        lax.fori_loop(0, pl.num_programs(0), clear_padding, 0)
        lax.fori_loop(0, n_assign // SUBLANES, invert, 0)
        start_gather(0, 0)

        @pl.when(n_used > 1)
        def _():
            start_gather(1, 1)

    def block_body(j, _):
        b = b0 + j
        slot = lax.bitwise_and(b, 1)
        gslot = lax.rem(b, GATHER_SLOTS)

        @pl.when(b + GATHER_SLOTS - 1 < n_used)
        def _():
            start_gather(b + GATHER_SLOTS - 1, lax.rem(b + GATHER_SLOTS - 1, GATHER_SLOTS))

        wait_gather(gslot)

        @pl.when(b >= 2)
        def _():
            out_copy(b - 2, slot).wait()

        x = xbuf[gslot].astype(MXU_DTYPE)
        hg = jnp.dot(x, wg_ref[...].astype(MXU_DTYPE), preferred_element_type=jnp.float32)
        hu = jnp.dot(x, wu_ref[...].astype(MXU_DTYPE), preferred_element_type=jnp.float32)
        hid = (jax.nn.silu(hg) * hu).astype(MXU_DTYPE)
        obuf[slot] = jnp.dot(hid, wd_ref[...].astype(MXU_DTYPE), preferred_element_type=jnp.float32)
        out_copy(b, slot).start()
        return 0

    lax.fori_loop(0, count_ref[e], block_body, 0)

    @pl.when(e == pl.num_programs(0) - 1)
    def _():
        @pl.when(n_used >= 2)
        def _():
            out_copy(n_used - 2, n_used % 2).wait()

        out_copy(n_used - 1, (n_used - 1) % 2).wait()
        obuf[0] = jnp.zeros(obuf.shape[1:], jnp.float32)

        def zero_fill(b, _):
            out_copy(b, 0).start()
            out_copy(b, 0).wait()
            return 0

        lax.fori_loop(n_used, n_blocks, zero_fill, 0)


def _experts(xn2, pos, blk_start, blk_count, exp_rows, n_used, w_gate, w_up, w_down, *, n_blocks):
    n, d = xn2.shape
    n_experts, _, f = w_gate.shape
    bm = MOE_BM
    grid_spec = pltpu.PrefetchScalarGridSpec(
        num_scalar_prefetch=5,
        grid=(n_experts,),
        in_specs=[
            pl.BlockSpec(memory_space=pl.ANY),
            pl.BlockSpec((None, d, f), lambda e, *_: (e, 0, 0)),
            pl.BlockSpec((None, d, f), lambda e, *_: (e, 0, 0)),
            pl.BlockSpec((None, f, d), lambda e, *_: (e, 0, 0)),
        ],
        out_specs=pl.BlockSpec(memory_space=pl.ANY),
        scratch_shapes=[
            pltpu.SMEM((n_blocks * bm,), jnp.int32),
            pltpu.VMEM((GATHER_SLOTS, bm, d), jnp.float32),
            pltpu.VMEM((2, bm, d), jnp.float32),
            pltpu.SemaphoreType.DMA((GATHER_SLOTS,)),
            pltpu.SemaphoreType.DMA((2,)),
        ],
    )
    return pl.pallas_call(
        functools.partial(_experts_kernel, bm=bm, n_blocks=n_blocks),
        grid_spec=grid_spec,
        out_shape=jax.ShapeDtypeStruct((n_blocks * bm, d), jnp.float32),
        compiler_params=_gather_cparams(EXPERTS_VMEM_LIMIT_BYTES),
    )(pos, blk_start, blk_count, exp_rows, n_used, xn2, w_gate, w_up, w_down)


def _combine_kernel(pos_ref, y_hbm, h_ref, meta_ref, g_ref, o_ref, ybuf, sem, *, tm, n_steps):
    step = pl.program_id(0)

    def start_gather(blk, slot):
        base = blk * tm * TOP_K
        for r in range(tm):
            for k in range(TOP_K):
                pos = pos_ref[base + r * TOP_K + k]
                pltpu.make_async_copy(y_hbm.at[pl.ds(pos, 1), :], ybuf.at[slot, k, pl.ds(r, 1), :],
                                      sem.at[slot]).start()

    def wait_gather(slot):
        for k in range(TOP_K):
            pltpu.make_async_copy(y_hbm.at[pl.ds(0, tm), :], ybuf.at[slot, k], sem.at[slot]).wait()

    slot = step % 2

    @pl.when(step == 0)
    def _():
        start_gather(0, 0)

    @pl.when(step + 1 < n_steps)
    def _():
        start_gather(step + 1, 1 - slot)

    wait_gather(slot)
    w1 =meta_ref[:, META_W:META_W + 1]
    w2 = meta_ref[:, META_W + 1:META_W + 2]
    h = h_ref[...] + (ybuf[slot, 0] * w1 + ybuf[slot, 1] * w2)
    ms = jnp.mean(h * h, axis=-1, keepdims=True)
    o_ref[...] = (h * lax.rsqrt(ms + EPS)) * g_ref[...]


def _combine(pos, yb, h2, meta, g):
    n, d = h2.shape
    tm = min(COMBINE_TM, n)
    n_steps = n // tm
    grid_spec = pltpu.PrefetchScalarGridSpec(
        num_scalar_prefetch=1,
        grid=(n_steps,),
        in_specs=[
            pl.BlockSpec(memory_space=pl.ANY),
            pl.BlockSpec((tm, d), lambda i, pos: (i, 0)),
            pl.BlockSpec((tm, LANES), lambda i, pos: (i, 0)),
            pl.BlockSpec((1, d), lambda i, pos: (0, 0)),
        ],
        out_specs=pl.BlockSpec((tm, d), lambda i, pos: (i, 0)),
        scratch_shapes=[
            pltpu.VMEM((2, TOP_K, tm, d), jnp.float32),
            pltpu.SemaphoreType.DMA((2,)),
        ],
    )
    return pl.pallas_call(
        functools.partial(_combine_kernel, tm=tm, n_steps=n_steps),
        grid_spec=grid_spec,
        out_shape=jax.ShapeDtypeStruct((n, d), jnp.float32),
        compiler_params=_gather_cparams(VMEM_LIMIT_BYTES),
    )(pos, yb, h2, meta, g.reshape(1, d))


def _dispatch_plan(meta, counts_f, n):
    bm = MOE_BM
    a = n * TOP_K
    expert = meta[:, META_E:META_E + TOP_K].astype(jnp.int32)
    rank = meta[:, META_RANK:META_RANK + TOP_K].astype(jnp.int32)
    counts = counts_f[0, :N_EXPERTS].astype(jnp.int32)
    padded = (counts + bm - 1) // bm * bm
    pad_end = jnp.cumsum(padded)
    pad_start = pad_end - padded
    pos = (pad_start[expert] + rank).reshape(a)
    n_blocks = -(-a // bm) + N_EXPERTS
    blk_start = (pad_start // bm).astype(jnp.int32)
    blk_count = (padded // bm).astype(jnp.int32)
    n_used = (pad_end[-1] // bm).astype(jnp.int32).reshape(1)
    return pos, blk_start, blk_count, counts, n_used, n_blocks


def _mixer_residual(x, p):
    b, s, d = x.shape
    n = b * s
    d_rnn = p["d_rnn"]
    d_attn = p["d_attn"]
    x2 = x.reshape(n, d)
    proj2 = _norm_proj(x2, p["norm_mix_g"], p["w_in"])
    d_in = proj2.shape[1]
    proj3 = proj2.reshape(b, s, d_in)
    hf = _lru_direction(proj3, p["conv_w"], p["conv_b"], p["lru_wg"][0], p["lru_b_a"][0], p["lru_b_x"][0],
                        p["lru_lambda"][0], None, reverse=False, d_rnn=d_rnn)
    ya = _lru_direction(proj3, p["conv_w"], p["conv_b"], p["lru_wg"][1], p["lru_b_a"][1], p["lru_b_x"][1],
                        p["lru_lambda"][1], hf, reverse=True, d_rnn=d_rnn)
    att = _attention(proj3, p["att_table"], d_rnn=d_rnn, d_attn=d_attn)
    m2 = _merge(ya.reshape(n, d_rnn), att.reshape(n, d_attn), p["w_lru_up"], p["w_attn_up"], proj2,
                p["b_merge"], gate_col=2 * d_rnn + 3 * d_attn)
    return _out_proj(m2, p["w_o"], x2)


def _moe_and_final_norm(h_a, h_b, p):
    n_a, n_b = h_a.shape[0], h_b.shape[0]
    xn2, meta, counts = _router(h_a, h_b, p["norm_ffn_g"], p["w_router"], p["b_router"])
    pos, blk_start, blk_count, exp_rows, n_used, n_blocks = _dispatch_plan(meta, counts, n_a + n_b)
    yb = _experts(xn2, pos, blk_start, blk_count, exp_rows, n_used, p["w_exp_gate"], p["w_exp_up"], p["w_exp_down"],
                  n_blocks=n_blocks)
    y_a = _combine(pos[:n_a * TOP_K], yb, h_a, meta[:n_a], p["norm_final_g"])
    y_b = _combine(pos[n_a * TOP_K:], yb, h_b, meta[n_a:], p["norm_final_g"])
    return y_a, y_b


def _prepare_params(norm_mix_g, w_in, conv_w, conv_b, lru_w_a, lru_b_a, lru_w_x, lru_b_x, lru_lambda, rpb,
                    w_lru_up, w_attn_up, b_merge, w_o, norm_ffn_g, w_router_group, b_router_group,
                    w_router_expert, b_router_expert, w_exp_gate, w_exp_up, w_exp_down, norm_final_g):
    bf16 = MXU_DTYPE
    d = w_in.shape[1]
    d_rnn = conv_w.shape[2]
    d_attn = w_attn_up.shape[1]
    lru_wg = jnp.concatenate([lru_w_a[0], lru_w_x[0]], axis=-1).astype(bf16)
    w_router = jnp.zeros((d, LANES), jnp.float32)
    w_router = w_router.at[:, :N_EXPERTS].set(w_router_expert[0])
    w_router = w_router.at[:, GROUP_LANE0:GROUP_LANE0 + N_GROUPS].set(w_router_group[0])
    b_router = jnp.zeros((1, LANES), jnp.float32)
    b_router = b_router.at[0, :N_EXPERTS].set(b_router_expert[0])
    b_router = b_router.at[0, GROUP_LANE0:GROUP_LANE0 + N_GROUPS].set(b_router_group[0])
    return dict(
        d_rnn=d_rnn, d_attn=d_attn,
        norm_mix_g=norm_mix_g[0], w_in=w_in[0].astype(bf16),
        conv_w=conv_w[0], conv_b=conv_b[0], lru_wg=lru_wg,
        lru_b_a=lru_b_a[0], lru_b_x=lru_b_x[0], lru_lambda=lru_lambda[0],
        att_table=_attention_bias_table(rpb[0]),
        w_lru_up=w_lru_up[0].astype(bf16), w_attn_up=w_attn_up[0].astype(bf16),
        b_merge=b_merge[0], w_o=w_o[0].astype(bf16),
        norm_ffn_g=norm_ffn_g[0], w_router=w_router, b_router=b_router,
        w_exp_gate=w_exp_gate[0], w_exp_up=w_exp_up[0], w_exp_down=w_exp_down[0],
        norm_final_g=norm_final_g,
    )


def kernel(x_prompt, x_sample, norm_mix_g, w_in, conv_w, conv_b, lru_w_a, lru_b_a, lru_w_x, lru_b_x, lru_lambda, rpb, w_lru_up, w_attn_up, b_merge, w_o, norm_ffn_g, w_router_group, b_router_group, w_router_expert, b_router_expert, w_exp_gate, w_exp_up, w_exp_down, norm_final_g):
    assert norm_mix_g.shape[0] == 1, "single-layer encoder"
    p = _prepare_params(norm_mix_g, w_in, conv_w, conv_b, lru_w_a, lru_b_a, lru_w_x, lru_b_x, lru_lambda, rpb,
                        w_lru_up, w_attn_up, b_merge, w_o, norm_ffn_g, w_router_group, b_router_group,
                        w_router_expert, b_router_expert, w_exp_gate, w_exp_up, w_exp_down, norm_final_g)
    h_prompt = _mixer_residual(x_prompt, p)
    h_sample = _mixer_residual(x_sample, p)
    y_prompt, y_sample = _moe_and_final_norm(h_prompt, h_sample, p)
    return (y_prompt.reshape(x_prompt.shape), y_sample.reshape(x_sample.shape))
```

```python
import functools

import numpy as np
import jax
import jax.numpy as jnp
from jax import lax
from jax.experimental import pallas as pl
from jax.experimental.pallas import tpu as pltpu

GRID_W = 64
WIN_R = 8
WIN_C = 16
CONV_W = 4
LRU_C = 8.0
LRU_BLOCK = 128
HEAD_DIM = 128
N_GROUPS = 4
EXPERTS_PER_GROUP = 8
N_EXPERTS = N_GROUPS * EXPERTS_PER_GROUP
TOP_K = 2
EPS = 1e-6
MASK_VALUE = -1e30

MXU_DTYPE = jnp.bfloat16

LANES = 128
SUBLANES = 8
VMEM_LIMIT_BYTES = 56 * 1024 * 1024
EXPERTS_VMEM_LIMIT_BYTES = 62 * 1024 * 1024

PROJ_TM = 1024
PROJ_TN = 1024
MERGE_TM = 256
MERGE_TN = 2048
OUT_PROJ_TM = 512
OUT_PROJ_TN = 2048
LRU_TC = 4096
LRU_GROUPS = 4
LRU_UNROLL = 4
ATT_ROWS = 32
ROUTER_TM = 512
MOE_BM = 256
COMBINE_TM = 256
GATHER_SLOTS = 3


def _cparams(*sem):
    return pltpu.CompilerParams(dimension_semantics=sem, vmem_limit_bytes=VMEM_LIMIT_BYTES)


def _gather_cparams(vmem_limit):
    return pltpu.CompilerParams(dimension_semantics=("arbitrary",), vmem_limit_bytes=vmem_limit)


def _norm_proj_kernel(x_ref, g_ref, w_ref, o_ref, xn_ref):
    @pl.when(pl.program_id(1) == 0)
    def _():
        x = x_ref[...]
        ms = jnp.mean(x * x, axis=-1, keepdims=True)
        xn_ref[...] = ((x * lax.rsqrt(ms + EPS)) * g_ref[...]).astype(MXU_DTYPE)

    o_ref[...] = jnp.dot(xn_ref[...], w_ref[...], preferred_element_type=jnp.float32)


def _norm_proj(x2, g, w_bf16):
    n, d = x2.shape
    d_out = w_bf16.shape[1]
    tm = min(PROJ_TM, n)
    tn = min(PROJ_TN, d_out)
    return pl.pallas_call(
        _norm_proj_kernel,
        grid=(n // tm, d_out // tn),
        in_specs=[
            pl.BlockSpec((tm, d), lambda i, j: (i, 0)),
            pl.BlockSpec((1, d), lambda i, j: (0, 0)),
            pl.BlockSpec((d, tn), lambda i, j: (0, j)),
        ],
        out_specs=pl.BlockSpec((tm, tn), lambda i, j: (i, j)),
        out_shape=jax.ShapeDtypeStruct((n, d_out), jnp.float32),
        scratch_shapes=[pltpu.VMEM((tm, d), MXU_DTYPE)],
        compiler_params=_cparams("parallel", "arbitrary"),
    )(x2, g.reshape(1, d), w_bf16)


def _softplus(y):
    return jnp.maximum(y, 0.0) + jnp.log1p(jnp.exp(-jnp.abs(y)))


def _lru_pitch(tc):
    seg_len = tc // (LRU_GROUPS * SUBLANES)
    assert seg_len * LRU_GROUPS * SUBLANES == tc and seg_len % SUBLANES == 0
    return seg_len + SUBLANES // 2


def _lru_kernel(*refs, reverse, n_chunks, tc):
    if reverse:
        (x_ref, xp_ref, xn_ref, cw_ref, cb_ref, wg_ref, ba_ref, bx_ref, lam_ref,
         hf_ref, gate_ref, o_ref, xpad_ref, a_ref, u_ref, h_ref, carry_ref) = refs
    else:
        (x_ref, xp_ref, xn_ref, cw_ref, cb_ref, wg_ref, ba_ref, bx_ref, lam_ref,
         o_ref, xpad_ref, a_ref, u_ref, h_ref, carry_ref) = refs
    t = pl.program_id(2)
    chunk = (n_chunks - 1 - t) if reverse else t

    @pl.when(t == 0)
    def _():
        carry_ref[...] = jnp.zeros_like(carry_ref)

    prev = jnp.where(chunk == 0, 0.0, xp_ref[...])
    nxt = jnp.where(chunk == n_chunks - 1, 0.0, xn_ref[...])
    xpad_ref[pl.ds(0, SUBLANES), :] = prev
    xpad_ref[pl.ds(SUBLANES, tc), :] = x_ref[...]
    xpad_ref[pl.ds(SUBLANES + tc, SUBLANES), :] = nxt
    left = CONV_W // 2
    xc = cb_ref[...]
    for tap in range(CONV_W):
        xc = xc + xpad_ref[pl.ds(SUBLANES - left + tap, tc), :] * cw_ref[pl.ds(tap, 1), :]

    z = jnp.dot(xc.astype(MXU_DTYPE), wg_ref[...], preferred_element_type=jnp.float32)
    r = jax.nn.sigmoid(z[:, :LRU_BLOCK] + ba_ref[...])
    i = jax.nn.sigmoid(z[:, LRU_BLOCK:] + bx_ref[...])
    log_a = (-LRU_C * r) * _softplus(-lam_ref[...])
    a = jnp.exp(log_a)
    u = jnp.sqrt(jnp.tanh(-log_a) * (1.0 + a * a)) * (i * xc)

    pitch = _lru_pitch(tc)
    slots = LRU_GROUPS * SUBLANES * pitch
    a_ref[pl.ds(0, tc), :] = a
    u_ref[pl.ds(0, tc), :] = u
    a_ref[pl.ds(tc, slots - tc), :] = jnp.ones((slots - tc, LRU_BLOCK), jnp.float32)
    u_ref[pl.ds(tc, slots - tc), :] = jnp.zeros((slots - tc, LRU_BLOCK), jnp.float32)

    def seg_rows(g, k):
        return pl.ds(g * SUBLANES * pitch + k, SUBLANES, stride=pitch)

    def slot_of(kk):
        return (pitch - 1 - kk) if reverse else kk

    def scan_step(kk, carry):
        hs, ps = carry
        k = slot_of(kk)
        new_h, new_p = [], []
        for g in range(LRU_GROUPS):
            a_k = a_ref[seg_rows(g, k), :]
            h = a_k * hs[g] + u_ref[seg_rows(g, k), :]
            h_ref[seg_rows(g, k), :] = h
            new_h.append(h)
            new_p.append(a_k * ps[g])
        return tuple(new_h), tuple(new_p)

    zeros = tuple(jnp.zeros((SUBLANES, LANES), jnp.float32) for _ in range(LRU_GROUPS))
    ones = tuple(jnp.ones((SUBLANES, LANES), jnp.float32) for _ in range(LRU_GROUPS))
    h_end, p_end = lax.fori_loop(0, pitch, scan_step, (zeros, ones), unroll=LRU_UNROLL)

    row = lax.broadcasted_iota(jnp.int32, (SUBLANES, LANES), 0)
    c_in = [None] * LRU_GROUPS
    cg = carry_ref[...]
    for g in (reversed(range(LRU_GROUPS)) if reverse else range(LRU_GROUPS)):
        pa, hb = p_end[g], h_end[g]
        for s in (1, 2, 4):
            keep = (row < SUBLANES - s) if reverse else (row >= s)
            shift = (SUBLANES - s) if reverse else s
            a_sh = jnp.where(keep, pltpu.roll(pa, shift, 0), 1.0)
            b_sh = jnp.where(keep, pltpu.roll(hb, shift, 0), 0.0)
            hb = pa * b_sh + hb
            pa = pa * a_sh
        seg_out = pa * cg + hb
        if reverse:
            c_in[g] = jnp.where(row == SUBLANES - 1, cg, pltpu.roll(seg_out, SUBLANES - 1, 0))
            cg = seg_out[0:1, :]
        else:
            c_in[g] = jnp.where(row == 0, cg, pltpu.roll(seg_out, 1, 0))
            cg = seg_out[SUBLANES - 1:SUBLANES, :]
    carry_ref[...] = cg

    def fix_step(kk, ps):
        k = slot_of(kk)
        new_p = []
        for g in range(LRU_GROUPS):
            p = a_ref[seg_rows(g, k), :] * ps[g]
            u_ref[seg_rows(g, k), :] = h_ref[seg_rows(g, k), :] + p * c_in[g]
            new_p.append(p)
        return tuple(new_p)

    lax.fori_loop(0, pitch, fix_step, ones, unroll=LRU_UNROLL)

    h_all = u_ref[pl.ds(0, tc), :]
    if reverse:
        o_ref[...] = (jax.nn.gelu(gate_ref[...]) * (hf_ref[...] + h_all)).astype(o_ref.dtype)
    else:
        o_ref[...] = h_all


def _lru_direction(proj3, conv_w, conv_b, wg, ba, bx, lam, hf, *, reverse, d_rnn):
    b, s, _ = proj3.shape
    n_cb = d_rnn // LRU_BLOCK
    tc = min(LRU_TC, s)
    n_chunks = s // tc
    tc8 = tc // SUBLANES
    s8 = s // SUBLANES
    slots = LRU_GROUPS * SUBLANES * _lru_pitch(tc)

    def chunk_of(t):
        return (n_chunks - 1 - t) if reverse else t

    vec_spec = pl.BlockSpec((1, LRU_BLOCK), lambda bi, c, t: (0, c))
    in_specs = [
        pl.BlockSpec((None, tc, LRU_BLOCK), lambda bi, c, t: (bi, chunk_of(t), c)),
        pl.BlockSpec((None, SUBLANES, LRU_BLOCK),
                     lambda bi, c, t: (bi, jnp.maximum(chunk_of(t) * tc8 - 1, 0), c)),
        pl.BlockSpec((None, SUBLANES, LRU_BLOCK),
                     lambda bi, c, t: (bi, jnp.minimum((chunk_of(t) + 1) * tc8, s8 - 1), c)),
        pl.BlockSpec((CONV_W, LRU_BLOCK), lambda bi, c, t: (0, c)),
        vec_spec,
        pl.BlockSpec((None, LRU_BLOCK, 2 * LRU_BLOCK), lambda bi, c, t: (c, 0, 0)),
        vec_spec, vec_spec, vec_spec,
    ]
    args = [proj3, proj3, proj3, conv_w, conv_b.reshape(1, d_rnn), wg,
            ba.reshape(1, d_rnn), bx.reshape(1, d_rnn), lam.reshape(1, d_rnn)]
    if reverse:
        in_specs += [
            pl.BlockSpec((None, tc, LRU_BLOCK), lambda bi, c, t: (bi, chunk_of(t), c)),
            pl.BlockSpec((None, tc, LRU_BLOCK), lambda bi, c, t: (bi, chunk_of(t), n_cb + c)),
        ]
        args += [hf, proj3]
        out_dtype = MXU_DTYPE
    else:
        out_dtype = jnp.float32
    return pl.pallas_call(
        functools.partial(_lru_kernel, reverse=reverse, n_chunks=n_chunks, tc=tc),
        grid=(b, n_cb, n_chunks),
        in_specs=in_specs,
        out_specs=pl.BlockSpec((None, tc, LRU_BLOCK), lambda bi, c, t: (bi, chunk_of(t), c)),
        out_shape=jax.ShapeDtypeStruct((b, s, d_rnn), out_dtype),
        scratch_shapes=[
            pltpu.VMEM((tc + 2 * SUBLANES, LRU_BLOCK), jnp.float32),
            pltpu.VMEM((slots, LRU_BLOCK), jnp.float32),
            pltpu.VMEM((slots, LRU_BLOCK), jnp.float32),
            pltpu.VMEM((slots, LRU_BLOCK), jnp.float32),
            pltpu.VMEM((1, LRU_BLOCK), jnp.float32),
        ],
        compiler_params=_cparams("parallel", "parallel", "arbitrary"),
    )(*args)


def _attention_bias_table(rpb):
    n_heads, n_dr, n_dc = rpb.shape
    cq = np.arange(GRID_W)
    cs = np.clip(cq - WIN_C // 2, 0, GRID_W - WIN_C)
    ck = np.arange(GRID_W)
    valid = (ck[None, :] >= cs[:, None]) & (ck[None, :] < cs[:, None] + WIN_C)
    pad = GRID_W - WIN_C
    width = n_dc + 2 * pad + 1
    padded = jnp.pad(rpb.astype(jnp.float32), ((0, 0), (0, 0), (pad, pad + 1)))
    tiled = jnp.tile(padded, (1, 1, GRID_W))[:, :, :GRID_W * (width - 1)]
    skew = tiled.reshape(n_heads, n_dr, GRID_W, width - 1)
    toep = skew[:, :, :, GRID_W - 1:2 * GRID_W - 1]
    toep = jnp.where(valid[None, None], toep, MASK_VALUE)
    per_off = [toep[:, WIN_R - 1 - off:2 * WIN_R - 1 - off] for off in range(WIN_R)]
    tab = jnp.stack(per_off, axis=1)
    tab = tab.transpose(0, 1, 3, 2, 4)
    return tab.reshape(n_heads, WIN_R, GRID_W, WIN_R * GRID_W)


def _attention_kernel(q_ref, k_ref, v_ref, tb_ref, o_ref, kb_ref, vb_ref, *, rows, tile_rows):
    kb_ref[...] = k_ref[...].astype(MXU_DTYPE)
    vb_ref[...] = v_ref[...].astype(MXU_DTYPE)
    scale = HEAD_DIM ** -0.5
    band = WIN_R * GRID_W

    def tile_body(i, _):
        starts, scores = [], []
        for rl in range(tile_rows):
            r = i * tile_rows + rl
            rs = jnp.clip(r - WIN_R // 2, 0, rows - WIN_R)
            start = pl.multiple_of(rs * GRID_W, GRID_W)
            q0 = pl.multiple_of(r * GRID_W, GRID_W)
            q = (q_ref[pl.ds(q0, GRID_W), :] * scale).astype(MXU_DTYPE)
            kband = kb_ref[pl.ds(start, band), :]
            s = lax.dot_general(q, kband, (((1,), (1,)), ((), ())), preferred_element_type=jnp.float32)
            starts.append(start)
            scores.append(s + tb_ref[r - rs])
        probs, denoms = [], []
        for s in scores:
            m = jnp.max(s, axis=-1, keepdims=True)
            p = jnp.exp(s - m)
            denoms.append(jnp.sum(p, axis=-1, keepdims=True))
            probs.append(p.astype(MXU_DTYPE))
        for rl in range(tile_rows):
            vband = vb_ref[pl.ds(starts[rl], band), :]
            o = jnp.dot(probs[rl], vband, preferred_element_type=jnp.float32)
            q0 = pl.multiple_of((i * tile_rows + rl) * GRID_W, GRID_W)
            o_ref[pl.ds(q0, GRID_W), :] = (o / denoms[rl]).astype(o_ref.dtype)
        return 0

    lax.fori_loop(0, rows // tile_rows, tile_body, 0)


def _attention(proj3, table, *, d_rnn, d_attn):
    b, s, _ = proj3.shape
    rows = s // GRID_W
    n_heads = d_attn // HEAD_DIM
    q_blk = 2 * d_rnn // HEAD_DIM
    k_blk = q_blk + n_heads
    v_blk = k_blk + n_heads
    tile_rows = min(ATT_ROWS, rows)
    assert rows % tile_rows == 0 and rows >= WIN_R
    return pl.pallas_call(
        functools.partial(_attention_kernel, rows=rows, tile_rows=tile_rows),
        grid=(b, n_heads),
        in_specs=[
            pl.BlockSpec((None, s, HEAD_DIM), lambda bi, h: (bi, 0, q_blk + h)),
            pl.BlockSpec((None, s, HEAD_DIM), lambda bi, h: (bi, 0, k_blk + h)),
            pl.BlockSpec((None, s, HEAD_DIM), lambda bi, h: (bi, 0, v_blk + h)),
            pl.BlockSpec((None, WIN_R, GRID_W, WIN_R * GRID_W), lambda bi, h: (h, 0, 0, 0)),
        ],
        out_specs=pl.BlockSpec((None, s, HEAD_DIM), lambda bi, h: (bi, 0, h)),
        out_shape=jax.ShapeDtypeStruct((b, s, d_attn), MXU_DTYPE),
        scratch_shapes=[
            pltpu.VMEM((s, HEAD_DIM), MXU_DTYPE),
            pltpu.VMEM((s, HEAD_DIM), MXU_DTYPE),
        ],
        compiler_params=_cparams("parallel", "parallel"),
    )(proj3, proj3, proj3, table)


def _weight_spec(d, tn):
    if tn == d:
        return pl.BlockSpec((d, tn), lambda i, j: (0, j), pipeline_mode=pl.Buffered(1))
    return pl.BlockSpec((d, tn), lambda i, j: (0, j))


def _merge_kernel(ya_ref, att_ref, wl_ref, wa_ref, gl_ref, ga_ref, bm_ref, o_ref):
    y_lru = jnp.dot(ya_ref[...], wl_ref[...], preferred_element_type=jnp.float32)
    y_att = jnp.dot(att_ref[...], wa_ref[...], preferred_element_type=jnp.float32)
    s_lru = jax.nn.sigmoid(gl_ref[...] + bm_ref[pl.ds(0, 1), :])
    s_att = jax.nn.sigmoid(ga_ref[...] + bm_ref[pl.ds(1, 1), :])
    o_ref[...] = (s_lru * y_lru + s_att * y_att).astype(o_ref.dtype)


def _merge(ya2, att2, wl, wa, proj2, b_merge, *, gate_col):
    n, d = ya2.shape
    tm = min(MERGE_TM, n)
    tn = min(MERGE_TN, d)
    gl_blk = gate_col // tn
    ga_blk = (gate_col + d) // tn
    return pl.pallas_call(
        _merge_kernel,
        grid=(n // tm, d // tn),
        in_specs=[
            pl.BlockSpec((tm, d), lambda i, j: (i, 0)),
            pl.BlockSpec((tm, d), lambda i, j: (i, 0)),
            _weight_spec(d, tn),
            _weight_spec(d, tn),
            pl.BlockSpec((tm, tn), lambda i, j: (i, gl_blk + j)),
            pl.BlockSpec((tm, tn), lambda i, j: (i, ga_blk + j)),
            pl.BlockSpec((2, tn), lambda i, j: (0, j)),
        ],
        out_specs=pl.BlockSpec((tm, tn), lambda i, j: (i, j)),
        out_shape=jax.ShapeDtypeStruct((n, d), MXU_DTYPE),
        compiler_params=_cparams("parallel", "arbitrary"),
    )(ya2, att2, wl, wa, proj2, proj2, b_merge)


def _out_proj_kernel(m_ref, w_ref, x_ref, o_ref):
    o_ref[...] = x_ref[...] + jnp.dot(m_ref[...], w_ref[...], preferred_element_type=jnp.float32)


def _out_proj(m2, w_o, x2):
    n, d = m2.shape
    tm = min(OUT_PROJ_TM, n)
    tn = min(OUT_PROJ_TN, d)
    return pl.pallas_call(
        _out_proj_kernel,
        grid=(n // tm, d // tn),
        in_specs=[
            pl.BlockSpec((tm, d), lambda i, j: (i, 0)),
            _weight_spec(d, tn),
            pl.BlockSpec((tm, tn), lambda i, j: (i, j)),
        ],
        out_specs=pl.BlockSpec((tm, tn), lambda i, j: (i, j)),
        out_shape=jax.ShapeDtypeStruct((n, d), jnp.float32),
        compiler_params=_cparams("parallel", "arbitrary"),
    )(m2, w_o, x2)


GROUP_LANE0 = N_EXPERTS
META_E, META_W, META_RANK = 0, 2, 4
GROUP_SHIFT = EXPERTS_PER_GROUP.bit_length() - 1
TOP_K_SHIFT = TOP_K.bit_length() - 1
assert 1 << GROUP_SHIFT == EXPERTS_PER_GROUP and 1 << TOP_K_SHIFT == TOP_K


def _router_kernel(ha_ref, hb_ref, g_ref, whi_ref, wlo_ref, b_ref, xn_ref, meta_ref, cnt_ref, *, steps_a):
    step = pl.program_id(0)

    @pl.when(step == 0)
    def _():
        cnt_ref[...] = jnp.zeros_like(cnt_ref)

    x = jnp.where(step < steps_a, ha_ref[...], hb_ref[...])
    tm = x.shape[0]
    ms = jnp.mean(x * x, axis=-1, keepdims=True)
    xn = (x * lax.rsqrt(ms + EPS)) * g_ref[...]
    xn_ref[...] = xn
    x_hi = xn.astype(MXU_DTYPE)
    x_lo = (xn - x_hi.astype(jnp.float32)).astype(MXU_DTYPE)
    logits = jnp.dot(x_hi, whi_ref[...], preferred_element_type=jnp.float32)
    logits = logits + (jnp.dot(x_hi, wlo_ref[...], preferred_element_type=jnp.float32)
                       + jnp.dot(x_lo, whi_ref[...], preferred_element_type=jnp.float32))
    logits = logits + b_ref[...]
    lane = lax.broadcasted_iota(jnp.int32, (tm, LANES), 1)
    neg_inf = -jnp.inf

    def first_argmax(vals, vmax):
        return jnp.min(jnp.where(vals == vmax, lane, LANES), axis=-1, keepdims=True)

    is_group = (lane >= GROUP_LANE0) & (lane < GROUP_LANE0 + N_GROUPS)
    lg = jnp.where(is_group, logits, neg_inf)
    mg = jnp.max(lg, axis=-1, keepdims=True)
    g_idx = first_argmax(lg, mg) - GROUP_LANE0
    g_val = 1.0 / jnp.sum(jnp.exp(lg - mg), axis=-1, keepdims=True)

    in_group = (lane < N_EXPERTS) & (lax.shift_right_logical(lane, GROUP_SHIFT) == g_idx)
    le = jnp.where(in_group, logits, neg_inf)
    m1 = jnp.max(le, axis=-1, keepdims=True)
    e1 = first_argmax(le, m1)
    le2 = jnp.where(lane == e1, neg_inf, le)
    m2 = jnp.max(le2, axis=-1, keepdims=True)
    e2 = first_argmax(le2, m2)
    z = jnp.sum(jnp.exp(le - m1), axis=-1, keepdims=True)
    p1 = 1.0 / z
    p2 = jnp.exp(m2 - m1) / z
    den = p1 + p2
    w1 = g_val * (p1 / den)
    w2 = g_val * (p2 / den)

    hot1 = lane == e1
    hot2 = lane == e2
    hot = hot1.astype(jnp.float32) + hot2.astype(jnp.float32)
    r_i = lax.broadcasted_iota(jnp.int32, (tm, tm), 0)
    c_i = lax.broadcasted_iota(jnp.int32, (tm, tm), 1)
    lower = (c_i < r_i).astype(MXU_DTYPE)
    before = jnp.dot(lower, hot.astype(MXU_DTYPE), preferred_element_type=jnp.float32) + cnt_ref[...]
    rank1 = jnp.sum(jnp.where(hot1, before, 0.0), axis=-1, keepdims=True)
    rank2 = jnp.sum(jnp.where(hot2, before, 0.0), axis=-1, keepdims=True)
    cnt_ref[...] += jnp.sum(hot, axis=0, keepdims=True)

    meta = jnp.where(lane == META_E, e1.astype(jnp.float32), 0.0)
    meta = jnp.where(lane == META_E + 1, e2.astype(jnp.float32), meta)
    meta = jnp.where(lane == META_W, w1, meta)
    meta = jnp.where(lane == META_W + 1, w2, meta)
    meta = jnp.where(lane == META_RANK, rank1, meta)
    meta = jnp.where(lane == META_RANK + 1, rank2, meta)
    meta_ref[...] = meta


def _router(h_a, h_b, g, w_hi, w_lo, b_router):
    n_a, d = h_a.shape
    n_b = h_b.shape[0]
    n = n_a + n_b
    tm = min(ROUTER_TM, n_a, n_b)
    steps_a = n_a // tm
    return pl.pallas_call(
        functools.partial(_router_kernel, steps_a=steps_a),
        grid=(n // tm,),
        in_specs=[
            pl.BlockSpec((tm, d), lambda i: (jnp.minimum(i, steps_a - 1), 0)),
            pl.BlockSpec((tm, d), lambda i: (jnp.maximum(i - steps_a, 0), 0)),
            pl.BlockSpec((1, d), lambda i: (0, 0)),
            pl.BlockSpec((d, LANES), lambda i: (0, 0)),
            pl.BlockSpec((d, LANES), lambda i: (0, 0)),
            pl.BlockSpec((1, LANES), lambda i: (0, 0)),
        ],
        out_specs=[
            pl.BlockSpec((tm, d), lambda i: (i, 0)),
            pl.BlockSpec((tm, LANES), lambda i: (i, 0)),
            pl.BlockSpec((1, LANES), lambda i: (0, 0)),
        ],
        out_shape=[
            jax.ShapeDtypeStruct((n, d), jnp.float32),
            jax.ShapeDtypeStruct((n, LANES), jnp.float32),
            jax.ShapeDtypeStruct((1, LANES), jnp.float32),
        ],
        compiler_params=_cparams("arbitrary"),
    )(h_a, h_b, g.reshape(1, d), w_hi, w_lo, b_router)


def _experts_kernel(pos_ref, start_ref, count_ref, rows_ref, nused_ref, x_hbm, wg_ref, wu_ref, wd_ref, y_hbm,
                    tok_ref, xbuf, obuf, gsem, osem, *, bm, n_blocks):
    e = pl.program_id(0)
    n_used = nused_ref[0]
    b0 = start_ref[e]

    def start_gather(blk, slot):
        base = blk * bm
        for r in range(bm):
            tok = tok_ref[base + r]
            pltpu.make_async_copy(x_hbm.at[pl.ds(tok, 1), :], xbuf.at[slot, pl.ds(r, 1), :],
                                  gsem.at[slot]).start()

    def wait_gather(slot):
        pltpu.make_async_copy(x_hbm.at[pl.ds(0, bm), :], xbuf.at[slot], gsem.at[slot]).wait()

    def out_copy(blk, slot):
        return pltpu.make_async_copy(obuf.at[slot], y_hbm.at[pl.ds(blk * bm, bm), :], osem.at[slot])

    @pl.when(e == 0)
    def _():
        n_assign = pos_ref.shape[0]

        def clear_padding(ex, _):
            def clear(i, _):
                tok_ref[i] = 0
                return 0
            row0 = start_ref[ex] * bm
            lax.fori_loop(row0 + rows_ref[ex], row0 + count_ref[ex] * bm, clear, 0)
            return 0

        def invert(i8, _):
            for k in range(SUBLANES):
                a = i8 * SUBLANES + k
                tok_ref[pos_ref[a]] = lax.shift_right_logical(a, TOP_K_SHIFT)
            return 0

        lax.fori_loop(0, pl.num_programs(0), clear_padding, 0)
        lax.fori_loop(0, n_assign // SUBLANES, invert, 0)
        start_gather(0, 0)

        @pl.when(n_used > 1)
        def _():
            start_gather(1, 1)

    def block_body(j, _):
        b = b0 + j
        slot = lax.bitwise_and(b, 1)
        gslot = lax.rem(b, GATHER_SLOTS)

        @pl.when(b + GATHER_SLOTS - 1 < n_used)
        def _():
            start_gather(b + GATHER_SLOTS - 1, lax.rem(b + GATHER_SLOTS - 1, GATHER_SLOTS))

        wait_gather(gslot)

        @pl.when(b >= 2)
        def _():
            out_copy(b - 2, slot).wait()

        x = xbuf[gslot].astype(MXU_DTYPE)
        hg = jnp.dot(x, wg_ref[...].astype(MXU_DTYPE), preferred_element_type=jnp.float32)
        hu = jnp.dot(x, wu_ref[...].astype(MXU_DTYPE), preferred_element_type=jnp.float32)
        hid = (jax.nn.silu(hg) * hu).astype(MXU_DTYPE)
        obuf[slot] = jnp.dot(hid, wd_ref[...].astype(MXU_DTYPE), preferred_element_type=jnp.float32)
        out_copy(b, slot).start()
        return 0

    lax.fori_loop(0, count_ref[e], block_body, 0)

    @pl.when(e == pl.num_programs(0) - 1)
    def _():
        @pl.when(n_used >= 2)
        def _():
            out_copy(n_used - 2, n_used % 2).wait()

        out_copy(n_used - 1, (n_used - 1) % 2).wait()
        obuf[0] = jnp.zeros(obuf.shape[1:], jnp.float32)

        def zero_fill(b, _):
            out_copy(b, 0).start()
            out_copy(b, 0).wait()
            return 0

        lax.fori_loop(n_used, n_blocks, zero_fill, 0)


def _experts(xn2, pos, blk_start, blk_count, exp_rows, n_used, w_gate, w_up, w_down, *, n_blocks):
    n, d = xn2.shape
    n_experts, _, f = w_gate.shape
    bm = MOE_BM
    grid_spec = pltpu.PrefetchScalarGridSpec(
        num_scalar_prefetch=5,
        grid=(n_experts,),
        in_specs=[
            pl.BlockSpec(memory_space=pl.ANY),
            pl.BlockSpec((None, d, f), lambda e, *_: (e, 0, 0)),
            pl.BlockSpec((None, d, f), lambda e, *_: (e, 0, 0)),
            pl.BlockSpec((None, f, d), lambda e, *_: (e, 0, 0)),
        ],
        out_specs=pl.BlockSpec(memory_space=pl.ANY),
        scratch_shapes=[
            pltpu.SMEM((n_blocks * bm,), jnp.int32),
            pltpu.VMEM((GATHER_SLOTS, bm, d), jnp.float32),
            pltpu.VMEM((2, bm, d), jnp.float32),
            pltpu.SemaphoreType.DMA((GATHER_SLOTS,)),
            pltpu.SemaphoreType.DMA((2,)),
        ],
    )
    return pl.pallas_call(
        functools.partial(_experts_kernel, bm=bm, n_blocks=n_blocks),
        grid_spec=grid_spec,
        out_shape=jax.ShapeDtypeStruct((n_blocks * bm, d), jnp.float32),
        compiler_params=_gather_cparams(EXPERTS_VMEM_LIMIT_BYTES),
    )(pos, blk_start, blk_count, exp_rows, n_used, xn2, w_gate, w_up, w_down)


def _combine_kernel(pos_ref, y_hbm, h_ref, meta_ref, g_ref, o_ref, ybuf, sem, *, tm, n_steps):
    step = pl.program_id(0)

    def start_gather(blk, slot):
        base = blk * tm * TOP_K
        for r in range(tm):
            for k in range(TOP_K):
                pos = pos_ref[base + r * TOP_K + k]
                pltpu.make_async_copy(y_hbm.at[pl.ds(pos, 1), :], ybuf.at[slot, k, pl.ds(r, 1), :],
                                      sem.at[slot]).start()

    def wait_gather(slot):
        for k in range(TOP_K):
            pltpu.make_async_copy(y_hbm.at[pl.ds(0, tm), :], ybuf.at[slot, k], sem.at[slot]).wait()

    slot = step % 2

    @pl.when(step == 0)
    def _():
        start_gather(0, 0)

    @pl.when(step + 1 < n_steps)
    def _():
        start_gather(step + 1, 1 - slot)

    wait_gather(slot)
    w1 =meta_ref[:, META_W:META_W + 1]
    w2 = meta_ref[:, META_W + 1:META_W + 2]
    h = h_ref[...] + (ybuf[slot, 0] * w1 + ybuf[slot, 1] * w2)
    ms = jnp.mean(h * h, axis=-1, keepdims=True)
    o_ref[...] = (h * lax.rsqrt(ms + EPS)) * g_ref[...]


def _combine(pos, yb, h2, meta, g):
    n, d = h2.shape
    tm = min(COMBINE_TM, n)
    n_steps = n // tm
    grid_spec = pltpu.PrefetchScalarGridSpec(
        num_scalar_prefetch=1,
        grid=(n_steps,),
        in_specs=[
            pl.BlockSpec(memory_space=pl.ANY),
            pl.BlockSpec((tm, d), lambda i, pos: (i, 0)),
            pl.BlockSpec((tm, LANES), lambda i, pos: (i, 0)),
            pl.BlockSpec((1, d), lambda i, pos: (0, 0)),
        ],
        out_specs=pl.BlockSpec((tm, d), lambda i, pos: (i, 0)),
        scratch_shapes=[
            pltpu.VMEM((2, TOP_K, tm, d), jnp.float32),
            pltpu.SemaphoreType.DMA((2,)),
        ],
    )
    return pl.pallas_call(
        functools.partial(_combine_kernel, tm=tm, n_steps=n_steps),
        grid_spec=grid_spec,
        out_shape=jax.ShapeDtypeStruct((n, d), jnp.float32),
        compiler_params=_gather_cparams(VMEM_LIMIT_BYTES),
    )(pos, yb, h2, meta, g.reshape(1, d))


def _dispatch_plan(meta, counts_f, n):
    bm = MOE_BM
    a = n * TOP_K
    expert = meta[:, META_E:META_E + TOP_K].astype(jnp.int32)
    rank = meta[:, META_RANK:META_RANK + TOP_K].astype(jnp.int32)
    counts = counts_f[0, :N_EXPERTS].astype(jnp.int32)
    padded = (counts + bm - 1) // bm * bm
    pad_end = jnp.cumsum(padded)
    pad_start = pad_end - padded
    pos = (pad_start[expert] + rank).reshape(a)
    n_blocks = -(-a // bm) + N_EXPERTS
    blk_start = (pad_start // bm).astype(jnp.int32)
    blk_count = (padded // bm).astype(jnp.int32)
    n_used = (pad_end[-1] // bm).astype(jnp.int32).reshape(1)
    return pos, blk_start, blk_count, counts, n_used, n_blocks


def _mixer_residual(x, p):
    b, s, d = x.shape
    n = b * s
    d_rnn = p["d_rnn"]
    d_attn = p["d_attn"]
    x2 = x.reshape(n, d)
    proj2 = _norm_proj(x2, p["norm_mix_g"], p["w_in"])
    d_in = proj2.shape[1]
    proj3 = proj2.reshape(b, s, d_in)
    hf = _lru_direction(proj3, p["conv_w"], p["conv_b"], p["lru_wg"][0], p["lru_b_a"][0], p["lru_b_x"][0],
                        p["lru_lambda"][0], None, reverse=False, d_rnn=d_rnn)
    ya = _lru_direction(proj3, p["conv_w"], p["conv_b"], p["lru_wg"][1], p["lru_b_a"][1], p["lru_b_x"][1],
                        p["lru_lambda"][1], hf, reverse=True, d_rnn=d_rnn)
    att = _attention(proj3, p["att_table"], d_rnn=d_rnn, d_attn=d_attn)
    m2 = _merge(ya.reshape(n, d_rnn), att.reshape(n, d_attn), p["w_lru_up"], p["w_attn_up"], proj2,
                p["b_merge"], gate_col=2 * d_rnn + 3 * d_attn)
    return _out_proj(m2, p["w_o"], x2)


def _moe_and_final_norm(h_a, h_b, p):
    n_a, n_b = h_a.shape[0], h_b.shape[0]
    xn2, meta, counts = _router(h_a, h_b, p["norm_ffn_g"], p["w_router_hi"], p["w_router_lo"], p["b_router"])
    pos, blk_start, blk_count, exp_rows, n_used, n_blocks = _dispatch_plan(meta, counts, n_a + n_b)
    yb = _experts(xn2, pos, blk_start, blk_count, exp_rows, n_used, p["w_exp_gate"], p["w_exp_up"], p["w_exp_down"],
                  n_blocks=n_blocks)
    y_a = _combine(pos[:n_a * TOP_K], yb, h_a, meta[:n_a], p["norm_final_g"])
    y_b = _combine(pos[n_a * TOP_K:], yb, h_b, meta[n_a:], p["norm_final_g"])
    return y_a, y_b


def _prepare_params(norm_mix_g, w_in, conv_w, conv_b, lru_w_a, lru_b_a, lru_w_x, lru_b_x, lru_lambda, rpb,
                    w_lru_up, w_attn_up, b_merge, w_o, norm_ffn_g, w_router_group, b_router_group,
                    w_router_expert, b_router_expert, w_exp_gate, w_exp_up, w_exp_down, norm_final_g):
    bf16 = MXU_DTYPE
    d = w_in.shape[1]
    d_rnn = conv_w.shape[2]
    d_attn = w_attn_up.shape[1]
    lru_wg = jnp.concatenate([lru_w_a[0], lru_w_x[0]], axis=-1).astype(bf16)
    w_router = jnp.zeros((d, LANES), jnp.float32)
    w_router = w_router.at[:, :N_EXPERTS].set(w_router_expert[0])
    w_router = w_router.at[:, GROUP_LANE0:GROUP_LANE0 + N_GROUPS].set(w_router_group[0])
    w_router_hi = w_router.astype(bf16)
    w_router_lo = (w_router - w_router_hi.astype(jnp.float32)).astype(bf16)
    b_router = jnp.zeros((1, LANES), jnp.float32)
    b_router = b_router.at[0, :N_EXPERTS].set(b_router_expert[0])
    b_router = b_router.at[0, GROUP_LANE0:GROUP_LANE0 + N_GROUPS].set(b_router_group[0])
    return dict(
        d_rnn=d_rnn, d_attn=d_attn,
        norm_mix_g=norm_mix_g[0], w_in=w_in[0].astype(bf16),
        conv_w=conv_w[0], conv_b=conv_b[0], lru_wg=lru_wg,
        lru_b_a=lru_b_a[0], lru_b_x=lru_b_x[0], lru_lambda=lru_lambda[0],
        att_table=_attention_bias_table(rpb[0]),
        w_lru_up=w_lru_up[0].astype(bf16), w_attn_up=w_attn_up[0].astype(bf16),
        b_merge=b_merge[0], w_o=w_o[0].astype(bf16),
        norm_ffn_g=norm_ffn_g[0], w_router_hi=w_router_hi, w_router_lo=w_router_lo, b_router=b_router,
        w_exp_gate=w_exp_gate[0], w_exp_up=w_exp_up[0], w_exp_down=w_exp_down[0],
        norm_final_g=norm_final_g,
    )


def kernel(x_prompt, x_sample, norm_mix_g, w_in, conv_w, conv_b, lru_w_a, lru_b_a, lru_w_x, lru_b_x, lru_lambda, rpb, w_lru_up, w_attn_up, b_merge, w_o, norm_ffn_g, w_router_group, b_router_group, w_router_expert, b_router_expert, w_exp_gate, w_exp_up, w_exp_down, norm_final_g):
    assert norm_mix_g.shape[0] == 1, "single-layer encoder"
    p = _prepare_params(norm_mix_g, w_in, conv_w, conv_b, lru_w_a, lru_b_a, lru_w_x, lru_b_x, lru_lambda, rpb,
                        w_lru_up, w_attn_up, b_merge, w_o, norm_ffn_g, w_router_group, b_router_group,
                        w_router_expert, b_router_expert, w_exp_gate, w_exp_up, w_exp_down, norm_final_g)
    h_prompt = _mixer_residual(x_prompt, p)
    h_sample = _mixer_residual(x_sample, p)
    y_prompt, y_sample = _moe_and_final_norm(h_prompt, h_sample, p)
    return (y_prompt.reshape(x_prompt.shape), y_sample.reshape(x_sample.shape))
```

```python
import functools

import numpy as np
import jax
import jax.numpy as jnp
from jax import lax
from jax.experimental import pallas as pl
from jax.experimental.pallas import tpu as pltpu

GRID_W = 64
WIN_R = 8
WIN_C = 16
CONV_W = 4
LRU_C = 8.0
LRU_BLOCK = 128
HEAD_DIM = 128
N_GROUPS = 4
EXPERTS_PER_GROUP = 8
N_EXPERTS = N_GROUPS * EXPERTS_PER_GROUP
TOP_K = 2
EPS = 1e-6
MASK_VALUE = -1e30

MXU_DTYPE = jnp.bfloat16

LANES = 128
SUBLANES = 8
VMEM_LIMIT_BYTES = 56 * 1024 * 1024
EXPERTS_VMEM_LIMIT_BYTES = 62 * 1024 * 1024

PROJ_TM = 1024
PROJ_TN = 2048
MERGE_TM = 256
MERGE_TN = 2048
OUT_PROJ_TM = 512
OUT_PROJ_TN = 2048
LRU_TC = 4096
LRU_GROUPS = 4
LRU_UNROLL = 4
ATT_ROWS = 32
ROUTER_TM = 512
MOE_BM = 256
COMBINE_TM = 256
GATHER_SLOTS = 3


def _cparams(*sem):
    return pltpu.CompilerParams(dimension_semantics=sem, vmem_limit_bytes=VMEM_LIMIT_BYTES)


def _gather_cparams(vmem_limit):
    return pltpu.CompilerParams(dimension_semantics=("arbitrary",), vmem_limit_bytes=vmem_limit)


def _norm_proj_kernel(x_ref, g_ref, w_ref, o_ref, xn_ref):
    @pl.when(pl.program_id(1) == 0)
    def _():
        x = x_ref[...]
        ms = jnp.mean(x * x, axis=-1, keepdims=True)
        xn_ref[...] = ((x * lax.rsqrt(ms + EPS)) * g_ref[...]).astype(MXU_DTYPE)

    o_ref[...] = jnp.dot(xn_ref[...], w_ref[...], preferred_element_type=jnp.float32)


def _norm_proj(x2, g, w_bf16):
    n, d = x2.shape
    d_out = w_bf16.shape[1]
    tm = min(PROJ_TM, n)
    tn = min(PROJ_TN, d_out)
    return pl.pallas_call(
        _norm_proj_kernel,
        grid=(n // tm, d_out // tn),
        in_specs=[
            pl.BlockSpec((tm, d), lambda i, j: (i, 0)),
            pl.BlockSpec((1, d), lambda i, j: (0, 0)),
            pl.BlockSpec((d, tn), lambda i, j: (0, j)),
        ],
        out_specs=pl.BlockSpec((tm, tn), lambda i, j: (i, j)),
        out_shape=jax.ShapeDtypeStruct((n, d_out), jnp.float32),
        scratch_shapes=[pltpu.VMEM((tm, d), MXU_DTYPE)],
        compiler_params=_cparams("parallel", "arbitrary"),
    )(x2, g.reshape(1, d), w_bf16)


def _softplus(y):
    return jnp.maximum(y, 0.0) + jnp.log1p(jnp.exp(-jnp.abs(y)))


def _lru_pitch(tc):
    seg_len = tc // (LRU_GROUPS * SUBLANES)
    assert seg_len * LRU_GROUPS * SUBLANES == tc and seg_len % SUBLANES == 0
    return seg_len + SUBLANES // 2


def _lru_kernel(*refs, reverse, n_chunks, tc):
    if reverse:
        (x_ref, xp_ref, xn_ref, cw_ref, cb_ref, wg_ref, ba_ref, bx_ref, lam_ref,
         hf_ref, gate_ref, o_ref, xpad_ref, a_ref, u_ref, h_ref, carry_ref) = refs
    else:
        (x_ref, xp_ref, xn_ref, cw_ref, cb_ref, wg_ref, ba_ref, bx_ref, lam_ref,
         o_ref, xpad_ref, a_ref, u_ref, h_ref, carry_ref) = refs
    t = pl.program_id(2)
    chunk = (n_chunks - 1 - t) if reverse else t

    @pl.when(t == 0)
    def _():
        carry_ref[...] = jnp.zeros_like(carry_ref)

    prev = jnp.where(chunk == 0, 0.0, xp_ref[...])
    nxt = jnp.where(chunk == n_chunks - 1, 0.0, xn_ref[...])
    xpad_ref[pl.ds(0, SUBLANES), :] = prev
    xpad_ref[pl.ds(SUBLANES, tc), :] = x_ref[...]
    xpad_ref[pl.ds(SUBLANES + tc, SUBLANES), :] = nxt
    left = CONV_W // 2
    xc = cb_ref[...]
    for tap in range(CONV_W):
        xc = xc + xpad_ref[pl.ds(SUBLANES - left + tap, tc), :] * cw_ref[pl.ds(tap, 1), :]

    z = jnp.dot(xc.astype(MXU_DTYPE), wg_ref[...], preferred_element_type=jnp.float32)
    r = jax.nn.sigmoid(z[:, :LRU_BLOCK] + ba_ref[...])
    i = jax.nn.sigmoid(z[:, LRU_BLOCK:] + bx_ref[...])
    log_a = (-LRU_C * r) * _softplus(-lam_ref[...])
    a = jnp.exp(log_a)
    u = jnp.sqrt(1.0 - a * a) * (i * xc)

    pitch = _lru_pitch(tc)
    slots = LRU_GROUPS * SUBLANES * pitch
    a_ref[pl.ds(0, tc), :] = a
    u_ref[pl.ds(0, tc), :] = u
    a_ref[pl.ds(tc, slots - tc), :] = jnp.ones((slots - tc, LRU_BLOCK), jnp.float32)
    u_ref[pl.ds(tc, slots - tc), :] = jnp.zeros((slots - tc, LRU_BLOCK), jnp.float32)

    def seg_rows(g, k):
        return pl.ds(g * SUBLANES * pitch + k, SUBLANES, stride=pitch)

    def slot_of(kk):
        return (pitch - 1 - kk) if reverse else kk

    def scan_step(kk, carry):
        hs, ps = carry
        k = slot_of(kk)
        new_h, new_p = [], []
        for g in range(LRU_GROUPS):
            a_k = a_ref[seg_rows(g, k), :]
            h = a_k * hs[g] + u_ref[seg_rows(g, k), :]
            h_ref[seg_rows(g, k), :] = h
            new_h.append(h)
            new_p.append(a_k * ps[g])
        return tuple(new_h), tuple(new_p)

    zeros = tuple(jnp.zeros((SUBLANES, LANES), jnp.float32) for _ in range(LRU_GROUPS))
    ones = tuple(jnp.ones((SUBLANES, LANES), jnp.float32) for _ in range(LRU_GROUPS))
    h_end, p_end = lax.fori_loop(0, pitch, scan_step, (zeros, ones), unroll=LRU_UNROLL)

    row = lax.broadcasted_iota(jnp.int32, (SUBLANES, LANES), 0)
    c_in = [None] * LRU_GROUPS
    cg = carry_ref[...]
    for g in (reversed(range(LRU_GROUPS)) if reverse else range(LRU_GROUPS)):
        pa, hb = p_end[g], h_end[g]
        for s in (1, 2, 4):
            keep = (row < SUBLANES - s) if reverse else (row >= s)
            shift = (SUBLANES - s) if reverse else s
            a_sh = jnp.where(keep, pltpu.roll(pa, shift, 0), 1.0)
            b_sh = jnp.where(keep, pltpu.roll(hb, shift, 0), 0.0)
            hb = pa * b_sh + hb
            pa = pa * a_sh
        seg_out = pa * cg + hb
        if reverse:
            c_in[g] = jnp.where(row == SUBLANES - 1, cg, pltpu.roll(seg_out, SUBLANES - 1, 0))
            cg = seg_out[0:1, :]
        else:
            c_in[g] = jnp.where(row == 0, cg, pltpu.roll(seg_out, 1, 0))
            cg = seg_out[SUBLANES - 1:SUBLANES, :]
    carry_ref[...] = cg

    def fix_step(kk, ps):
        k = slot_of(kk)
        new_p = []
        for g in range(LRU_GROUPS):
            p = a_ref[seg_rows(g, k), :] * ps[g]
            u_ref[seg_rows(g, k), :] = h_ref[seg_rows(g, k), :] + p * c_in[g]
            new_p.append(p)
        return tuple(new_p)

    lax.fori_loop(0, pitch, fix_step, ones, unroll=LRU_UNROLL)

    h_all = u_ref[pl.ds(0, tc), :]
    if reverse:
        o_ref[...] = (jax.nn.gelu(gate_ref[...]) * (hf_ref[...] + h_all)).astype(o_ref.dtype)
    else:
        o_ref[...] = h_all


def _lru_direction(proj3, conv_w, conv_b, wg, ba, bx, lam, hf, *, reverse, d_rnn):
    b, s, _ = proj3.shape
    n_cb = d_rnn // LRU_BLOCK
    tc = min(LRU_TC, s)
    n_chunks = s // tc
    tc8 = tc // SUBLANES
    s8 = s // SUBLANES
    slots = LRU_GROUPS * SUBLANES * _lru_pitch(tc)

    def chunk_of(t):
        return (n_chunks - 1 - t) if reverse else t

    vec_spec = pl.BlockSpec((1, LRU_BLOCK), lambda bi, c, t: (0, c))
    in_specs = [
        pl.BlockSpec((None, tc, LRU_BLOCK), lambda bi, c, t: (bi, chunk_of(t), c)),
        pl.BlockSpec((None, SUBLANES, LRU_BLOCK),
                     lambda bi, c, t: (bi, jnp.maximum(chunk_of(t) * tc8 - 1, 0), c)),
        pl.BlockSpec((None, SUBLANES, LRU_BLOCK),
                     lambda bi, c, t: (bi, jnp.minimum((chunk_of(t) + 1) * tc8, s8 - 1), c)),
        pl.BlockSpec((CONV_W, LRU_BLOCK), lambda bi, c, t: (0, c)),
        vec_spec,
        pl.BlockSpec((None, LRU_BLOCK, 2 * LRU_BLOCK), lambda bi, c, t: (c, 0, 0)),
        vec_spec, vec_spec, vec_spec,
    ]
    args = [proj3, proj3, proj3, conv_w, conv_b.reshape(1, d_rnn), wg,
            ba.reshape(1, d_rnn), bx.reshape(1, d_rnn), lam.reshape(1, d_rnn)]
    if reverse:
        in_specs += [
            pl.BlockSpec((None, tc, LRU_BLOCK), lambda bi, c, t: (bi, chunk_of(t), c)),
            pl.BlockSpec((None, tc, LRU_BLOCK), lambda bi, c, t: (bi, chunk_of(t), n_cb + c)),
        ]
        args += [hf, proj3]
        out_dtype = MXU_DTYPE
    else:
        out_dtype = jnp.float32
    return pl.pallas_call(
        functools.partial(_lru_kernel, reverse=reverse, n_chunks=n_chunks, tc=tc),
        grid=(b, n_cb, n_chunks),
        in_specs=in_specs,
        out_specs=pl.BlockSpec((None, tc, LRU_BLOCK), lambda bi, c, t: (bi, chunk_of(t), c)),
        out_shape=jax.ShapeDtypeStruct((b, s, d_rnn), out_dtype),
        scratch_shapes=[
            pltpu.VMEM((tc + 2 * SUBLANES, LRU_BLOCK), jnp.float32),
            pltpu.VMEM((slots, LRU_BLOCK), jnp.float32),
            pltpu.VMEM((slots, LRU_BLOCK), jnp.float32),
            pltpu.VMEM((slots, LRU_BLOCK), jnp.float32),
            pltpu.VMEM((1, LRU_BLOCK), jnp.float32),
        ],
        compiler_params=_cparams("parallel", "parallel", "arbitrary"),
    )(*args)


def _attention_bias_table(rpb):
    n_heads, n_dr, n_dc = rpb.shape
    cq = np.arange(GRID_W)
    cs = np.clip(cq - WIN_C // 2, 0, GRID_W - WIN_C)
    ck = np.arange(GRID_W)
    valid = (ck[None, :] >= cs[:, None]) & (ck[None, :] < cs[:, None] + WIN_C)
    pad = GRID_W - WIN_C
    width = n_dc + 2 * pad + 1
    padded = jnp.pad(rpb.astype(jnp.float32), ((0, 0), (0, 0), (pad, pad + 1)))
    tiled = jnp.tile(padded, (1, 1, GRID_W))[:, :, :GRID_W * (width - 1)]
    skew = tiled.reshape(n_heads, n_dr, GRID_W, width - 1)
    toep = skew[:, :, :, GRID_W - 1:2 * GRID_W - 1]
    toep = jnp.where(valid[None, None], toep, MASK_VALUE)
    per_off = [toep[:, WIN_R - 1 - off:2 * WIN_R - 1 - off] for off in range(WIN_R)]
    tab = jnp.stack(per_off, axis=1)
    tab = tab.transpose(0, 1, 3, 2, 4)
    return tab.reshape(n_heads, WIN_R, GRID_W, WIN_R * GRID_W)


def _attention_kernel(q_ref, k_ref, v_ref, tb_ref, o_ref, kb_ref, vb_ref, *, rows, tile_rows):
    kb_ref[...] = k_ref[...].astype(MXU_DTYPE)
    vb_ref[...] = v_ref[...].astype(MXU_DTYPE)
    scale = HEAD_DIM ** -0.5
    band = WIN_R * GRID_W

    def tile_body(i, _):
        starts, scores = [], []
        for rl in range(tile_rows):
            r = i * tile_rows + rl
            rs = jnp.clip(r - WIN_R // 2, 0, rows - WIN_R)
            start = pl.multiple_of(rs * GRID_W, GRID_W)
            q0 = pl.multiple_of(r * GRID_W, GRID_W)
            q = (q_ref[pl.ds(q0, GRID_W), :] * scale).astype(MXU_DTYPE)
            kband = kb_ref[pl.ds(start, band), :]
            s = lax.dot_general(q, kband, (((1,), (1,)), ((), ())), preferred_element_type=jnp.float32)
            starts.append(start)
            scores.append(s + tb_ref[r - rs])
        probs, denoms = [], []
        for s in scores:
            m = jnp.max(s, axis=-1, keepdims=True)
            p = jnp.exp(s - m)
            denoms.append(jnp.sum(p, axis=-1, keepdims=True))
            probs.append(p.astype(MXU_DTYPE))
        for rl in range(tile_rows):
            vband = vb_ref[pl.ds(starts[rl], band), :]
            o = jnp.dot(probs[rl], vband, preferred_element_type=jnp.float32)
            q0 = pl.multiple_of((i * tile_rows + rl) * GRID_W, GRID_W)
            o_ref[pl.ds(q0, GRID_W), :] = (o / denoms[rl]).astype(o_ref.dtype)
        return 0

    lax.fori_loop(0, rows // tile_rows, tile_body, 0)


def _attention(proj3, table, *, d_rnn, d_attn):
    b, s, _ = proj3.shape
    rows = s // GRID_W
    n_heads = d_attn // HEAD_DIM
    q_blk = 2 * d_rnn // HEAD_DIM
    k_blk = q_blk + n_heads
    v_blk = k_blk + n_heads
    tile_rows = min(ATT_ROWS, rows)
    assert rows % tile_rows == 0 and rows >= WIN_R
    return pl.pallas_call(
        functools.partial(_attention_kernel, rows=rows, tile_rows=tile_rows),
        grid=(b, n_heads),
        in_specs=[
            pl.BlockSpec((None, s, HEAD_DIM), lambda bi, h: (bi, 0, q_blk + h)),
            pl.BlockSpec((None, s, HEAD_DIM), lambda bi, h: (bi, 0, k_blk + h)),
            pl.BlockSpec((None, s, HEAD_DIM), lambda bi, h: (bi, 0, v_blk + h)),
            pl.BlockSpec((None, WIN_R, GRID_W, WIN_R * GRID_W), lambda bi, h: (h, 0, 0, 0)),
        ],
        out_specs=pl.BlockSpec((None, s, HEAD_DIM), lambda bi, h: (bi, 0, h)),
        out_shape=jax.ShapeDtypeStruct((b, s, d_attn), MXU_DTYPE),
        scratch_shapes=[
            pltpu.VMEM((s, HEAD_DIM), MXU_DTYPE),
            pltpu.VMEM((s, HEAD_DIM), MXU_DTYPE),
        ],
        compiler_params=_cparams("parallel", "parallel"),
    )(proj3, proj3, proj3, table)


def _weight_spec(d, tn):
    if tn == d:
        return pl.BlockSpec((d, tn), lambda i, j: (0, j), pipeline_mode=pl.Buffered(1))
    return pl.BlockSpec((d, tn), lambda i, j: (0, j))


def _merge_kernel(ya_ref, att_ref, wl_ref, wa_ref, gl_ref, ga_ref, bm_ref, o_ref):
    y_lru = jnp.dot(ya_ref[...], wl_ref[...], preferred_element_type=jnp.float32)
    y_att = jnp.dot(att_ref[...], wa_ref[...], preferred_element_type=jnp.float32)
    s_lru = jax.nn.sigmoid(gl_ref[...] + bm_ref[pl.ds(0, 1), :])
    s_att = jax.nn.sigmoid(ga_ref[...] + bm_ref[pl.ds(1, 1), :])
    o_ref[...] = (s_lru * y_lru + s_att * y_att).astype(o_ref.dtype)


def _merge(ya2, att2, wl, wa, proj2, b_merge, *, gate_col):
    n, d = ya2.shape
    tm = min(MERGE_TM, n)
    tn = min(MERGE_TN, d)
    gl_blk = gate_col // tn
    ga_blk = (gate_col + d) // tn
    return pl.pallas_call(
        _merge_kernel,
        grid=(n // tm, d // tn),
        in_specs=[
            pl.BlockSpec((tm, d), lambda i, j: (i, 0)),
            pl.BlockSpec((tm, d), lambda i, j: (i, 0)),
            _weight_spec(d, tn),
            _weight_spec(d, tn),
            pl.BlockSpec((tm, tn), lambda i, j: (i, gl_blk + j)),
            pl.BlockSpec((tm, tn), lambda i, j: (i, ga_blk + j)),
            pl.BlockSpec((2, tn), lambda i, j: (0, j)),
        ],
        out_specs=pl.BlockSpec((tm, tn), lambda i, j: (i, j)),
        out_shape=jax.ShapeDtypeStruct((n, d), MXU_DTYPE),
        compiler_params=_cparams("parallel", "arbitrary"),
    )(ya2, att2, wl, wa, proj2, proj2, b_merge)


def _out_proj_kernel(m_ref, w_ref, x_ref, o_ref):
    o_ref[...] = x_ref[...] + jnp.dot(m_ref[...], w_ref[...], preferred_element_type=jnp.float32)


def _out_proj(m2, w_o, x2):
    n, d = m2.shape
    tm = min(OUT_PROJ_TM, n)
    tn = min(OUT_PROJ_TN, d)
    return pl.pallas_call(
        _out_proj_kernel,
        grid=(n // tm, d // tn),
        in_specs=[
            pl.BlockSpec((tm, d), lambda i, j: (i, 0)),
            _weight_spec(d, tn),
            pl.BlockSpec((tm, tn), lambda i, j: (i, j)),
        ],
        out_specs=pl.BlockSpec((tm, tn), lambda i, j: (i, j)),
        out_shape=jax.ShapeDtypeStruct((n, d), jnp.float32),
        compiler_params=_cparams("parallel", "arbitrary"),
    )(m2, w_o, x2)


GROUP_LANE0 = N_EXPERTS
META_E, META_W, META_RANK = 0, 2, 4
GROUP_SHIFT = EXPERTS_PER_GROUP.bit_length() - 1
TOP_K_SHIFT = TOP_K.bit_length() - 1
assert 1 << GROUP_SHIFT == EXPERTS_PER_GROUP and 1 << TOP_K_SHIFT == TOP_K


def _router_kernel(ha_ref, hb_ref, g_ref, whi_ref, wlo_ref, b_ref, xn_ref, meta_ref, cnt_ref, *, steps_a):
    step = pl.program_id(0)

    @pl.when(step == 0)
    def _():
        cnt_ref[...] = jnp.zeros_like(cnt_ref)

    x = jnp.where(step < steps_a, ha_ref[...], hb_ref[...])
    tm = x.shape[0]
    ms = jnp.mean(x * x, axis=-1, keepdims=True)
    xn = (x * lax.rsqrt(ms + EPS)) * g_ref[...]
    xn_ref[...] = xn
    x_hi = xn.astype(MXU_DTYPE)
    x_lo = (xn - x_hi.astype(jnp.float32)).astype(MXU_DTYPE)
    logits = jnp.dot(x_hi, whi_ref[...], preferred_element_type=jnp.float32)
    logits = logits + (jnp.dot(x_hi, wlo_ref[...], preferred_element_type=jnp.float32)
                       + jnp.dot(x_lo, whi_ref[...], preferred_element_type=jnp.float32))
    logits = logits + b_ref[...]
    lane = lax.broadcasted_iota(jnp.int32, (tm, LANES), 1)
    neg_inf = -jnp.inf

    def first_argmax(vals, vmax):
        return jnp.min(jnp.where(vals == vmax, lane, LANES), axis=-1, keepdims=True)

    is_group = (lane >= GROUP_LANE0) & (lane < GROUP_LANE0 + N_GROUPS)
    lg = jnp.where(is_group, logits, neg_inf)
    mg = jnp.max(lg, axis=-1, keepdims=True)
    g_idx = first_argmax(lg, mg) - GROUP_LANE0
    g_val = 1.0 / jnp.sum(jnp.exp(lg - mg), axis=-1, keepdims=True)

    in_group = (lane < N_EXPERTS) & (lax.shift_right_logical(lane, GROUP_SHIFT) == g_idx)
    le = jnp.where(in_group, logits, neg_inf)
    m1 = jnp.max(le, axis=-1, keepdims=True)
    e1 = first_argmax(le, m1)
    le2 = jnp.where(lane == e1, neg_inf, le)
    m2 = jnp.max(le2, axis=-1, keepdims=True)
    e2 = first_argmax(le2, m2)
    z = jnp.sum(jnp.exp(le - m1), axis=-1, keepdims=True)
    p1 = 1.0 / z
    p2 = jnp.exp(m2 - m1) / z
    den = p1 + p2
    w1 = g_val * (p1 / den)
    w2 = g_val * (p2 / den)

    hot1 = lane == e1
    hot2 = lane == e2
    hot = hot1.astype(jnp.float32) + hot2.astype(jnp.float32)
    r_i = lax.broadcasted_iota(jnp.int32, (tm, tm), 0)
    c_i = lax.broadcasted_iota(jnp.int32, (tm, tm), 1)
    lower = (c_i < r_i).astype(MXU_DTYPE)
    before = jnp.dot(lower, hot.astype(MXU_DTYPE), preferred_element_type=jnp.float32) + cnt_ref[...]
    rank1 = jnp.sum(jnp.where(hot1, before, 0.0), axis=-1, keepdims=True)
    rank2 = jnp.sum(jnp.where(hot2, before, 0.0), axis=-1, keepdims=True)
    cnt_ref[...] += jnp.sum(hot, axis=0, keepdims=True)

    meta = jnp.where(lane == META_E, e1.astype(jnp.float32), 0.0)
    meta = jnp.where(lane == META_E + 1, e2.astype(jnp.float32), meta)
    meta = jnp.where(lane == META_W, w1, meta)
    meta = jnp.where(lane == META_W + 1, w2, meta)
    meta = jnp.where(lane == META_RANK, rank1, meta)
    meta = jnp.where(lane == META_RANK + 1, rank2, meta)
    meta_ref[...] = meta


def _router(h_a, h_b, g, w_hi, w_lo, b_router):
    n_a, d = h_a.shape
    n_b = h_b.shape[0]
    n = n_a + n_b
    tm = min(ROUTER_TM, n_a, n_b)
    steps_a = n_a // tm
    return pl.pallas_call(
        functools.partial(_router_kernel, steps_a=steps_a),
        grid=(n // tm,),
        in_specs=[
            pl.BlockSpec((tm, d), lambda i: (jnp.minimum(i, steps_a - 1), 0)),
            pl.BlockSpec((tm, d), lambda i: (jnp.maximum(i - steps_a, 0), 0)),
            pl.BlockSpec((1, d), lambda i: (0, 0)),
            pl.BlockSpec((d, LANES), lambda i: (0, 0)),
            pl.BlockSpec((d, LANES), lambda i: (0, 0)),
            pl.BlockSpec((1, LANES), lambda i: (0, 0)),
        ],
        out_specs=[
            pl.BlockSpec((tm, d), lambda i: (i, 0)),
            pl.BlockSpec((tm, LANES), lambda i: (i, 0)),
            pl.BlockSpec((1, LANES), lambda i: (0, 0)),
        ],
        out_shape=[
            jax.ShapeDtypeStruct((n, d), jnp.float32),
            jax.ShapeDtypeStruct((n, LANES), jnp.float32),
            jax.ShapeDtypeStruct((1, LANES), jnp.float32),
        ],
        compiler_params=_cparams("arbitrary"),
    )(h_a, h_b, g.reshape(1, d), w_hi, w_lo, b_router)


def _experts_kernel(pos_ref, start_ref, count_ref, rows_ref, nused_ref, x_hbm, wg_ref, wu_ref, wd_ref, y_hbm,
                    tok_ref, xbuf, obuf, gsem, osem, *, bm, n_blocks):
    e = pl.program_id(0)
    n_used = nused_ref[0]
    b0 = start_ref[e]

    def start_gather(blk, slot):
        base = blk * bm
        for r in range(bm):
            tok = tok_ref[base + r]
            pltpu.make_async_copy(x_hbm.at[pl.ds(tok, 1), :], xbuf.at[slot, pl.ds(r, 1), :],
                                  gsem.at[slot]).start()

    def wait_gather(slot):
        pltpu.make_async_copy(x_hbm.at[pl.ds(0, bm), :], xbuf.at[slot], gsem.at[slot]).wait()

    def out_copy(blk, slot):
        return pltpu.make_async_copy(obuf.at[slot], y_hbm.at[pl.ds(blk * bm, bm), :], osem.at[slot])

    @pl.when(e == 0)
    def _():
        n_assign = pos_ref.shape[0]

        def clear_padding(ex, _):
            def clear(i, _):
                tok_ref[i] = 0
                return 0
            row0 = start_ref[ex] * bm
            lax.fori_loop(row0 + rows_ref[ex], row0 + count_ref[ex] * bm, clear, 0)
            return 0

        def invert(i8, _):
            for k in range(SUBLANES):
                a = i8 * SUBLANES + k
                tok_ref[pos_ref[a]] = lax.shift_right_logical(a, TOP_K_SHIFT)
            return 0

        lax.fori_loop(0, pl.num_programs(0), clear_padding, 0)
        lax.fori_loop(0, n_assign // SUBLANES, invert, 0)
        start_gather(0, 0)

        @pl.when(n_used > 1)
        def _():
            start_gather(1, 1)

    def block_body(j, _):
        b = b0 + j
        slot = lax.bitwise_and(b, 1)
        gslot = lax.rem(b, GATHER_SLOTS)

        @pl.when(b + GATHER_SLOTS - 1 < n_used)
        def _():
            start_gather(b + GATHER_SLOTS - 1, lax.rem(b + GATHER_SLOTS - 1, GATHER_SLOTS))

        wait_gather(gslot)

        @pl.when(b >= 2)
        def _():
            out_copy(b - 2, slot).wait()

        x = xbuf[gslot].astype(MXU_DTYPE)
        hg = jnp.dot(x, wg_ref[...].astype(MXU_DTYPE), preferred_element_type=jnp.float32)
        hu = jnp.dot(x, wu_ref[...].astype(MXU_DTYPE), preferred_element_type=jnp.float32)
        hid = (jax.nn.silu(hg) * hu).astype(MXU_DTYPE)
        obuf[slot] = jnp.dot(hid, wd_ref[...].astype(MXU_DTYPE), preferred_element_type=jnp.float32)
        out_copy(b, slot).start()
        return 0

    lax.fori_loop(0, count_ref[e], block_body, 0)

    @pl.when(e == pl.num_programs(0) - 1)
    def _():
        @pl.when(n_used >= 2)
        def _():
            out_copy(n_used - 2, n_used % 2).wait()

        out_copy(n_used - 1, (n_used - 1) % 2).wait()
        obuf[0] = jnp.zeros(obuf.shape[1:], jnp.float32)

        def zero_fill(b, _):
            out_copy(b, 0).start()
            out_copy(b, 0).wait()
            return 0

        lax.fori_loop(n_used, n_blocks, zero_fill, 0)


def _experts(xn2, pos, blk_start, blk_count, exp_rows, n_used, w_gate, w_up, w_down, *, n_blocks):
    n, d = xn2.shape
    n_experts, _, f = w_gate.shape
    bm = MOE_BM
    grid_spec = pltpu.PrefetchScalarGridSpec(
        num_scalar_prefetch=5,
        grid=(n_experts,),
        in_specs=[
            pl.BlockSpec(memory_space=pl.ANY),
            pl.BlockSpec((None, d, f), lambda e, *_: (e, 0, 0)),
            pl.BlockSpec((None, d, f), lambda e, *_: (e, 0, 0)),
            pl.BlockSpec((None, f, d), lambda e, *_: (e, 0, 0)),
        ],
        out_specs=pl.BlockSpec(memory_space=pl.ANY),
        scratch_shapes=[
            pltpu.SMEM((n_blocks * bm,), jnp.int32),
            pltpu.VMEM((GATHER_SLOTS, bm, d), jnp.float32),
            pltpu.VMEM((2, bm, d), jnp.float32),
            pltpu.SemaphoreType.DMA((GATHER_SLOTS,)),
            pltpu.SemaphoreType.DMA((2,)),
        ],
    )
    return pl.pallas_call(
        functools.partial(_experts_kernel, bm=bm, n_blocks=n_blocks),
        grid_spec=grid_spec,
        out_shape=jax.ShapeDtypeStruct((n_blocks * bm, d), jnp.float32),
        compiler_params=_gather_cparams(EXPERTS_VMEM_LIMIT_BYTES),
    )(pos, blk_start, blk_count, exp_rows, n_used, xn2, w_gate, w_up, w_down)


def _combine_kernel(pos_ref, y_hbm, h_ref, meta_ref, g_ref, o_ref, ybuf, sem, *, tm, n_steps, blk0):
    step = pl.program_id(0)

    def start_gather(blk, slot):
        base = (blk0 + blk) * tm * TOP_K
        for r in range(tm):
            for k in range(TOP_K):
                pos = pos_ref[base + r * TOP_K + k]
                pltpu.make_async_copy(y_hbm.at[pl.ds(pos, 1), :], ybuf.at[slot, k, pl.ds(r, 1), :],
                                      sem.at[slot]).start()

    def wait_gather(slot):
        for k in range(TOP_K):
            pltpu.make_async_copy(y_hbm.at[pl.ds(0, tm), :], ybuf.at[slot, k], sem.at[slot]).wait()

    slot = step % 2

    @pl.when(step == 0)
    def _():
        start_gather(0, 0)

    @pl.when(step + 1 < n_steps)
    def _():
        start_gather(step + 1, 1 - slot)

    wait_gather(slot)
    w1 = meta_ref[:, META_W:META_W + 1]
    w2 = meta_ref[:, META_W + 1:META_W + 2]
    h = h_ref[...] + (ybuf[slot, 0] * w1 + ybuf[slot, 1] * w2)
    ms = jnp.mean(h * h, axis=-1, keepdims=True)
    o_ref[...] = (h * lax.rsqrt(ms + EPS)) * g_ref[...]


def _combine(pos, yb, h2, meta, g, row0):
    n, d = h2.shape
    tm = min(COMBINE_TM, n)
    n_steps = n // tm
    assert row0 % tm == 0
    blk0 = row0 // tm
    grid_spec = pltpu.PrefetchScalarGridSpec(
        num_scalar_prefetch=1,
        grid=(n_steps,),
        in_specs=[
            pl.BlockSpec(memory_space=pl.ANY),
            pl.BlockSpec((tm, d), lambda i, pos: (i, 0)),
            pl.BlockSpec((tm, LANES), lambda i, pos: (blk0 + i, 0)),
            pl.BlockSpec((1, d), lambda i, pos: (0, 0)),
        ],
        out_specs=pl.BlockSpec((tm, d), lambda i, pos: (i, 0)),
        scratch_shapes=[
            pltpu.VMEM((2, TOP_K, tm, d), jnp.float32),
            pltpu.SemaphoreType.DMA((2,)),
        ],
    )
    return pl.pallas_call(
        functools.partial(_combine_kernel, tm=tm, n_steps=n_steps, blk0=blk0),
        grid_spec=grid_spec,
        out_shape=jax.ShapeDtypeStruct((n, d), jnp.float32),
        compiler_params=_gather_cparams(VMEM_LIMIT_BYTES),
    )(pos, yb, h2, meta, g.reshape(1, d))


def _dispatch_plan(meta, counts_f, n):
    bm = MOE_BM
    a = n * TOP_K
    expert = meta[:, META_E:META_E + TOP_K].astype(jnp.int32)
    rank = meta[:, META_RANK:META_RANK + TOP_K].astype(jnp.int32)
    counts = counts_f[0, :N_EXPERTS].astype(jnp.int32)
    padded = (counts + bm - 1) // bm * bm
    pad_end = jnp.cumsum(padded)
    pad_start = pad_end - padded
    pos = (pad_start[expert] + rank).reshape(a)
    n_blocks = -(-a // bm) + N_EXPERTS
    blk_start = (pad_start // bm).astype(jnp.int32)
    blk_count = (padded // bm).astype(jnp.int32)
    n_used = (pad_end[-1] // bm).astype(jnp.int32).reshape(1)
    return pos, blk_start, blk_count, counts, n_used, n_blocks


def _mixer_residual(x, p):
    b, s, d = x.shape
    n = b * s
    d_rnn = p["d_rnn"]
    d_attn = p["d_attn"]
    x2 = x.reshape(n, d)
    proj2 = _norm_proj(x2, p["norm_mix_g"], p["w_in"])
    d_in = proj2.shape[1]
    proj3 = proj2.reshape(b, s, d_in)
    hf = _lru_direction(proj3, p["conv_w"], p["conv_b"], p["lru_wg"][0], p["lru_b_a"][0], p["lru_b_x"][0],
                        p["lru_lambda"][0], None, reverse=False, d_rnn=d_rnn)
    ya = _lru_direction(proj3, p["conv_w"], p["conv_b"], p["lru_wg"][1], p["lru_b_a"][1], p["lru_b_x"][1],
                        p["lru_lambda"][1], hf, reverse=True, d_rnn=d_rnn)
    att = _attention(proj3, p["att_table"], d_rnn=d_rnn, d_attn=d_attn)
    m2 = _merge(ya.reshape(n, d_rnn), att.reshape(n, d_attn), p["w_lru_up"], p["w_attn_up"], proj2,
                p["b_merge"], gate_col=2 * d_rnn + 3 * d_attn)
    return _out_proj(m2, p["w_o"], x2)


def _moe_and_final_norm(h_a, h_b, p):
    n_a, n_b = h_a.shape[0], h_b.shape[0]
    xn2, meta, counts = _router(h_a, h_b, p["norm_ffn_g"], p["w_router_hi"], p["w_router_lo"], p["b_router"])
    pos, blk_start, blk_count, exp_rows, n_used, n_blocks = _dispatch_plan(meta, counts, n_a + n_b)
    yb = _experts(xn2, pos, blk_start, blk_count, exp_rows, n_used, p["w_exp_gate"], p["w_exp_up"], p["w_exp_down"],
                  n_blocks=n_blocks)
    y_a = _combine(pos, yb, h_a, meta, p["norm_final_g"], 0)
    y_b = _combine(pos, yb, h_b, meta, p["norm_final_g"], n_a)
    return y_a, y_b


def _prepare_params(norm_mix_g, w_in, conv_w, conv_b, lru_w_a, lru_b_a, lru_w_x, lru_b_x, lru_lambda, rpb,
                    w_lru_up, w_attn_up, b_merge, w_o, norm_ffn_g, w_router_group, b_router_group,
                    w_router_expert, b_router_expert, w_exp_gate, w_exp_up, w_exp_down, norm_final_g):
    bf16 = MXU_DTYPE
    d = w_in.shape[1]
    d_rnn = conv_w.shape[2]
    d_attn = w_attn_up.shape[1]
    lru_wg = jnp.concatenate([lru_w_a[0], lru_w_x[0]], axis=-1).astype(bf16)
    w_router = jnp.zeros((d, LANES), jnp.float32)
    w_router = w_router.at[:, :N_EXPERTS].set(w_router_expert[0])
    w_router = w_router.at[:, GROUP_LANE0:GROUP_LANE0 + N_GROUPS].set(w_router_group[0])
    w_router_hi = w_router.astype(bf16)
    w_router_lo = (w_router - w_router_hi.astype(jnp.float32)).astype(bf16)
    b_router = jnp.zeros((1, LANES), jnp.float32)
    b_router = b_router.at[0, :N_EXPERTS].set(b_router_expert[0])
    b_router = b_router.at[0, GROUP_LANE0:GROUP_LANE0 + N_GROUPS].set(b_router_group[0])
    return dict(
        d_rnn=d_rnn, d_attn=d_attn,
        norm_mix_g=norm_mix_g[0], w_in=w_in[0].astype(bf16),
        conv_w=conv_w[0], conv_b=conv_b[0], lru_wg=lru_wg,
        lru_b_a=lru_b_a[0], lru_b_x=lru_b_x[0], lru_lambda=lru_lambda[0],
        att_table=_attention_bias_table(rpb[0]),
        w_lru_up=w_lru_up[0].astype(bf16), w_attn_up=w_attn_up[0].astype(bf16),
        b_merge=b_merge[0], w_o=w_o[0].astype(bf16),
        norm_ffn_g=norm_ffn_g[0], w_router_hi=w_router_hi, w_router_lo=w_router_lo, b_router=b_router,
        w_exp_gate=w_exp_gate[0], w_exp_up=w_exp_up[0], w_exp_down=w_exp_down[0],
        norm_final_g=norm_final_g,
    )


def kernel(x_prompt, x_sample, norm_mix_g, w_in, conv_w, conv_b, lru_w_a, lru_b_a, lru_w_x, lru_b_x, lru_lambda, rpb, w_lru_up, w_attn_up, b_merge, w_o, norm_ffn_g, w_router_group, b_router_group, w_router_expert, b_router_expert, w_exp_gate, w_exp_up, w_exp_down, norm_final_g):
    assert norm_mix_g.shape[0] == 1, "single-layer encoder"
    p = _prepare_params(norm_mix_g, w_in, conv_w, conv_b, lru_w_a, lru_b_a, lru_w_x, lru_b_x, lru_lambda, rpb,
                        w_lru_up, w_attn_up, b_merge, w_o, norm_ffn_g, w_router_group, b_router_group,
                        w_router_expert, b_router_expert, w_exp_gate, w_exp_up, w_exp_down, norm_final_g)
    h_prompt = _mixer_residual(x_prompt, p)
    h_sample = _mixer_residual(x_sample, p)
    y_prompt, y_sample = _moe_and_final_norm(h_prompt, h_sample, p)
    return (y_prompt.reshape(x_prompt.shape), y_sample.reshape(x_sample.shape))
```

```python
import functools

import numpy as np
import jax
import jax.numpy as jnp
from jax import lax
from jax.experimental import pallas as pl
from jax.experimental.pallas import tpu as pltpu

GRID_W = 64
WIN_R = 8
WIN_C = 16
CONV_W = 4
LRU_C = 8.0
LRU_BLOCK = 128
HEAD_DIM = 128
N_GROUPS = 4
EXPERTS_PER_GROUP = 8
N_EXPERTS = N_GROUPS * EXPERTS_PER_GROUP
TOP_K = 2
EPS = 1e-6
MASK_VALUE = -1e30

MXU_DTYPE = jnp.bfloat16

LANES = 128
SUBLANES = 8
VMEM_LIMIT_BYTES = 56 * 1024 * 1024
EXPERTS_VMEM_LIMIT_BYTES = 62 * 1024 * 1024

PROJ_TM = 1024
PROJ_TN = 2048
MERGE_TM = 256
MERGE_TN = 2048
OUT_PROJ_TM = 512
OUT_PROJ_TN = 2048
LRU_TC = 8192
LRU_GROUPS = 4
LRU_UNROLL = 4
ATT_ROWS = 32
ROUTER_TM = 512
MOE_BM = 256
COMBINE_TM = 512
GATHER_SLOTS = 3


def _cparams(*sem):
    return pltpu.CompilerParams(dimension_semantics=sem, vmem_limit_bytes=VMEM_LIMIT_BYTES)


def _gather_cparams(vmem_limit):
    return pltpu.CompilerParams(dimension_semantics=("arbitrary",), vmem_limit_bytes=vmem_limit)


def _norm_proj_kernel(x_ref, g_ref, w_ref, o_ref, xn_ref):
    @pl.when(pl.program_id(1) == 0)
    def _():
        x = x_ref[...]
        ms = jnp.mean(x * x, axis=-1, keepdims=True)
        xn_ref[...] = ((x * lax.rsqrt(ms + EPS)) * g_ref[...]).astype(MXU_DTYPE)

    o_ref[...] = jnp.dot(xn_ref[...], w_ref[...], preferred_element_type=jnp.float32)


def _norm_proj(x2, g, w_bf16):
    n, d = x2.shape
    d_out = w_bf16.shape[1]
    tm = min(PROJ_TM, n)
    tn = min(PROJ_TN, d_out)
    return pl.pallas_call(
        _norm_proj_kernel,
        grid=(n // tm, d_out // tn),
        in_specs=[
            pl.BlockSpec((tm, d), lambda i, j: (i, 0)),
            pl.BlockSpec((1, d), lambda i, j: (0, 0)),
            pl.BlockSpec((d, tn), lambda i, j: (0, j)),
        ],
        out_specs=pl.BlockSpec((tm, tn), lambda i, j: (i, j)),
        out_shape=jax.ShapeDtypeStruct((n, d_out), jnp.float32),
        scratch_shapes=[pltpu.VMEM((tm, d), MXU_DTYPE)],
        compiler_params=_cparams("parallel", "arbitrary"),
    )(x2, g.reshape(1, d), w_bf16)


def _softplus(y):
    return jnp.maximum(y, 0.0) + jnp.log1p(jnp.exp(-jnp.abs(y)))


def _lru_pitch(tc):
    seg_len = tc // (LRU_GROUPS * SUBLANES)
    assert seg_len * LRU_GROUPS * SUBLANES == tc and seg_len % SUBLANES == 0
    return seg_len + SUBLANES // 2


def _lru_kernel(*refs, reverse, n_chunks, tc):
    if reverse:
        (x_ref, xp_ref, xn_ref, cw_ref, cb_ref, wg_ref, ba_ref, bx_ref, lam_ref,
         hf_ref, gate_ref, o_ref, xpad_ref, a_ref, u_ref, h_ref, carry_ref) = refs
    else:
        (x_ref, xp_ref, xn_ref, cw_ref, cb_ref, wg_ref, ba_ref, bx_ref, lam_ref,
         o_ref, xpad_ref, a_ref, u_ref, h_ref, carry_ref) = refs
    t = pl.program_id(2)
    chunk = (n_chunks - 1 - t) if reverse else t

    @pl.when(t == 0)
    def _():
        carry_ref[...] = jnp.zeros_like(carry_ref)

    prev = jnp.where(chunk == 0, 0.0, xp_ref[...])
    nxt = jnp.where(chunk == n_chunks - 1, 0.0, xn_ref[...])
    xpad_ref[pl.ds(0, SUBLANES), :] = prev
    xpad_ref[pl.ds(SUBLANES, tc), :] = x_ref[...]
    xpad_ref[pl.ds(SUBLANES + tc, SUBLANES), :] = nxt
    left = CONV_W // 2
    xc = cb_ref[...]
    for tap in range(CONV_W):
        xc = xc + xpad_ref[pl.ds(SUBLANES - left + tap, tc), :] * cw_ref[pl.ds(tap, 1), :]

    z = jnp.dot(xc.astype(MXU_DTYPE), wg_ref[...], preferred_element_type=jnp.float32)
    r = jax.nn.sigmoid(z[:, :LRU_BLOCK] + ba_ref[...])
    i = jax.nn.sigmoid(z[:, LRU_BLOCK:] + bx_ref[...])
    log_a = (-LRU_C * r) * _softplus(-lam_ref[...])
    a = jnp.exp(log_a)
    u = jnp.sqrt(1.0 - a * a) * (i * xc)

    pitch = _lru_pitch(tc)
    slots = LRU_GROUPS * SUBLANES * pitch
    a_ref[pl.ds(0, tc), :] = a
    u_ref[pl.ds(0, tc), :] = u
    a_ref[pl.ds(tc, slots - tc), :] = jnp.ones((slots - tc, LRU_BLOCK), jnp.float32)
    u_ref[pl.ds(tc, slots - tc), :] = jnp.zeros((slots - tc, LRU_BLOCK), jnp.float32)

    def seg_rows(g, k):
        return pl.ds(g * SUBLANES * pitch + k, SUBLANES, stride=pitch)

    def slot_of(kk):
        return (pitch - 1 - kk) if reverse else kk

    def scan_step(kk, carry):
        hs, ps = carry
        k = slot_of(kk)
        new_h, new_p = [], []
        for g in range(LRU_GROUPS):
            a_k = a_ref[seg_rows(g, k), :]
            h = a_k * hs[g] + u_ref[seg_rows(g, k), :]
            h_ref[seg_rows(g, k), :] = h
            new_h.append(h)
            new_p.append(a_k * ps[g])
        return tuple(new_h), tuple(new_p)

    zeros = tuple(jnp.zeros((SUBLANES, LANES), jnp.float32) for _ in range(LRU_GROUPS))
    ones = tuple(jnp.ones((SUBLANES, LANES), jnp.float32) for _ in range(LRU_GROUPS))
    h_end, p_end = lax.fori_loop(0, pitch, scan_step, (zeros, ones), unroll=LRU_UNROLL)

    row = lax.broadcasted_iota(jnp.int32, (SUBLANES, LANES), 0)
    c_in = [None] * LRU_GROUPS
    cg = carry_ref[...]
    for g in (reversed(range(LRU_GROUPS)) if reverse else range(LRU_GROUPS)):
        pa, hb = p_end[g], h_end[g]
        for s in (1, 2, 4):
            keep = (row < SUBLANES - s) if reverse else (row >= s)
            shift = (SUBLANES - s) if reverse else s
            a_sh = jnp.where(keep, pltpu.roll(pa, shift, 0), 1.0)
            b_sh = jnp.where(keep, pltpu.roll(hb, shift, 0), 0.0)
            hb = pa * b_sh + hb
            pa = pa * a_sh
        seg_out = pa * cg + hb
        if reverse:
            c_in[g] = jnp.where(row == SUBLANES - 1, cg, pltpu.roll(seg_out, SUBLANES - 1, 0))
            cg = seg_out[0:1, :]
        else:
            c_in[g] = jnp.where(row == 0, cg, pltpu.roll(seg_out, 1, 0))
            cg = seg_out[SUBLANES - 1:SUBLANES, :]
    carry_ref[...] = cg

    def fix_step(kk, ps):
        k = slot_of(kk)
        new_p = []
        for g in range(LRU_GROUPS):
            p = a_ref[seg_rows(g, k), :] * ps[g]
            u_ref[seg_rows(g, k), :] = h_ref[seg_rows(g, k), :] + p * c_in[g]
            new_p.append(p)
        return tuple(new_p)

    lax.fori_loop(0, pitch, fix_step, ones, unroll=LRU_UNROLL)

    h_all = u_ref[pl.ds(0, tc), :]
    if reverse:
        o_ref[...] = (jax.nn.gelu(gate_ref[...]) * (hf_ref[...] + h_all)).astype(o_ref.dtype)
    else:
        o_ref[...] = h_all


def _lru_direction(proj3, conv_w, conv_b, wg, ba, bx, lam, hf, *, reverse, d_rnn):
    b, s, _ = proj3.shape
    n_cb = d_rnn // LRU_BLOCK
    tc = min(LRU_TC, s)
    n_chunks = s // tc
    tc8 = tc // SUBLANES
    s8 = s // SUBLANES
    slots = LRU_GROUPS * SUBLANES * _lru_pitch(tc)

    def chunk_of(t):
        return (n_chunks - 1 - t) if reverse else t

    vec_spec = pl.BlockSpec((1, LRU_BLOCK), lambda bi, c, t: (0, c))
    in_specs = [
        pl.BlockSpec((None, tc, LRU_BLOCK), lambda bi, c, t: (bi, chunk_of(t), c)),
        pl.BlockSpec((None, SUBLANES, LRU_BLOCK),
                     lambda bi, c, t: (bi, jnp.maximum(chunk_of(t) * tc8 - 1, 0), c)),
        pl.BlockSpec((None, SUBLANES, LRU_BLOCK),
                     lambda bi, c, t: (bi, jnp.minimum((chunk_of(t) + 1) * tc8, s8 - 1), c)),
        pl.BlockSpec((CONV_W, LRU_BLOCK), lambda bi, c, t: (0, c)),
        vec_spec,
        pl.BlockSpec((None, LRU_BLOCK, 2 * LRU_BLOCK), lambda bi, c, t: (c, 0, 0)),
        vec_spec, vec_spec, vec_spec,
    ]
    args = [proj3, proj3, proj3, conv_w, conv_b.reshape(1, d_rnn), wg,
            ba.reshape(1, d_rnn), bx.reshape(1, d_rnn), lam.reshape(1, d_rnn)]
    if reverse:
        in_specs += [
            pl.BlockSpec((None, tc, LRU_BLOCK), lambda bi, c, t: (bi, chunk_of(t), c)),
            pl.BlockSpec((None, tc, LRU_BLOCK), lambda bi, c, t: (bi, chunk_of(t), n_cb + c)),
        ]
        args += [hf, proj3]
        out_dtype = MXU_DTYPE
    else:
        out_dtype = jnp.float32
    return pl.pallas_call(
        functools.partial(_lru_kernel, reverse=reverse, n_chunks=n_chunks, tc=tc),
        grid=(b, n_cb, n_chunks),
        in_specs=in_specs,
        out_specs=pl.BlockSpec((None, tc, LRU_BLOCK), lambda bi, c, t: (bi, chunk_of(t), c)),
        out_shape=jax.ShapeDtypeStruct((b, s, d_rnn), out_dtype),
        scratch_shapes=[
            pltpu.VMEM((tc + 2 * SUBLANES, LRU_BLOCK), jnp.float32),
            pltpu.VMEM((slots, LRU_BLOCK), jnp.float32),
            pltpu.VMEM((slots, LRU_BLOCK), jnp.float32),
            pltpu.VMEM((slots, LRU_BLOCK), jnp.float32),
            pltpu.VMEM((1, LRU_BLOCK), jnp.float32),
        ],
        compiler_params=_cparams("parallel", "parallel", "arbitrary"),
    )(*args)


def _attention_bias_table(rpb):
    n_heads, n_dr, n_dc = rpb.shape
    cq = np.arange(GRID_W)
    cs = np.clip(cq - WIN_C // 2, 0, GRID_W - WIN_C)
    ck = np.arange(GRID_W)
    valid = (ck[None, :] >= cs[:, None]) & (ck[None, :] < cs[:, None] + WIN_C)
    pad = GRID_W - WIN_C
    width = n_dc + 2 * pad + 1
    padded = jnp.pad(rpb.astype(jnp.float32), ((0, 0), (0, 0), (pad, pad + 1)))
    tiled = jnp.tile(padded, (1, 1, GRID_W))[:, :, :GRID_W * (width - 1)]
    skew = tiled.reshape(n_heads, n_dr, GRID_W, width - 1)
    toep = skew[:, :, :, GRID_W - 1:2 * GRID_W - 1]
    toep = jnp.where(valid[None, None], toep, MASK_VALUE)
    toep = toep.transpose(0, 2, 1, 3)
    per_off = [toep[:, :, WIN_R - 1 - off:2 * WIN_R - 1 - off] for off in range(WIN_R)]
    tab = jnp.stack(per_off, axis=1)
    return tab.reshape(n_heads, WIN_R, GRID_W, WIN_R * GRID_W)


def _attention_kernel(q_ref, k_ref, v_ref, tb_ref, o_ref, kb_ref, vb_ref, *, rows, tile_rows):
    kb_ref[...] = k_ref[...].astype(MXU_DTYPE)
    vb_ref[...] = v_ref[...].astype(MXU_DTYPE)
    scale = HEAD_DIM ** -0.5
    band = WIN_R * GRID_W

    def tile_body(i, _):
        starts, scores = [], []
        for rl in range(tile_rows):
            r = i * tile_rows + rl
            rs = jnp.clip(r - WIN_R // 2, 0, rows - WIN_R)
            start = pl.multiple_of(rs * GRID_W, GRID_W)
            q0 = pl.multiple_of(r * GRID_W, GRID_W)
            q = (q_ref[pl.ds(q0, GRID_W), :] * scale).astype(MXU_DTYPE)
            kband = kb_ref[pl.ds(start, band), :]
            s = lax.dot_general(q, kband, (((1,), (1,)), ((), ())), preferred_element_type=jnp.float32)
            starts.append(start)
            scores.append(s + tb_ref[r - rs])
        probs, denoms = [], []
        for s in scores:
            m = jnp.max(s, axis=-1, keepdims=True)
            p = jnp.exp(s - m)
            denoms.append(jnp.sum(p, axis=-1, keepdims=True))
            probs.append(p.astype(MXU_DTYPE))
        for rl in range(tile_rows):
            vband = vb_ref[pl.ds(starts[rl], band), :]
            o = jnp.dot(probs[rl], vband, preferred_element_type=jnp.float32)
            q0 = pl.multiple_of((i * tile_rows + rl) * GRID_W, GRID_W)
            o_ref[pl.ds(q0, GRID_W), :] = (o / denoms[rl]).astype(o_ref.dtype)
        return 0

    lax.fori_loop(0, rows // tile_rows, tile_body, 0)


def _attention(proj3, table, *, d_rnn, d_attn):
    b, s, _ = proj3.shape
    rows = s // GRID_W
    n_heads = d_attn // HEAD_DIM
    q_blk = 2 * d_rnn // HEAD_DIM
    k_blk = q_blk + n_heads
    v_blk = k_blk + n_heads
    tile_rows = min(ATT_ROWS, rows)
    assert rows % tile_rows == 0 and rows >= WIN_R
    return pl.pallas_call(
        functools.partial(_attention_kernel, rows=rows, tile_rows=tile_rows),
        grid=(b, n_heads),
        in_specs=[
            pl.BlockSpec((None, s, HEAD_DIM), lambda bi, h: (bi, 0, q_blk + h)),
            pl.BlockSpec((None, s, HEAD_DIM), lambda bi, h: (bi, 0, k_blk + h)),
            pl.BlockSpec((None, s, HEAD_DIM), lambda bi, h: (bi, 0, v_blk + h)),
            pl.BlockSpec((None, WIN_R, GRID_W, WIN_R * GRID_W), lambda bi, h: (h, 0, 0, 0)),
        ],
        out_specs=pl.BlockSpec((None, s, HEAD_DIM), lambda bi, h: (bi, 0, h)),
        out_shape=jax.ShapeDtypeStruct((b, s, d_attn), MXU_DTYPE),
        scratch_shapes=[
            pltpu.VMEM((s, HEAD_DIM), MXU_DTYPE),
            pltpu.VMEM((s, HEAD_DIM), MXU_DTYPE),
        ],
        compiler_params=_cparams("parallel", "parallel"),
    )(proj3, proj3, proj3, table)


def _weight_spec(d, tn):
    if tn == d:
        return pl.BlockSpec((d, tn), lambda i, j: (0, j), pipeline_mode=pl.Buffered(1))
    return pl.BlockSpec((d, tn), lambda i, j: (0, j))


def _merge_kernel(ya_ref, att_ref, wl_ref, wa_ref, gl_ref, ga_ref, bm_ref, o_ref):
    y_lru = jnp.dot(ya_ref[...], wl_ref[...], preferred_element_type=jnp.float32)
    y_att = jnp.dot(att_ref[...], wa_ref[...], preferred_element_type=jnp.float32)
    s_lru = jax.nn.sigmoid(gl_ref[...] + bm_ref[pl.ds(0, 1), :])
    s_att = jax.nn.sigmoid(ga_ref[...] + bm_ref[pl.ds(1, 1), :])
    o_ref[...] = (s_lru * y_lru + s_att * y_att).astype(o_ref.dtype)


def _merge(ya2, att2, wl, wa, proj2, b_merge, *, gate_col):
    n, d = ya2.shape
    tm = min(MERGE_TM, n)
    tn = min(MERGE_TN, d)
    gl_blk = gate_col // tn
    ga_blk = (gate_col + d) // tn
    return pl.pallas_call(
        _merge_kernel,
        grid=(n // tm, d // tn),
        in_specs=[
            pl.BlockSpec((tm, d), lambda i, j: (i, 0)),
            pl.BlockSpec((tm, d), lambda i, j: (i, 0)),
            _weight_spec(d, tn),
            _weight_spec(d, tn),
            pl.BlockSpec((tm, tn), lambda i, j: (i, gl_blk + j)),
            pl.BlockSpec((tm, tn), lambda i, j: (i, ga_blk + j)),
            pl.BlockSpec((2, tn), lambda i, j: (0, j)),
        ],
        out_specs=pl.BlockSpec((tm, tn), lambda i, j: (i, j)),
        out_shape=jax.ShapeDtypeStruct((n, d), MXU_DTYPE),
        compiler_params=_cparams("parallel", "arbitrary"),
    )(ya2, att2, wl, wa, proj2, proj2, b_merge)


def _out_proj_kernel(m_ref, w_ref, x_ref, o_ref):
    o_ref[...] = x_ref[...] + jnp.dot(m_ref[...], w_ref[...], preferred_element_type=jnp.float32)


def _out_proj(m2, w_o, x2):
    n, d = m2.shape
    tm = min(OUT_PROJ_TM, n)
    tn = min(OUT_PROJ_TN, d)
    return pl.pallas_call(
        _out_proj_kernel,
        grid=(n // tm, d // tn),
        in_specs=[
            pl.BlockSpec((tm, d), lambda i, j: (i, 0)),
            _weight_spec(d, tn),
            pl.BlockSpec((tm, tn), lambda i, j: (i, j)),
        ],
        out_specs=pl.BlockSpec((tm, tn), lambda i, j: (i, j)),
        out_shape=jax.ShapeDtypeStruct((n, d), jnp.float32),
        compiler_params=_cparams("parallel", "arbitrary"),
    )(m2, w_o, x2)


GROUP_LANE0 = N_EXPERTS
META_E, META_W, META_RANK = 0, 2, 4
GROUP_SHIFT = EXPERTS_PER_GROUP.bit_length() - 1
TOP_K_SHIFT = TOP_K.bit_length() - 1
assert 1 << GROUP_SHIFT == EXPERTS_PER_GROUP and 1 << TOP_K_SHIFT == TOP_K


def _router_kernel(ha_ref, hb_ref, g_ref, whi_ref, wlo_ref, b_ref, xn_ref, meta_ref, cnt_ref, *, steps_a):
    step = pl.program_id(0)

    @pl.when(step == 0)
    def _():
        cnt_ref[...] = jnp.zeros_like(cnt_ref)

    x = jnp.where(step < steps_a, ha_ref[...], hb_ref[...])
    tm = x.shape[0]
    ms = jnp.mean(x * x, axis=-1, keepdims=True)
    xn = (x * lax.rsqrt(ms + EPS)) * g_ref[...]
    xn_ref[...] = xn
    x_hi = xn.astype(MXU_DTYPE)
    x_lo = (xn - x_hi.astype(jnp.float32)).astype(MXU_DTYPE)
    logits = jnp.dot(x_hi, whi_ref[...], preferred_element_type=jnp.float32)
    logits = logits + (jnp.dot(x_hi, wlo_ref[...], preferred_element_type=jnp.float32)
                       + jnp.dot(x_lo, whi_ref[...], preferred_element_type=jnp.float32))
    logits = logits + b_ref[...]
    lane = lax.broadcasted_iota(jnp.int32, (tm, LANES), 1)
    neg_inf = -jnp.inf

    def first_argmax(vals, vmax):
        return jnp.min(jnp.where(vals == vmax, lane, LANES), axis=-1, keepdims=True)

    is_group = (lane >= GROUP_LANE0) & (lane < GROUP_LANE0 + N_GROUPS)
    lg = jnp.where(is_group, logits, neg_inf)
    mg = jnp.max(lg, axis=-1, keepdims=True)
    g_idx = first_argmax(lg, mg) - GROUP_LANE0
    g_val = 1.0 / jnp.sum(jnp.exp(lg - mg), axis=-1, keepdims=True)

    in_group = (lane < N_EXPERTS) & (lax.shift_right_logical(lane, GROUP_SHIFT) == g_idx)
    le = jnp.where(in_group, logits, neg_inf)
    m1 = jnp.max(le, axis=-1, keepdims=True)
    e1 = first_argmax(le, m1)
    le2 = jnp.where(lane == e1, neg_inf, le)
    m2 = jnp.max(le2, axis=-1, keepdims=True)
    e2 = first_argmax(le2, m2)
    z = jnp.sum(jnp.exp(le - m1), axis=-1, keepdims=True)
    p1 = 1.0 / z
    p2 = jnp.exp(m2 - m1) / z
    den = p1 + p2
    w1 = g_val * (p1 / den)
    w2 = g_val * (p2 / den)

    hot1 = lane == e1
    hot2 = lane == e2
    hot = hot1.astype(jnp.float32) + hot2.astype(jnp.float32)
    r_i = lax.broadcasted_iota(jnp.int32, (tm, tm), 0)
    c_i = lax.broadcasted_iota(jnp.int32, (tm, tm), 1)
    lower = (c_i < r_i).astype(MXU_DTYPE)
    before = jnp.dot(lower, hot.astype(MXU_DTYPE), preferred_element_type=jnp.float32) + cnt_ref[...]
    rank1 = jnp.sum(jnp.where(hot1, before, 0.0), axis=-1, keepdims=True)
    rank2 = jnp.sum(jnp.where(hot2, before, 0.0), axis=-1, keepdims=True)
    cnt_ref[...] += jnp.sum(hot, axis=0, keepdims=True)

    meta = jnp.where(lane == META_E, e1.astype(jnp.float32), 0.0)
    meta = jnp.where(lane == META_E + 1, e2.astype(jnp.float32), meta)
    meta = jnp.where(lane == META_W, w1, meta)
    meta = jnp.where(lane == META_W + 1, w2, meta)
    meta = jnp.where(lane == META_RANK, rank1, meta)
    meta = jnp.where(lane == META_RANK + 1, rank2, meta)
    meta_ref[...] = meta


def _router(h_a, h_b, g, w_hi, w_lo, b_router):
    n_a, d = h_a.shape
    n_b = h_b.shape[0]
    n = n_a + n_b
    tm = min(ROUTER_TM, n_a, n_b)
    steps_a = n_a // tm
    return pl.pallas_call(
        functools.partial(_router_kernel, steps_a=steps_a),
        grid=(n // tm,),
        in_specs=[
            pl.BlockSpec((tm, d), lambda i: (jnp.minimum(i, steps_a - 1), 0)),
            pl.BlockSpec((tm, d), lambda i: (jnp.maximum(i - steps_a, 0), 0)),
            pl.BlockSpec((1, d), lambda i: (0, 0)),
            pl.BlockSpec((d, LANES), lambda i: (0, 0)),
            pl.BlockSpec((d, LANES), lambda i: (0, 0)),
            pl.BlockSpec((1, LANES), lambda i: (0, 0)),
        ],
        out_specs=[
            pl.BlockSpec((tm, d), lambda i: (i, 0)),
            pl.BlockSpec((tm, LANES), lambda i: (i, 0)),
            pl.BlockSpec((1, LANES), lambda i: (0, 0)),
        ],
        out_shape=[
            jax.ShapeDtypeStruct((n, d), jnp.float32),
            jax.ShapeDtypeStruct((n, LANES), jnp.float32),
            jax.ShapeDtypeStruct((1, LANES), jnp.float32),
        ],
        compiler_params=_cparams("arbitrary"),
    )(h_a, h_b, g.reshape(1, d), w_hi, w_lo, b_router)


def _experts_kernel(pos_ref, start_ref, count_ref, rows_ref, nused_ref, x_hbm, wg_ref, wu_ref, wd_ref, y_hbm,
                    tok_ref, xbuf, obuf, gsem, osem, *, bm, n_blocks):
    e = pl.program_id(0)
    n_used = nused_ref[0]
    b0 = start_ref[e]

    def start_gather(blk, slot):
        base = blk * bm
        for r in range(bm):
            tok = tok_ref[base + r]
            pltpu.make_async_copy(x_hbm.at[pl.ds(tok, 1), :], xbuf.at[slot, pl.ds(r, 1), :],
                                  gsem.at[slot]).start()

    def wait_gather(slot):
        pltpu.make_async_copy(x_hbm.at[pl.ds(0, bm), :], xbuf.at[slot], gsem.at[slot]).wait()

    def out_copy(blk, slot):
        return pltpu.make_async_copy(obuf.at[slot], y_hbm.at[pl.ds(blk * bm, bm), :], osem.at[slot])

    @pl.when(e == 0)
    def _():
        n_assign = pos_ref.shape[0]

        def clear_padding(ex, _):
            def clear(i, _):
                tok_ref[i] = 0
                return 0
            row0 = start_ref[ex] * bm
            lax.fori_loop(row0 + rows_ref[ex], row0 + count_ref[ex] * bm, clear, 0)
            return 0

        def invert(i8, _):
            for k in range(SUBLANES):
                a = i8 * SUBLANES + k
                tok_ref[pos_ref[a]] = lax.shift_right_logical(a, TOP_K_SHIFT)
            return 0

        lax.fori_loop(0, pl.num_programs(0), clear_padding, 0)
        lax.fori_loop(0, n_assign // SUBLANES, invert, 0)
        start_gather(0, 0)

        @pl.when(n_used > 1)
        def _():
            start_gather(1, 1)

    def block_body(j, _):
        b = b0 + j
        slot = lax.bitwise_and(b, 1)
        gslot = lax.rem(b, GATHER_SLOTS)

        @pl.when(b + GATHER_SLOTS - 1 < n_used)
        def _():
            start_gather(b + GATHER_SLOTS - 1, lax.rem(b + GATHER_SLOTS - 1, GATHER_SLOTS))

        wait_gather(gslot)

        @pl.when(b >= 2)
        def _():
            out_copy(b - 2, slot).wait()

        x = xbuf[gslot].astype(MXU_DTYPE)
        hg = jnp.dot(x, wg_ref[...].astype(MXU_DTYPE), preferred_element_type=jnp.float32)
        hu = jnp.dot(x, wu_ref[...].astype(MXU_DTYPE), preferred_element_type=jnp.float32)
        hid = (jax.nn.silu(hg) * hu).astype(MXU_DTYPE)
        obuf[slot] = jnp.dot(hid, wd_ref[...].astype(MXU_DTYPE), preferred_element_type=jnp.float32)
        out_copy(b, slot).start()
        return 0

    lax.fori_loop(0, count_ref[e], block_body, 0)

    @pl.when(e == pl.num_programs(0) - 1)
    def _():
        @pl.when(n_used >= 2)
        def _():
            out_copy(n_used - 2, n_used % 2).wait()

        out_copy(n_used - 1, (n_used - 1) % 2).wait()
        obuf[0] = jnp.zeros(obuf.shape[1:], jnp.float32)

        def zero_start(b, _):
            out_copy(b, 0).start()
            return 0

        def zero_wait(b, _):
            out_copy(b, 0).wait()
            return 0

        lax.fori_loop(n_used, n_blocks, zero_start, 0)
        lax.fori_loop(n_used, n_blocks, zero_wait, 0)


def _experts(xn2, pos, blk_start, blk_count, exp_rows, n_used, w_gate, w_up, w_down, *, n_blocks):
    n, d = xn2.shape
    n_experts, _, f = w_gate.shape
    bm = MOE_BM
    grid_spec = pltpu.PrefetchScalarGridSpec(
        num_scalar_prefetch=5,
        grid=(n_experts,),
        in_specs=[
            pl.BlockSpec(memory_space=pl.ANY),
            pl.BlockSpec((None, d, f), lambda e, *_: (e, 0, 0)),
            pl.BlockSpec((None, d, f), lambda e, *_: (e, 0, 0)),
            pl.BlockSpec((None, f, d), lambda e, *_: (e, 0, 0)),
        ],
        out_specs=pl.BlockSpec(memory_space=pl.ANY),
        scratch_shapes=[
            pltpu.SMEM((n_blocks * bm,), jnp.int32),
            pltpu.VMEM((GATHER_SLOTS, bm, d), jnp.float32),
            pltpu.VMEM((2, bm, d), jnp.float32),
            pltpu.SemaphoreType.DMA((GATHER_SLOTS,)),
            pltpu.SemaphoreType.DMA((2,)),
        ],
    )
    return pl.pallas_call(
        functools.partial(_experts_kernel, bm=bm, n_blocks=n_blocks),
        grid_spec=grid_spec,
        out_shape=jax.ShapeDtypeStruct((n_blocks * bm, d), jnp.float32),
        compiler_params=_gather_cparams(EXPERTS_VMEM_LIMIT_BYTES),
    )(pos, blk_start, blk_count, exp_rows, n_used, xn2, w_gate, w_up, w_down)


def _combine_kernel(pos_ref, y_hbm, h_ref, meta_ref, g_ref, o_ref, ybuf, sem, *, tm, n_steps, blk0):
    step = pl.program_id(0)

    def start_gather(blk, slot):
        base = (blk0 + blk) * tm * TOP_K
        for r in range(tm):
            for k in range(TOP_K):
                pos = pos_ref[base + r * TOP_K + k]
                pltpu.make_async_copy(y_hbm.at[pl.ds(pos, 1), :], ybuf.at[slot, k, pl.ds(r, 1), :],
                                      sem.at[slot]).start()

    def wait_gather(slot):
        for k in range(TOP_K):
            pltpu.make_async_copy(y_hbm.at[pl.ds(0, tm), :], ybuf.at[slot, k], sem.at[slot]).wait()

    slot = step % 2

    @pl.when(step == 0)
    def _():
        start_gather(0, 0)

    @pl.when(step + 1 < n_steps)
    def _():
        start_gather(step + 1, 1 - slot)

    wait_gather(slot)
    w1 = meta_ref[:, META_W:META_W + 1]
    w2 = meta_ref[:, META_W + 1:META_W + 2]
    h = h_ref[...] + (ybuf[slot, 0] * w1 + ybuf[slot, 1] * w2)
    ms = jnp.mean(h * h, axis=-1, keepdims=True)
    o_ref[...] = (h * lax.rsqrt(ms + EPS)) * g_ref[...]


def _combine(pos, yb, h2, meta, g, row0):
    n, d = h2.shape
    tm = min(COMBINE_TM, n)
    n_steps = n // tm
    assert row0 % tm == 0
    blk0 = row0 // tm
    grid_spec = pltpu.PrefetchScalarGridSpec(
        num_scalar_prefetch=1,
        grid=(n_steps,),
        in_specs=[
            pl.BlockSpec(memory_space=pl.ANY),
            pl.BlockSpec((tm, d), lambda i, pos: (i, 0)),
            pl.BlockSpec((tm, LANES), lambda i, pos: (blk0 + i, 0)),
            pl.BlockSpec((1, d), lambda i, pos: (0, 0)),
        ],
        out_specs=pl.BlockSpec((tm, d), lambda i, pos: (i, 0)),
        scratch_shapes=[
            pltpu.VMEM((2, TOP_K, tm, d), jnp.float32),
            pltpu.SemaphoreType.DMA((2,)),
        ],
    )
    return pl.pallas_call(
        functools.partial(_combine_kernel, tm=tm, n_steps=n_steps, blk0=blk0),
        grid_spec=grid_spec,
        out_shape=jax.ShapeDtypeStruct((n, d), jnp.float32),
        compiler_params=_gather_cparams(VMEM_LIMIT_BYTES),
    )(pos, yb, h2, meta, g.reshape(1, d))


def _dispatch_plan(meta, counts_f, n):
    bm = MOE_BM
    a = n * TOP_K
    expert = meta[:, META_E:META_E + TOP_K].astype(jnp.int32)
    rank = meta[:, META_RANK:META_RANK + TOP_K].astype(jnp.int32)
    counts = counts_f[0, :N_EXPERTS].astype(jnp.int32)
    padded = (counts + bm - 1) // bm * bm
    pad_end = jnp.cumsum(padded)
    pad_start = pad_end - padded
    pos = (pad_start[expert] + rank).reshape(a)
    n_blocks = -(-a // bm) + N_EXPERTS
    blk_start = (pad_start // bm).astype(jnp.int32)
    blk_count = (padded // bm).astype(jnp.int32)
    n_used = (pad_end[-1] // bm).astype(jnp.int32).reshape(1)
    return pos, blk_start, blk_count, counts, n_used, n_blocks


def _mixer_residual(x, p):
    b, s, d = x.shape
    n = b * s
    d_rnn = p["d_rnn"]
    d_attn = p["d_attn"]
    x2 = x.reshape(n, d)
    proj2 = _norm_proj(x2, p["norm_mix_g"], p["w_in"])
    d_in = proj2.shape[1]
    proj3 = proj2.reshape(b, s, d_in)
    hf = _lru_direction(proj3, p["conv_w"], p["conv_b"], p["lru_wg"][0], p["lru_b_a"][0], p["lru_b_x"][0],
                        p["lru_lambda"][0], None, reverse=False, d_rnn=d_rnn)
    ya = _lru_direction(proj3, p["conv_w"], p["conv_b"], p["lru_wg"][1], p["lru_b_a"][1], p["lru_b_x"][1],
                        p["lru_lambda"][1], hf, reverse=True, d_rnn=d_rnn)
    att = _attention(proj3, p["att_table"], d_rnn=d_rnn, d_attn=d_attn)
    m2 = _merge(ya.reshape(n, d_rnn), att.reshape(n, d_attn), p["w_lru_up"], p["w_attn_up"], proj2,
                p["b_merge"], gate_col=2 * d_rnn + 3 * d_attn)
    return _out_proj(m2, p["w_o"], x2)


def _moe_and_final_norm(h_a, h_b, p):
    n_a, n_b = h_a.shape[0], h_b.shape[0]
    xn2, meta, counts = _router(h_a, h_b, p["norm_ffn_g"], p["w_router_hi"], p["w_router_lo"], p["b_router"])
    pos, blk_start, blk_count, exp_rows, n_used, n_blocks = _dispatch_plan(meta, counts, n_a + n_b)
    yb = _experts(xn2, pos, blk_start, blk_count, exp_rows, n_used, p["w_exp_gate"], p["w_exp_up"], p["w_exp_down"],
                  n_blocks=n_blocks)
    y_a = _combine(pos, yb, h_a, meta, p["norm_final_g"], 0)
    y_b = _combine(pos, yb, h_b, meta, p["norm_final_g"], n_a)
    return y_a, y_b


def _prepare_params(norm_mix_g, w_in, conv_w, conv_b, lru_w_a, lru_b_a, lru_w_x, lru_b_x, lru_lambda, rpb,
                    w_lru_up, w_attn_up, b_merge, w_o, norm_ffn_g, w_router_group, b_router_group,
                    w_router_expert, b_router_expert, w_exp_gate, w_exp_up, w_exp_down, norm_final_g):
    bf16 = MXU_DTYPE
    d = w_in.shape[1]
    d_rnn = conv_w.shape[2]
    d_attn = w_attn_up.shape[1]
    lru_wg = jnp.concatenate([lru_w_a[0], lru_w_x[0]], axis=-1).astype(bf16)
    w_router = jnp.zeros((d, LANES), jnp.float32)
    w_router = w_router.at[:, :N_EXPERTS].set(w_router_expert[0])
    w_router = w_router.at[:, GROUP_LANE0:GROUP_LANE0 + N_GROUPS].set(w_router_group[0])
    w_router_hi = w_router.astype(bf16)
    w_router_lo = (w_router - w_router_hi.astype(jnp.float32)).astype(bf16)
    b_router = jnp.zeros((1, LANES), jnp.float32)
    b_router = b_router.at[0, :N_EXPERTS].set(b_router_expert[0])
    b_router = b_router.at[0, GROUP_LANE0:GROUP_LANE0 + N_GROUPS].set(b_router_group[0])
    return dict(
        d_rnn=d_rnn, d_attn=d_attn,
        norm_mix_g=norm_mix_g[0], w_in=w_in[0].astype(bf16),
        conv_w=conv_w[0], conv_b=conv_b[0], lru_wg=lru_wg,
        lru_b_a=lru_b_a[0], lru_b_x=lru_b_x[0], lru_lambda=lru_lambda[0],
        att_table=_attention_bias_table(rpb[0]),
        w_lru_up=w_lru_up[0].astype(bf16), w_attn_up=w_attn_up[0].astype(bf16),
        b_merge=b_merge[0], w_o=w_o[0].astype(bf16),
        norm_ffn_g=norm_ffn_g[0], w_router_hi=w_router_hi, w_router_lo=w_router_lo, b_router=b_router,
        w_exp_gate=w_exp_gate[0], w_exp_up=w_exp_up[0], w_exp_down=w_exp_down[0],
        norm_final_g=norm_final_g,
    )


def kernel(x_prompt, x_sample, norm_mix_g, w_in, conv_w, conv_b, lru_w_a, lru_b_a, lru_w_x, lru_b_x, lru_lambda, rpb, w_lru_up, w_attn_up, b_merge, w_o, norm_ffn_g, w_router_group, b_router_group, w_router_expert, b_router_expert, w_exp_gate, w_exp_up, w_exp_down, norm_final_g):
    assert norm_mix_g.shape[0] == 1, "single-layer encoder"
    p = _prepare_params(norm_mix_g, w_in, conv_w, conv_b, lru_w_a, lru_b_a, lru_w_x, lru_b_x, lru_lambda, rpb,
                        w_lru_up, w_attn_up, b_merge, w_o, norm_ffn_g, w_router_group, b_router_group,
                        w_router_expert, b_router_expert, w_exp_gate, w_exp_up, w_exp_down, norm_final_g)
    h_prompt = _mixer_residual(x_prompt, p)
    h_sample = _mixer_residual(x_sample, p)
    y_prompt, y_sample = _moe_and_final_norm(h_prompt, h_sample, p)
    return (y_prompt.reshape(x_prompt.shape), y_sample.reshape(x_sample.shape))
```

```python
import functools

import numpy as np
import jax
import jax.numpy as jnp
from jax import lax
from jax.experimental import pallas as pl
from jax.experimental.pallas import tpu as pltpu

GRID_W = 64
WIN_R = 8
WIN_C = 16
CONV_W = 4
LRU_C = 8.0
LRU_BLOCK = 128
HEAD_DIM = 128
N_GROUPS = 4
EXPERTS_PER_GROUP = 8
N_EXPERTS = N_GROUPS * EXPERTS_PER_GROUP
TOP_K = 2
EPS = 1e-6
MASK_VALUE = -1e30

MXU_DTYPE = jnp.bfloat16

LANES = 128
SUBLANES = 8
VMEM_LIMIT_BYTES = 56 * 1024 * 1024
EXPERTS_VMEM_LIMIT_BYTES = 62 * 1024 * 1024

PROJ_TM = 1024
PROJ_TN = 2048
MERGE_TM = 512
MERGE_TN = 2048
OUT_PROJ_TM = 512
OUT_PROJ_TN = 2048
LRU_TC = 8192
LRU_GROUPS = 4
LRU_UNROLL = 4
ATT_ROWS = 32
ROUTER_TM = 512
MOE_BM = 256
COMBINE_TM = 256
GATHER_SLOTS = 3


def _cparams(*sem):
    return pltpu.CompilerParams(dimension_semantics=sem, vmem_limit_bytes=VMEM_LIMIT_BYTES)


def _gather_cparams(vmem_limit):
    return pltpu.CompilerParams(dimension_semantics=("arbitrary",), vmem_limit_bytes=vmem_limit)


def _norm_proj_kernel(x_ref, g_ref, w_ref, o_ref, xn_ref):
    @pl.when(pl.program_id(1) == 0)
    def _():
        x = x_ref[...]
        ms = jnp.mean(x * x, axis=-1, keepdims=True)
        xn_ref[...] = ((x * lax.rsqrt(ms + EPS)) * g_ref[...]).astype(MXU_DTYPE)

    o_ref[...] = jnp.dot(xn_ref[...], w_ref[...], preferred_element_type=jnp.float32)


def _norm_proj(x2, g, w_bf16):
    n, d = x2.shape
    d_out = w_bf16.shape[1]
    tm = min(PROJ_TM, n)
    tn = min(PROJ_TN, d_out)
    return pl.pallas_call(
        _norm_proj_kernel,
        grid=(n // tm, d_out // tn),
        in_specs=[
            pl.BlockSpec((tm, d), lambda i, j: (i, 0)),
            pl.BlockSpec((1, d), lambda i, j: (0, 0)),
            pl.BlockSpec((d, tn), lambda i, j: (0, j)),
        ],
        out_specs=pl.BlockSpec((tm, tn), lambda i, j: (i, j)),
        out_shape=jax.ShapeDtypeStruct((n, d_out), jnp.float32),
        scratch_shapes=[pltpu.VMEM((tm, d), MXU_DTYPE)],
        compiler_params=_cparams("parallel", "arbitrary"),
    )(x2, g.reshape(1, d), w_bf16)


def _softplus(y):
    return jnp.maximum(y, 0.0) + jnp.log1p(jnp.exp(-jnp.abs(y)))


def _lru_pitch(tc):
    seg_len = tc // (LRU_GROUPS * SUBLANES)
    assert seg_len * LRU_GROUPS * SUBLANES == tc and seg_len % SUBLANES == 0
    return seg_len + SUBLANES // 2


def _lru_kernel(*refs, reverse, n_chunks, tc):
    if reverse:
        (x_ref, xp_ref, xn_ref, cw_ref, cb_ref, wg_ref, ba_ref, bx_ref, lam_ref,
         hf_ref, gate_ref, o_ref, xpad_ref, a_ref, u_ref, h_ref, carry_ref) = refs
    else:
        (x_ref, xp_ref, xn_ref, cw_ref, cb_ref, wg_ref, ba_ref, bx_ref, lam_ref,
         o_ref, xpad_ref, a_ref, u_ref, h_ref, carry_ref) = refs
    t = pl.program_id(2)
    chunk = (n_chunks - 1 - t) if reverse else t

    @pl.when(t == 0)
    def _():
        carry_ref[...] = jnp.zeros_like(carry_ref)

    prev = jnp.where(chunk == 0, 0.0, xp_ref[...])
    nxt = jnp.where(chunk == n_chunks - 1, 0.0, xn_ref[...])
    xpad_ref[pl.ds(0, SUBLANES), :] = prev
    xpad_ref[pl.ds(SUBLANES, tc), :] = x_ref[...]
    xpad_ref[pl.ds(SUBLANES + tc, SUBLANES), :] = nxt
    left = CONV_W // 2
    xc = cb_ref[...]
    for tap in range(CONV_W):
        xc = xc + xpad_ref[pl.ds(SUBLANES - left + tap, tc), :] * cw_ref[pl.ds(tap, 1), :]

    z = jnp.dot(xc.astype(MXU_DTYPE), wg_ref[...], preferred_element_type=jnp.float32)
    r = jax.nn.sigmoid(z[:, :LRU_BLOCK] + ba_ref[...])
    i = jax.nn.sigmoid(z[:, LRU_BLOCK:] + bx_ref[...])
    log_a = (-LRU_C * r) * _softplus(-lam_ref[...])
    a = jnp.exp(log_a)
    u = jnp.sqrt(1.0 - a * a) * (i * xc)

    pitch = _lru_pitch(tc)
    slots = LRU_GROUPS * SUBLANES * pitch
    a_ref[pl.ds(0, tc), :] = a
    u_ref[pl.ds(0, tc), :] = u
    a_ref[pl.ds(tc, slots - tc), :] = jnp.ones((slots - tc, LRU_BLOCK), jnp.float32)
    u_ref[pl.ds(tc, slots - tc), :] = jnp.zeros((slots - tc, LRU_BLOCK), jnp.float32)

    def seg_rows(g, k):
        return pl.ds(g * SUBLANES * pitch + k, SUBLANES, stride=pitch)

    def slot_of(kk):
        return (pitch - 1 - kk) if reverse else kk

    def scan_step(kk, carry):
        hs, ps = carry
        k = slot_of(kk)
        new_h, new_p = [], []
        for g in range(LRU_GROUPS):
            a_k = a_ref[seg_rows(g, k), :]
            h = a_k * hs[g] + u_ref[seg_rows(g, k), :]
            h_ref[seg_rows(g, k), :] = h
            new_h.append(h)
            new_p.append(a_k * ps[g])
        return tuple(new_h), tuple(new_p)

    zeros = tuple(jnp.zeros((SUBLANES, LANES), jnp.float32) for _ in range(LRU_GROUPS))
    ones = tuple(jnp.ones((SUBLANES, LANES), jnp.float32) for _ in range(LRU_GROUPS))
    h_end, p_end = lax.fori_loop(0, pitch, scan_step, (zeros, ones), unroll=LRU_UNROLL)

    row = lax.broadcasted_iota(jnp.int32, (SUBLANES, LANES), 0)
    c_in = [None] * LRU_GROUPS
    cg = carry_ref[...]
    for g in (reversed(range(LRU_GROUPS)) if reverse else range(LRU_GROUPS)):
        pa, hb = p_end[g], h_end[g]
        for s in (1, 2, 4):
            keep = (row < SUBLANES - s) if reverse else (row >= s)
            shift = (SUBLANES - s) if reverse else s
            a_sh = jnp.where(keep, pltpu.roll(pa, shift, 0), 1.0)
            b_sh = jnp.where(keep, pltpu.roll(hb, shift, 0), 0.0)
            hb = pa * b_sh + hb
            pa = pa * a_sh
        seg_out = pa * cg + hb
        if reverse:
            c_in[g] = jnp.where(row == SUBLANES - 1, cg, pltpu.roll(seg_out, SUBLANES - 1, 0))
            cg = seg_out[0:1, :]
        else:
            c_in[g] = jnp.where(row == 0, cg, pltpu.roll(seg_out, 1, 0))
            cg = seg_out[SUBLANES - 1:SUBLANES, :]
    carry_ref[...] = cg

    def fix_step(kk, ps):
        k = slot_of(kk)
        new_p = []
        for g in range(LRU_GROUPS):
            p = a_ref[seg_rows(g, k), :] * ps[g]
            u_ref[seg_rows(g, k), :] = h_ref[seg_rows(g, k), :] + p * c_in[g]
            new_p.append(p)
        return tuple(new_p)

    lax.fori_loop(0, pitch, fix_step, ones, unroll=LRU_UNROLL)

    h_all = u_ref[pl.ds(0, tc), :]
    if reverse:
        o_ref[...] = (jax.nn.gelu(gate_ref[...]) * (hf_ref[...] + h_all)).astype(o_ref.dtype)
    else:
        o_ref[...] = h_all


def _lru_direction(proj3, conv_w, conv_b, wg, ba, bx, lam, hf, *, reverse, d_rnn):
    b, s, _ = proj3.shape
    n_cb = d_rnn // LRU_BLOCK
    tc = min(LRU_TC, s)
    n_chunks = s // tc
    tc8 = tc // SUBLANES
    s8 = s // SUBLANES
    slots = LRU_GROUPS * SUBLANES * _lru_pitch(tc)

    def chunk_of(t):
        return (n_chunks - 1 - t) if reverse else t

    vec_spec = pl.BlockSpec((1, LRU_BLOCK), lambda bi, c, t: (0, c))
    in_specs = [
        pl.BlockSpec((None, tc, LRU_BLOCK), lambda bi, c, t: (bi, chunk_of(t), c)),
        pl.BlockSpec((None, SUBLANES, LRU_BLOCK),
                     lambda bi, c, t: (bi, jnp.maximum(chunk_of(t) * tc8 - 1, 0), c)),
        pl.BlockSpec((None, SUBLANES, LRU_BLOCK),
                     lambda bi, c, t: (bi, jnp.minimum((chunk_of(t) + 1) * tc8, s8 - 1), c)),
        pl.BlockSpec((CONV_W, LRU_BLOCK), lambda bi, c, t: (0, c)),
        vec_spec,
        pl.BlockSpec((None, LRU_BLOCK, 2 * LRU_BLOCK), lambda bi, c, t: (c, 0, 0)),
        vec_spec, vec_spec, vec_spec,
    ]
    args = [proj3, proj3, proj3, conv_w, conv_b.reshape(1, d_rnn), wg,
            ba.reshape(1, d_rnn), bx.reshape(1, d_rnn), lam.reshape(1, d_rnn)]
    if reverse:
        in_specs += [
            pl.BlockSpec((None, tc, LRU_BLOCK), lambda bi, c, t: (bi, chunk_of(t), c)),
            pl.BlockSpec((None, tc, LRU_BLOCK), lambda bi, c, t: (bi, chunk_of(t), n_cb + c)),
        ]
        args += [hf, proj3]
        out_dtype = MXU_DTYPE
    else:
        out_dtype = jnp.float32
    return pl.pallas_call(
        functools.partial(_lru_kernel, reverse=reverse, n_chunks=n_chunks, tc=tc),
        grid=(b, n_cb, n_chunks),
        in_specs=in_specs,
        out_specs=pl.BlockSpec((None, tc, LRU_BLOCK), lambda bi, c, t: (bi, chunk_of(t), c)),
        out_shape=jax.ShapeDtypeStruct((b, s, d_rnn), out_dtype),
        scratch_shapes=[
            pltpu.VMEM((tc + 2 * SUBLANES, LRU_BLOCK), jnp.float32),
            pltpu.VMEM((slots, LRU_BLOCK), jnp.float32),
            pltpu.VMEM((slots, LRU_BLOCK), jnp.float32),
            pltpu.VMEM((slots, LRU_BLOCK), jnp.float32),
            pltpu.VMEM((1, LRU_BLOCK), jnp.float32),
        ],
        compiler_params=_cparams("parallel", "parallel", "arbitrary"),
    )(*args)


def _attention_bias_table(rpb):
    n_heads, n_dr, n_dc = rpb.shape
    cq = np.arange(GRID_W)
    cs = np.clip(cq - WIN_C // 2, 0, GRID_W - WIN_C)
    ck = np.arange(GRID_W)
    valid = (ck[None, :] >= cs[:, None]) & (ck[None, :] < cs[:, None] + WIN_C)
    pad = GRID_W - WIN_C
    width = n_dc + 2 * pad + 1
    padded = jnp.pad(rpb.astype(jnp.float32), ((0, 0), (0, 0), (pad, pad + 1)))
    tiled = jnp.tile(padded, (1, 1, GRID_W))[:, :, :GRID_W * (width - 1)]
    skew = tiled.reshape(n_heads, n_dr, GRID_W, width - 1)
    toep = skew[:, :, :, GRID_W - 1:2 * GRID_W - 1]
    toep = jnp.where(valid[None, None], toep, MASK_VALUE)
    toep = toep.transpose(0, 2, 1, 3)
    per_off = [toep[:, :, WIN_R - 1 - off:2 * WIN_R - 1 - off] for off in range(WIN_R)]
    tab = jnp.stack(per_off, axis=1)
    return tab.reshape(n_heads, WIN_R, GRID_W, WIN_R * GRID_W)


def _attention_kernel(q_ref, k_ref, v_ref, tb_ref, o_ref, kb_ref, vb_ref, *, rows, tile_rows):
    kb_ref[...] = k_ref[...].astype(MXU_DTYPE)
    vb_ref[...] = v_ref[...].astype(MXU_DTYPE)
    scale = HEAD_DIM ** -0.5
    band = WIN_R * GRID_W

    def tile_body(i, _):
        starts, scores = [], []
        for rl in range(tile_rows):
            r = i * tile_rows + rl
            rs = jnp.clip(r - WIN_R // 2, 0, rows - WIN_R)
            start = pl.multiple_of(rs * GRID_W, GRID_W)
            q0 = pl.multiple_of(r * GRID_W, GRID_W)
            q = (q_ref[pl.ds(q0, GRID_W), :] * scale).astype(MXU_DTYPE)
            kband = kb_ref[pl.ds(start, band), :]
            s = lax.dot_general(q, kband, (((1,), (1,)), ((), ())), preferred_element_type=jnp.float32)
            starts.append(start)
            scores.append(s + tb_ref[r - rs])
        probs, denoms = [], []
        for s in scores:
            m = jnp.max(s, axis=-1, keepdims=True)
            p = jnp.exp(s - m)
            denoms.append(jnp.sum(p, axis=-1, keepdims=True))
            probs.append(p.astype(MXU_DTYPE))
        for rl in range(tile_rows):
            vband = vb_ref[pl.ds(starts[rl], band), :]
            o = jnp.dot(probs[rl], vband, preferred_element_type=jnp.float32)
            q0 = pl.multiple_of((i * tile_rows + rl) * GRID_W, GRID_W)
            o_ref[pl.ds(q0, GRID_W), :] = (o / denoms[rl]).astype(o_ref.dtype)
        return 0

    lax.fori_loop(0, rows // tile_rows, tile_body, 0)


def _attention(proj3, table, *, d_rnn, d_attn):
    b, s, _ = proj3.shape
    rows = s // GRID_W
    n_heads = d_attn // HEAD_DIM
    q_blk = 2 * d_rnn // HEAD_DIM
    k_blk = q_blk + n_heads
    v_blk = k_blk + n_heads
    tile_rows = min(ATT_ROWS, rows)
    assert rows % tile_rows == 0 and rows >= WIN_R
    return pl.pallas_call(
        functools.partial(_attention_kernel, rows=rows, tile_rows=tile_rows),
        grid=(b, n_heads),
        in_specs=[
            pl.BlockSpec((None, s, HEAD_DIM), lambda bi, h: (bi, 0, q_blk + h)),
            pl.BlockSpec((None, s, HEAD_DIM), lambda bi, h: (bi, 0, k_blk + h)),
            pl.BlockSpec((None, s, HEAD_DIM), lambda bi, h: (bi, 0, v_blk + h)),
            pl.BlockSpec((None, WIN_R, GRID_W, WIN_R * GRID_W), lambda bi, h: (h, 0, 0, 0)),
        ],
        out_specs=pl.BlockSpec((None, s, HEAD_DIM), lambda bi, h: (bi, 0, h)),
        out_shape=jax.ShapeDtypeStruct((b, s, d_attn), MXU_DTYPE),
        scratch_shapes=[
            pltpu.VMEM((s, HEAD_DIM), MXU_DTYPE),
            pltpu.VMEM((s, HEAD_DIM), MXU_DTYPE),
        ],
        compiler_params=_cparams("parallel", "parallel"),
    )(proj3, proj3, proj3, table)


def _weight_spec(d, tn):
    if tn == d:
        return pl.BlockSpec((d, tn), lambda i, j: (0, j), pipeline_mode=pl.Buffered(1))
    return pl.BlockSpec((d, tn), lambda i, j: (0, j))


def _merge_kernel(ya_ref, att_ref, wl_ref, wa_ref, gl_ref, ga_ref, bm_ref, o_ref):
    y_lru = jnp.dot(ya_ref[...], wl_ref[...], preferred_element_type=jnp.float32)
    y_att = jnp.dot(att_ref[...], wa_ref[...], preferred_element_type=jnp.float32)
    s_lru = jax.nn.sigmoid(gl_ref[...] + bm_ref[pl.ds(0, 1), :])
    s_att = jax.nn.sigmoid(ga_ref[...] + bm_ref[pl.ds(1, 1), :])
    o_ref[...] = (s_lru * y_lru + s_att * y_att).astype(o_ref.dtype)


def _merge(ya2, att2, wl, wa, proj2, b_merge, *, gate_col):
    n, d = ya2.shape
    tm = min(MERGE_TM, n)
    tn = min(MERGE_TN, d)
    gl_blk = gate_col // tn
    ga_blk = (gate_col + d) // tn
    return pl.pallas_call(
        _merge_kernel,
        grid=(n // tm, d // tn),
        in_specs=[
            pl.BlockSpec((tm, d), lambda i, j: (i, 0)),
            pl.BlockSpec((tm, d), lambda i, j: (i, 0)),
            _weight_spec(d, tn),
            _weight_spec(d, tn),
            pl.BlockSpec((tm, tn), lambda i, j: (i, gl_blk + j)),
            pl.BlockSpec((tm, tn), lambda i, j: (i, ga_blk + j)),
            pl.BlockSpec((2, tn), lambda i, j: (0, j)),
        ],
        out_specs=pl.BlockSpec((tm, tn), lambda i, j: (i, j)),
        out_shape=jax.ShapeDtypeStruct((n, d), MXU_DTYPE),
        compiler_params=_cparams("parallel", "arbitrary"),
    )(ya2, att2, wl, wa, proj2, proj2, b_merge)


def _out_proj_kernel(m_ref, w_ref, x_ref, o_ref):
    o_ref[...] = x_ref[...] + jnp.dot(m_ref[...], w_ref[...], preferred_element_type=jnp.float32)


def _out_proj(m2, w_o, x2):
    n, d = m2.shape
    tm = min(OUT_PROJ_TM, n)
    tn = min(OUT_PROJ_TN, d)
    return pl.pallas_call(
        _out_proj_kernel,
        grid=(n // tm, d // tn),
        in_specs=[
            pl.BlockSpec((tm, d), lambda i, j: (i, 0)),
            _weight_spec(d, tn),
            pl.BlockSpec((tm, tn), lambda i, j: (i, j)),
        ],
        out_specs=pl.BlockSpec((tm, tn), lambda i, j: (i, j)),
        out_shape=jax.ShapeDtypeStruct((n, d), jnp.float32),
        compiler_params=_cparams("parallel", "arbitrary"),
    )(m2, w_o, x2)


GROUP_LANE0 = N_EXPERTS
META_E, META_W, META_RANK = 0, 2, 4
GROUP_SHIFT = EXPERTS_PER_GROUP.bit_length() - 1
TOP_K_SHIFT = TOP_K.bit_length() - 1
assert 1 << GROUP_SHIFT == EXPERTS_PER_GROUP and 1 << TOP_K_SHIFT == TOP_K


def _router_kernel(ha_ref, hb_ref, g_ref, whi_ref, wlo_ref, b_ref, xn_ref, meta_ref, cnt_ref, *, steps_a):
    step = pl.program_id(0)

    @pl.when(step == 0)
    def _():
        cnt_ref[...] = jnp.zeros_like(cnt_ref)

    x = jnp.where(step < steps_a, ha_ref[...], hb_ref[...])
    tm = x.shape[0]
    ms = jnp.mean(x * x, axis=-1, keepdims=True)
    xn = (x * lax.rsqrt(ms + EPS)) * g_ref[...]
    xn_ref[...] = xn
    x_hi = xn.astype(MXU_DTYPE)
    x_lo = (xn - x_hi.astype(jnp.float32)).astype(MXU_DTYPE)
    logits = jnp.dot(x_hi, whi_ref[...], preferred_element_type=jnp.float32)
    logits = logits + (jnp.dot(x_hi, wlo_ref[...], preferred_element_type=jnp.float32)
                       + jnp.dot(x_lo, whi_ref[...], preferred_element_type=jnp.float32))
    logits = logits + b_ref[...]
    lane = lax.broadcasted_iota(jnp.int32, (tm, LANES), 1)
    neg_inf = -jnp.inf

    def first_argmax(vals, vmax):
        return jnp.min(jnp.where(vals == vmax, lane, LANES), axis=-1, keepdims=True)

    is_group = (lane >= GROUP_LANE0) & (lane < GROUP_LANE0 + N_GROUPS)
    lg = jnp.where(is_group, logits, neg_inf)
    mg = jnp.max(lg, axis=-1, keepdims=True)
    g_idx = first_argmax(lg, mg) - GROUP_LANE0
    g_val = 1.0 / jnp.sum(jnp.exp(lg - mg), axis=-1, keepdims=True)

    in_group = (lane < N_EXPERTS) & (lax.shift_right_logical(lane, GROUP_SHIFT) == g_idx)
    le = jnp.where(in_group, logits, neg_inf)
    m1 = jnp.max(le, axis=-1, keepdims=True)
    e1 = first_argmax(le, m1)
    le2 = jnp.where(lane == e1, neg_inf, le)
    m2 = jnp.max(le2, axis=-1, keepdims=True)
    e2 = first_argmax(le2, m2)
    z = jnp.sum(jnp.exp(le - m1), axis=-1, keepdims=True)
    p1 = 1.0 / z
    p2 = jnp.exp(m2 - m1) / z
    den = p1 + p2
    w1 = g_val * (p1 / den)
    w2 = g_val * (p2 / den)

    hot1 = lane == e1
    hot2 = lane == e2
    hot = hot1.astype(jnp.float32) + hot2.astype(jnp.float32)
    r_i = lax.broadcasted_iota(jnp.int32, (tm, tm), 0)
    c_i = lax.broadcasted_iota(jnp.int32, (tm, tm), 1)
    lower = (c_i < r_i).astype(MXU_DTYPE)
    before = jnp.dot(lower, hot.astype(MXU_DTYPE), preferred_element_type=jnp.float32) + cnt_ref[...]
    rank1 = jnp.sum(jnp.where(hot1, before, 0.0), axis=-1, keepdims=True)
    rank2 = jnp.sum(jnp.where(hot2, before, 0.0), axis=-1, keepdims=True)
    cnt_ref[...] += jnp.sum(hot, axis=0, keepdims=True)

    meta = jnp.where(lane == META_E, e1.astype(jnp.float32), 0.0)
    meta = jnp.where(lane == META_E + 1, e2.astype(jnp.float32), meta)
    meta = jnp.where(lane == META_W, w1, meta)
    meta = jnp.where(lane == META_W + 1, w2, meta)
    meta = jnp.where(lane == META_RANK, rank1, meta)
    meta = jnp.where(lane == META_RANK + 1, rank2, meta)
    meta_ref[...] = meta


def _router(h_a, h_b, g, w_hi, w_lo, b_router):
    n_a, d = h_a.shape
    n_b = h_b.shape[0]
    n = n_a + n_b
    tm = min(ROUTER_TM, n_a, n_b)
    steps_a = n_a // tm
    return pl.pallas_call(
        functools.partial(_router_kernel, steps_a=steps_a),
        grid=(n // tm,),
        in_specs=[
            pl.BlockSpec((tm, d), lambda i: (jnp.minimum(i, steps_a - 1), 0)),
            pl.BlockSpec((tm, d), lambda i: (jnp.maximum(i - steps_a, 0), 0)),
            pl.BlockSpec((1, d), lambda i: (0, 0)),
            pl.BlockSpec((d, LANES), lambda i: (0, 0)),
            pl.BlockSpec((d, LANES), lambda i: (0, 0)),
            pl.BlockSpec((1, LANES), lambda i: (0, 0)),
        ],
        out_specs=[
            pl.BlockSpec((tm, d), lambda i: (i, 0)),
            pl.BlockSpec((tm, LANES), lambda i: (i, 0)),
            pl.BlockSpec((1, LANES), lambda i: (0, 0)),
        ],
        out_shape=[
            jax.ShapeDtypeStruct((n, d), jnp.float32),
            jax.ShapeDtypeStruct((n, LANES), jnp.float32),
            jax.ShapeDtypeStruct((1, LANES), jnp.float32),
        ],
        compiler_params=_cparams("arbitrary"),
    )(h_a, h_b, g.reshape(1, d), w_hi, w_lo, b_router)


def _experts_kernel(pos_ref, start_ref, count_ref, rows_ref, nused_ref, x_hbm, wg_ref, wu_ref, wd_ref, y_hbm,
                    tok_ref, xbuf, obuf, gsem, osem, *, bm, n_blocks):
    e = pl.program_id(0)
    n_used = nused_ref[0]
    b0 = start_ref[e]

    def start_gather(blk, slot):
        base = blk * bm
        for r in range(bm):
            tok = tok_ref[base + r]
            pltpu.make_async_copy(x_hbm.at[pl.ds(tok, 1), :], xbuf.at[slot, pl.ds(r, 1), :],
                                  gsem.at[slot]).start()

    def wait_gather(slot):
        pltpu.make_async_copy(x_hbm.at[pl.ds(0, bm), :], xbuf.at[slot], gsem.at[slot]).wait()

    def out_copy(blk, slot):
        return pltpu.make_async_copy(obuf.at[slot], y_hbm.at[pl.ds(blk * bm, bm), :], osem.at[slot])

    @pl.when(e == 0)
    def _():
        n_assign = pos_ref.shape[0]

        def clear_padding(ex, _):
            def clear(i, _):
                tok_ref[i] = 0
                return 0
            row0 = start_ref[ex] * bm
            lax.fori_loop(row0 + rows_ref[ex], row0 + count_ref[ex] * bm, clear, 0)
            return 0

        def invert(i8, _):
            for k in range(SUBLANES):
                a = i8 * SUBLANES + k
                tok_ref[pos_ref[a]] = lax.shift_right_logical(a, TOP_K_SHIFT)
            return 0

        lax.fori_loop(0, pl.num_programs(0), clear_padding, 0)
        lax.fori_loop(0, n_assign // SUBLANES, invert, 0)
        start_gather(0, 0)

        @pl.when(n_used > 1)
        def _():
            start_gather(1, 1)

    def block_body(j, _):
        b = b0 + j
        slot = lax.bitwise_and(b, 1)
        gslot = lax.rem(b, GATHER_SLOTS)

        @pl.when(b + GATHER_SLOTS - 1 < n_used)
        def _():
            start_gather(b + GATHER_SLOTS - 1, lax.rem(b + GATHER_SLOTS - 1, GATHER_SLOTS))

        wait_gather(gslot)

        @pl.when(b >= 2)
        def _():
            out_copy(b - 2, slot).wait()

        x = xbuf[gslot].astype(MXU_DTYPE)
        hg = jnp.dot(x, wg_ref[...].astype(MXU_DTYPE), preferred_element_type=jnp.float32)
        hu = jnp.dot(x, wu_ref[...].astype(MXU_DTYPE), preferred_element_type=jnp.float32)
        hid = (jax.nn.silu(hg) * hu).astype(MXU_DTYPE)
        obuf[slot] = jnp.dot(hid, wd_ref[...].astype(MXU_DTYPE), preferred_element_type=jnp.float32)
        out_copy(b, slot).start()
        return 0

    lax.fori_loop(0, count_ref[e], block_body, 0)

    @pl.when(e == pl.num_programs(0) - 1)
    def _():
        @pl.when(n_used >= 2)
        def _():
            out_copy(n_used - 2, n_used % 2).wait()

        out_copy(n_used - 1, (n_used - 1) % 2).wait()
        obuf[0] = jnp.zeros(obuf.shape[1:], jnp.float32)

        def zero_start(b, _):
            out_copy(b, 0).start()
            return 0

        def zero_wait(b, _):
            out_copy(b, 0).wait()
            return 0

        lax.fori_loop(n_used, n_blocks, zero_start, 0)
        lax.fori_loop(n_used, n_blocks, zero_wait, 0)


def _experts(xn2, pos, blk_start, blk_count, exp_rows, n_used, w_gate, w_up, w_down, *, n_blocks):
    n, d = xn2.shape
    n_experts, _, f = w_gate.shape
    bm = MOE_BM
    grid_spec = pltpu.PrefetchScalarGridSpec(
        num_scalar_prefetch=5,
        grid=(n_experts,),
        in_specs=[
            pl.BlockSpec(memory_space=pl.ANY),
            pl.BlockSpec((None, d, f), lambda e, *_: (e, 0, 0)),
            pl.BlockSpec((None, d, f), lambda e, *_: (e, 0, 0)),
            pl.BlockSpec((None, f, d), lambda e, *_: (e, 0, 0)),
        ],
        out_specs=pl.BlockSpec(memory_space=pl.ANY),
        scratch_shapes=[
            pltpu.SMEM((n_blocks * bm,), jnp.int32),
            pltpu.VMEM((GATHER_SLOTS, bm, d), jnp.float32),
            pltpu.VMEM((2, bm, d), jnp.float32),
            pltpu.SemaphoreType.DMA((GATHER_SLOTS,)),
            pltpu.SemaphoreType.DMA((2,)),
        ],
    )
    return pl.pallas_call(
        functools.partial(_experts_kernel, bm=bm, n_blocks=n_blocks),
        grid_spec=grid_spec,
        out_shape=jax.ShapeDtypeStruct((n_blocks * bm, d), jnp.float32),
        compiler_params=_gather_cparams(EXPERTS_VMEM_LIMIT_BYTES),
    )(pos, blk_start, blk_count, exp_rows, n_used, xn2, w_gate, w_up, w_down)


def _combine_kernel(pos_ref, y_hbm, h_ref, meta_ref, g_ref, o_ref, ybuf, sem, *, tm, n_steps, blk0):
    step = pl.program_id(0)

    def start_gather(blk, slot):
        base = (blk0 + blk) * tm * TOP_K
        for r in range(tm):
            for k in range(TOP_K):
                pos = pos_ref[base + r * TOP_K + k]
                pltpu.make_async_copy(y_hbm.at[pl.ds(pos, 1), :], ybuf.at[slot, k, pl.ds(r, 1), :],
                                      sem.at[slot]).start()

    def wait_gather(slot):
        for k in range(TOP_K):
            pltpu.make_async_copy(y_hbm.at[pl.ds(0, tm), :], ybuf.at[slot, k], sem.at[slot]).wait()

    slot = step % 2

    @pl.when(step == 0)
    def _():
        start_gather(0, 0)

    @pl.when(step + 1 < n_steps)
    def _():
        start_gather(step + 1, 1 - slot)

    wait_gather(slot)
    w1 = meta_ref[:, META_W:META_W + 1]
    w2 = meta_ref[:, META_W + 1:META_W + 2]
    h = h_ref[...] + (ybuf[slot, 0] * w1 + ybuf[slot, 1] * w2)
    ms = jnp.mean(h * h, axis=-1, keepdims=True)
    o_ref[...] = (h * lax.rsqrt(ms + EPS)) * g_ref[...]


def _combine(pos, yb, h2, meta, g, row0):
    n, d = h2.shape
    tm = min(COMBINE_TM, n)
    n_steps = n // tm
    assert row0 % tm == 0
    blk0 = row0 // tm
    grid_spec = pltpu.PrefetchScalarGridSpec(
        num_scalar_prefetch=1,
        grid=(n_steps,),
        in_specs=[
            pl.BlockSpec(memory_space=pl.ANY),
            pl.BlockSpec((tm, d), lambda i, pos: (i, 0)),
            pl.BlockSpec((tm, LANES), lambda i, pos: (blk0 + i, 0)),
            pl.BlockSpec((1, d), lambda i, pos: (0, 0)),
        ],
        out_specs=pl.BlockSpec((tm, d), lambda i, pos: (i, 0)),
        scratch_shapes=[
            pltpu.VMEM((2, TOP_K, tm, d), jnp.float32),
            pltpu.SemaphoreType.DMA((2,)),
        ],
    )
    return pl.pallas_call(
        functools.partial(_combine_kernel, tm=tm, n_steps=n_steps, blk0=blk0),
        grid_spec=grid_spec,
        out_shape=jax.ShapeDtypeStruct((n, d), jnp.float32),
        compiler_params=_gather_cparams(VMEM_LIMIT_BYTES),
    )(pos, yb, h2, meta, g.reshape(1, d))


def _dispatch_plan(meta, counts_f, n):
    bm = MOE_BM
    a = n * TOP_K
    expert = meta[:, META_E:META_E + TOP_K].astype(jnp.int32)
    rank = meta[:, META_RANK:META_RANK + TOP_K].astype(jnp.int32)
    counts = counts_f[0, :N_EXPERTS].astype(jnp.int32)
    padded = (counts + bm - 1) // bm * bm
    pad_end = jnp.cumsum(padded)
    pad_start = pad_end - padded
    pos = (pad_start[expert] + rank).reshape(a)
    n_blocks = -(-a // bm) + N_EXPERTS
    blk_start = (pad_start // bm).astype(jnp.int32)
    blk_count = (padded // bm).astype(jnp.int32)
    n_used = (pad_end[-1] // bm).astype(jnp.int32).reshape(1)
    return pos, blk_start, blk_count, counts, n_used, n_blocks


def _mixer_residual(x, p):
    b, s, d = x.shape
    n = b * s
    d_rnn = p["d_rnn"]
    d_attn = p["d_attn"]
    x2 = x.reshape(n, d)
    proj2 = _norm_proj(x2, p["norm_mix_g"], p["w_in"])
    d_in = proj2.shape[1]
    proj3 = proj2.reshape(b, s, d_in)
    hf = _lru_direction(proj3, p["conv_w"], p["conv_b"], p["lru_wg"][0], p["lru_b_a"][0], p["lru_b_x"][0],
                        p["lru_lambda"][0], None, reverse=False, d_rnn=d_rnn)
    ya = _lru_direction(proj3, p["conv_w"], p["conv_b"], p["lru_wg"][1], p["lru_b_a"][1], p["lru_b_x"][1],
                        p["lru_lambda"][1], hf, reverse=True, d_rnn=d_rnn)
    att = _attention(proj3, p["att_table"], d_rnn=d_rnn, d_attn=d_attn)
    m2 = _merge(ya.reshape(n, d_rnn), att.reshape(n, d_attn), p["w_lru_up"], p["w_attn_up"], proj2,
                p["b_merge"], gate_col=2 * d_rnn + 3 * d_attn)
    return _out_proj(m2, p["w_o"], x2)


def _moe_and_final_norm(h_a, h_b, p):
    n_a, n_b = h_a.shape[0], h_b.shape[0]
    xn2, meta, counts = _router(h_a, h_b, p["norm_ffn_g"], p["w_router_hi"], p["w_router_lo"], p["b_router"])
    pos, blk_start, blk_count, exp_rows, n_used, n_blocks = _dispatch_plan(meta, counts, n_a + n_b)
    yb = _experts(xn2, pos, blk_start, blk_count, exp_rows, n_used, p["w_exp_gate"], p["w_exp_up"], p["w_exp_down"],
                  n_blocks=n_blocks)
    y_a = _combine(pos, yb, h_a, meta, p["norm_final_g"], 0)
    y_b = _combine(pos, yb, h_b, meta, p["norm_final_g"], n_a)
    return y_a, y_b


def _prepare_params(norm_mix_g, w_in, conv_w, conv_b, lru_w_a, lru_b_a, lru_w_x, lru_b_x, lru_lambda, rpb,
                    w_lru_up, w_attn_up, b_merge, w_o, norm_ffn_g, w_router_group, b_router_group,
                    w_router_expert, b_router_expert, w_exp_gate, w_exp_up, w_exp_down, norm_final_g):
    bf16 = MXU_DTYPE
    d = w_in.shape[1]
    d_rnn = conv_w.shape[2]
    d_attn = w_attn_up.shape[1]
    lru_wg = jnp.concatenate([lru_w_a[0], lru_w_x[0]], axis=-1).astype(bf16)
    w_router = jnp.zeros((d, LANES), jnp.float32)
    w_router = w_router.at[:, :N_EXPERTS].set(w_router_expert[0])
    w_router = w_router.at[:, GROUP_LANE0:GROUP_LANE0 + N_GROUPS].set(w_router_group[0])
    w_router_hi = w_router.astype(bf16)
    w_router_lo = (w_router - w_router_hi.astype(jnp.float32)).astype(bf16)
    b_router = jnp.zeros((1, LANES), jnp.float32)
    b_router = b_router.at[0, :N_EXPERTS].set(b_router_expert[0])
    b_router = b_router.at[0, GROUP_LANE0:GROUP_LANE0 + N_GROUPS].set(b_router_group[0])
    return dict(
        d_rnn=d_rnn, d_attn=d_attn,
        norm_mix_g=norm_mix_g[0], w_in=w_in[0].astype(bf16),
        conv_w=conv_w[0], conv_b=conv_b[0], lru_wg=lru_wg,
        lru_b_a=lru_b_a[0], lru_b_x=lru_b_x[0], lru_lambda=lru_lambda[0],
        att_table=_attention_bias_table(rpb[0]),
        w_lru_up=w_lru_up[0].astype(bf16), w_attn_up=w_attn_up[0].astype(bf16),
        b_merge=b_merge[0], w_o=w_o[0].astype(bf16),
        norm_ffn_g=norm_ffn_g[0], w_router_hi=w_router_hi, w_router_lo=w_router_lo, b_router=b_router,
        w_exp_gate=w_exp_gate[0], w_exp_up=w_exp_up[0], w_exp_down=w_exp_down[0],
        norm_final_g=norm_final_g,
    )


def kernel(x_prompt, x_sample, norm_mix_g, w_in, conv_w, conv_b, lru_w_a, lru_b_a, lru_w_x, lru_b_x, lru_lambda, rpb, w_lru_up, w_attn_up, b_merge, w_o, norm_ffn_g, w_router_group, b_router_group, w_router_expert, b_router_expert, w_exp_gate, w_exp_up, w_exp_down, norm_final_g):
    assert norm_mix_g.shape[0] == 1, "single-layer encoder"
    p = _prepare_params(norm_mix_g, w_in, conv_w, conv_b, lru_w_a, lru_b_a, lru_w_x, lru_b_x, lru_lambda, rpb,
                        w_lru_up, w_attn_up, b_merge, w_o, norm_ffn_g, w_router_group, b_router_group,
                        w_router_expert, b_router_expert, w_exp_gate, w_exp_up, w_exp_down, norm_final_g)
    h_prompt = _mixer_residual(x_prompt, p)
    h_sample = _mixer_residual(x_sample, p)
    y_prompt, y_sample = _moe_and_final_norm(h_prompt, h_sample, p)
    return (y_prompt.reshape(x_prompt.shape), y_sample.reshape(x_sample.shape))
```

```python
import functools

import numpy as np
import jax
import jax.numpy as jnp
from jax import lax
from jax.experimental import pallas as pl
from jax.experimental.pallas import tpu as pltpu

GRID_W = 64
WIN_R = 8
WIN_C = 16
CONV_W = 4
LRU_C = 8.0
LRU_BLOCK = 128
HEAD_DIM = 128
N_GROUPS = 4
EXPERTS_PER_GROUP = 8
N_EXPERTS = N_GROUPS * EXPERTS_PER_GROUP
TOP_K = 2
EPS = 1e-6
MASK_VALUE = -1e30

MXU_DTYPE = jnp.bfloat16

LANES = 128
SUBLANES = 8
VMEM_LIMIT_BYTES = 56 * 1024 * 1024
EXPERTS_VMEM_LIMIT_BYTES = 62 * 1024 * 1024

PROJ_TM = 1024
PROJ_TN = 2048
MERGE_TM = 512
MERGE_TN = 2048
OUT_PROJ_TM = 1024
OUT_PROJ_TN = 2048
LRU_TC = 8192
LRU_GROUPS = 4
LRU_UNROLL = 4
ATT_ROWS = 64
ROUTER_TM = 512
MOE_BM = 256
COMBINE_TM = 256
GATHER_SLOTS = 3


def _cparams(*sem):
    return pltpu.CompilerParams(dimension_semantics=sem, vmem_limit_bytes=VMEM_LIMIT_BYTES)


def _gather_cparams(vmem_limit):
    return pltpu.CompilerParams(dimension_semantics=("arbitrary",), vmem_limit_bytes=vmem_limit)


def _norm_proj_kernel(x_ref, g_ref, w_ref, o_ref, xn_ref):
    @pl.when(pl.program_id(1) == 0)
    def _():
        x = x_ref[...]
        ms = jnp.mean(x * x, axis=-1, keepdims=True)
        xn_ref[...] = ((x * lax.rsqrt(ms + EPS)) * g_ref[...]).astype(MXU_DTYPE)

    o_ref[...] = jnp.dot(xn_ref[...], w_ref[...], preferred_element_type=jnp.float32)


def _norm_proj(x2, g, w_bf16):
    n, d = x2.shape
    d_out = w_bf16.shape[1]
    tm = min(PROJ_TM, n)
    tn = min(PROJ_TN, d_out)
    return pl.pallas_call(
        _norm_proj_kernel,
        grid=(n // tm, d_out // tn),
        in_specs=[
            pl.BlockSpec((tm, d), lambda i, j: (i, 0)),
            pl.BlockSpec((1, d), lambda i, j: (0, 0)),
            pl.BlockSpec((d, tn), lambda i, j: (0, j)),
        ],
        out_specs=pl.BlockSpec((tm, tn), lambda i, j: (i, j)),
        out_shape=jax.ShapeDtypeStruct((n, d_out), jnp.float32),
        scratch_shapes=[pltpu.VMEM((tm, d), MXU_DTYPE)],
        compiler_params=_cparams("parallel", "arbitrary"),
    )(x2, g.reshape(1, d), w_bf16)


def _softplus(y):
    return jnp.maximum(y, 0.0) + jnp.log1p(jnp.exp(-jnp.abs(y)))


def _lru_pitch(tc):
    seg_len = tc // (LRU_GROUPS * SUBLANES)
    assert seg_len * LRU_GROUPS * SUBLANES == tc and seg_len % SUBLANES == 0
    return seg_len + SUBLANES // 2


def _lru_kernel(*refs, reverse, n_chunks, tc):
    if reverse:
        (x_ref, xp_ref, xn_ref, cw_ref, cb_ref, wg_ref, ba_ref, bx_ref, lam_ref,
         hf_ref, gate_ref, o_ref, xpad_ref, a_ref, u_ref, h_ref, carry_ref) = refs
    else:
        (x_ref, xp_ref, xn_ref, cw_ref, cb_ref, wg_ref, ba_ref, bx_ref, lam_ref,
         o_ref, xpad_ref, a_ref, u_ref, h_ref, carry_ref) = refs
    t = pl.program_id(2)
    chunk = (n_chunks - 1 - t) if reverse else t

    @pl.when(t == 0)
    def _():
        carry_ref[...] = jnp.zeros_like(carry_ref)

    prev = jnp.where(chunk == 0, 0.0, xp_ref[...])
    nxt = jnp.where(chunk == n_chunks - 1, 0.0, xn_ref[...])
    xpad_ref[pl.ds(0, SUBLANES), :] = prev
    xpad_ref[pl.ds(SUBLANES, tc), :] = x_ref[...]
    xpad_ref[pl.ds(SUBLANES + tc, SUBLANES), :] = nxt
    left = CONV_W // 2
    xc = cb_ref[...]
    for tap in range(CONV_W):
        xc = xc + xpad_ref[pl.ds(SUBLANES - left + tap, tc), :] * cw_ref[pl.ds(tap, 1), :]

    z = jnp.dot(xc.astype(MXU_DTYPE), wg_ref[...], preferred_element_type=jnp.float32)
    r = jax.nn.sigmoid(z[:, :LRU_BLOCK] + ba_ref[...])
    i = jax.nn.sigmoid(z[:, LRU_BLOCK:] + bx_ref[...])
    log_a = (-LRU_C * r) * _softplus(-lam_ref[...])
    a = jnp.exp(log_a)
    u = jnp.sqrt(1.0 - a * a) * (i * xc)

    pitch = _lru_pitch(tc)
    slots = LRU_GROUPS * SUBLANES * pitch
    a_ref[pl.ds(0, tc), :] = a
    u_ref[pl.ds(0, tc), :] = u
    a_ref[pl.ds(tc, slots - tc), :] = jnp.ones((slots - tc, LRU_BLOCK), jnp.float32)
    u_ref[pl.ds(tc, slots - tc), :] = jnp.zeros((slots - tc, LRU_BLOCK), jnp.float32)

    def seg_rows(g, k):
        return pl.ds(g * SUBLANES * pitch + k, SUBLANES, stride=pitch)

    def slot_of(kk):
        return (pitch - 1 - kk) if reverse else kk

    def scan_step(kk, carry):
        hs, ps = carry
        k = slot_of(kk)
        new_h, new_p = [], []
        for g in range(LRU_GROUPS):
            a_k = a_ref[seg_rows(g, k), :]
            h = a_k * hs[g] + u_ref[seg_rows(g, k), :]
            h_ref[seg_rows(g, k), :] = h
            new_h.append(h)
            new_p.append(a_k * ps[g])
        return tuple(new_h), tuple(new_p)

    zeros = tuple(jnp.zeros((SUBLANES, LANES), jnp.float32) for _ in range(LRU_GROUPS))
    ones = tuple(jnp.ones((SUBLANES, LANES), jnp.float32) for _ in range(LRU_GROUPS))
    h_end, p_end = lax.fori_loop(0, pitch, scan_step, (zeros, ones), unroll=LRU_UNROLL)

    row = lax.broadcasted_iota(jnp.int32, (SUBLANES, LANES), 0)
    c_in = [None] * LRU_GROUPS
    cg = carry_ref[...]
    for g in (reversed(range(LRU_GROUPS)) if reverse else range(LRU_GROUPS)):
        pa, hb = p_end[g], h_end[g]
        for s in (1, 2, 4):
            keep = (row < SUBLANES - s) if reverse else (row >= s)
            shift = (SUBLANES - s) if reverse else s
            a_sh = jnp.where(keep, pltpu.roll(pa, shift, 0), 1.0)
            b_sh = jnp.where(keep, pltpu.roll(hb, shift, 0), 0.0)
            hb = pa * b_sh + hb
            pa = pa * a_sh
        seg_out = pa * cg + hb
        if reverse:
            c_in[g] = jnp.where(row == SUBLANES - 1, cg, pltpu.roll(seg_out, SUBLANES - 1, 0))
            cg = seg_out[0:1, :]
        else:
            c_in[g] = jnp.where(row == 0, cg, pltpu.roll(seg_out, 1, 0))
            cg = seg_out[SUBLANES - 1:SUBLANES, :]
    carry_ref[...] = cg

    def fix_step(kk, ps):
        k = slot_of(kk)
        new_p = []
        for g in range(LRU_GROUPS):
            p = a_ref[seg_rows(g, k), :] * ps[g]
            u_ref[seg_rows(g, k), :] = h_ref[seg_rows(g, k), :] + p * c_in[g]
            new_p.append(p)
        return tuple(new_p)

    lax.fori_loop(0, pitch, fix_step, ones, unroll=LRU_UNROLL)

    h_all = u_ref[pl.ds(0, tc), :]
    if reverse:
        o_ref[...] = (jax.nn.gelu(gate_ref[...]) * (hf_ref[...] + h_all)).astype(o_ref.dtype)
    else:
        o_ref[...] = h_all


def _lru_direction(proj3, conv_w, conv_b, wg, ba, bx, lam, hf, *, reverse, d_rnn):
    b, s, _ = proj3.shape
    n_cb = d_rnn // LRU_BLOCK
    tc = min(LRU_TC, s)
    n_chunks = s // tc
    tc8 = tc // SUBLANES
    s8 = s // SUBLANES
    slots = LRU_GROUPS * SUBLANES * _lru_pitch(tc)

    def chunk_of(t):
        return (n_chunks - 1 - t) if reverse else t

    vec_spec = pl.BlockSpec((1, LRU_BLOCK), lambda bi, c, t: (0, c))
    in_specs = [
        pl.BlockSpec((None, tc, LRU_BLOCK), lambda bi, c, t: (bi, chunk_of(t), c)),
        pl.BlockSpec((None, SUBLANES, LRU_BLOCK),
                     lambda bi, c, t: (bi, jnp.maximum(chunk_of(t) * tc8 - 1, 0), c)),
        pl.BlockSpec((None, SUBLANES, LRU_BLOCK),
                     lambda bi, c, t: (bi, jnp.minimum((chunk_of(t) + 1) * tc8, s8 - 1), c)),
        pl.BlockSpec((CONV_W, LRU_BLOCK), lambda bi, c, t: (0, c)),
        vec_spec,
        pl.BlockSpec((None, LRU_BLOCK, 2 * LRU_BLOCK), lambda bi, c, t: (c, 0, 0)),
        vec_spec, vec_spec, vec_spec,
    ]
    args = [proj3, proj3, proj3, conv_w, conv_b.reshape(1, d_rnn), wg,
            ba.reshape(1, d_rnn), bx.reshape(1, d_rnn), lam.reshape(1, d_rnn)]
    if reverse:
        in_specs += [
            pl.BlockSpec((None, tc, LRU_BLOCK), lambda bi, c, t: (bi, chunk_of(t), c)),
            pl.BlockSpec((None, tc, LRU_BLOCK), lambda bi, c, t: (bi, chunk_of(t), n_cb + c)),
        ]
        args += [hf, proj3]
        out_dtype = MXU_DTYPE
    else:
        out_dtype = jnp.float32
    return pl.pallas_call(
        functools.partial(_lru_kernel, reverse=reverse, n_chunks=n_chunks, tc=tc),
        grid=(b, n_cb, n_chunks),
        in_specs=in_specs,
        out_specs=pl.BlockSpec((None, tc, LRU_BLOCK), lambda bi, c, t: (bi, chunk_of(t), c)),
        out_shape=jax.ShapeDtypeStruct((b, s, d_rnn), out_dtype),
        scratch_shapes=[
            pltpu.VMEM((tc + 2 * SUBLANES, LRU_BLOCK), jnp.float32),
            pltpu.VMEM((slots, LRU_BLOCK), jnp.float32),
            pltpu.VMEM((slots, LRU_BLOCK), jnp.float32),
            pltpu.VMEM((slots, LRU_BLOCK), jnp.float32),
            pltpu.VMEM((1, LRU_BLOCK), jnp.float32),
        ],
        compiler_params=_cparams("parallel", "parallel", "arbitrary"),
    )(*args)


def _attention_bias_table(rpb):
    n_heads, n_dr, n_dc = rpb.shape
    cq = np.arange(GRID_W)
    cs = np.clip(cq - WIN_C // 2, 0, GRID_W - WIN_C)
    ck = np.arange(GRID_W)
    valid = (ck[None, :] >= cs[:, None]) & (ck[None, :] < cs[:, None] + WIN_C)
    pad = GRID_W - WIN_C
    width = n_dc + 2 * pad + 1
    padded = jnp.pad(rpb.astype(jnp.float32), ((0, 0), (0, 0), (pad, pad + 1)))
    tiled = jnp.tile(padded, (1, 1, GRID_W))[:, :, :GRID_W * (width - 1)]
    skew = tiled.reshape(n_heads, n_dr, GRID_W, width - 1)
    toep = skew[:, :, :, GRID_W - 1:2 * GRID_W - 1]
    toep = jnp.where(valid[None, None], toep, MASK_VALUE)
    toep = toep.transpose(0, 2, 1, 3)
    per_off = [toep[:, :, WIN_R - 1 - off:2 * WIN_R - 1 - off] for off in range(WIN_R)]
    tab = jnp.stack(per_off, axis=1)
    return tab.reshape(n_heads, WIN_R, GRID_W, WIN_R * GRID_W)


def _attention_kernel(q_ref, k_ref, v_ref, tb_ref, o_ref, kb_ref, vb_ref, *, rows, tile_rows):
    kb_ref[...] = k_ref[...].astype(MXU_DTYPE)
    vb_ref[...] = v_ref[...].astype(MXU_DTYPE)
    scale = HEAD_DIM ** -0.5
    band = WIN_R * GRID_W

    def tile_body(i, _):
        starts, scores = [], []
        for rl in range(tile_rows):
            r = i * tile_rows + rl
            rs = jnp.clip(r - WIN_R // 2, 0, rows - WIN_R)
            start = pl.multiple_of(rs * GRID_W, GRID_W)
            q0 = pl.multiple_of(r * GRID_W, GRID_W)
            q = (q_ref[pl.ds(q0, GRID_W), :] * scale).astype(MXU_DTYPE)
            kband = kb_ref[pl.ds(start, band), :]
            s = lax.dot_general(q, kband, (((1,), (1,)), ((), ())), preferred_element_type=jnp.float32)
            starts.append(start)
            scores.append(s + tb_ref[r - rs])
        probs, denoms = [], []
        for s in scores:
            m = jnp.max(s, axis=-1, keepdims=True)
            p = jnp.exp(s - m)
            denoms.append(jnp.sum(p, axis=-1, keepdims=True))
            probs.append(p.astype(MXU_DTYPE))
        for rl in range(tile_rows):
            vband = vb_ref[pl.ds(starts[rl], band), :]
            o = jnp.dot(probs[rl], vband, preferred_element_type=jnp.float32)
            q0 = pl.multiple_of((i * tile_rows + rl) * GRID_W, GRID_W)
            o_ref[pl.ds(q0, GRID_W), :] = (o / denoms[rl]).astype(o_ref.dtype)
        return 0

    lax.fori_loop(0, rows // tile_rows, tile_body, 0)


def _attention(proj3, table, *, d_rnn, d_attn):
    b, s, _ = proj3.shape
    rows = s // GRID_W
    n_heads = d_attn // HEAD_DIM
    q_blk = 2 * d_rnn // HEAD_DIM
    k_blk = q_blk + n_heads
    v_blk = k_blk + n_heads
    tile_rows = min(ATT_ROWS, rows)
    assert rows % tile_rows == 0 and rows >= WIN_R
    return pl.pallas_call(
        functools.partial(_attention_kernel, rows=rows, tile_rows=tile_rows),
        grid=(b, n_heads),
        in_specs=[
            pl.BlockSpec((None, s, HEAD_DIM), lambda bi, h: (bi, 0, q_blk + h)),
            pl.BlockSpec((None, s, HEAD_DIM), lambda bi, h: (bi, 0, k_blk + h)),
            pl.BlockSpec((None, s, HEAD_DIM), lambda bi, h: (bi, 0, v_blk + h)),
            pl.BlockSpec((None, WIN_R, GRID_W, WIN_R * GRID_W), lambda bi, h: (h, 0, 0, 0)),
        ],
        out_specs=pl.BlockSpec((None, s, HEAD_DIM), lambda bi, h: (bi, 0, h)),
        out_shape=jax.ShapeDtypeStruct((b, s, d_attn), MXU_DTYPE),
        scratch_shapes=[
            pltpu.VMEM((s, HEAD_DIM), MXU_DTYPE),
            pltpu.VMEM((s, HEAD_DIM), MXU_DTYPE),
        ],
        compiler_params=_cparams("parallel", "parallel"),
    )(proj3, proj3, proj3, table)


def _weight_spec(d, tn):
    if tn == d:
        return pl.BlockSpec((d, tn), lambda i, j: (0, j), pipeline_mode=pl.Buffered(1))
    return pl.BlockSpec((d, tn), lambda i, j: (0, j))


def _merge_kernel(ya_ref, att_ref, wl_ref, wa_ref, gl_ref, ga_ref, bm_ref, o_ref):
    y_lru = jnp.dot(ya_ref[...], wl_ref[...], preferred_element_type=jnp.float32)
    y_att = jnp.dot(att_ref[...], wa_ref[...], preferred_element_type=jnp.float32)
    s_lru = jax.nn.sigmoid(gl_ref[...] + bm_ref[pl.ds(0, 1), :])
    s_att = jax.nn.sigmoid(ga_ref[...] + bm_ref[pl.ds(1, 1), :])
    o_ref[...] = (s_lru * y_lru + s_att * y_att).astype(o_ref.dtype)


def _merge(ya2, att2, wl, wa, proj2, b_merge, *, gate_col):
    n, d = ya2.shape
    tm = min(MERGE_TM, n)
    tn = min(MERGE_TN, d)
    gl_blk = gate_col // tn
    ga_blk = (gate_col + d) // tn
    return pl.pallas_call(
        _merge_kernel,
        grid=(n // tm, d // tn),
        in_specs=[
            pl.BlockSpec((tm, d), lambda i, j: (i, 0)),
            pl.BlockSpec((tm, d), lambda i, j: (i, 0)),
            _weight_spec(d, tn),
            _weight_spec(d, tn),
            pl.BlockSpec((tm, tn), lambda i, j: (i, gl_blk + j)),
            pl.BlockSpec((tm, tn), lambda i, j: (i, ga_blk + j)),
            pl.BlockSpec((2, tn), lambda i, j: (0, j)),
        ],
        out_specs=pl.BlockSpec((tm, tn), lambda i, j: (i, j)),
        out_shape=jax.ShapeDtypeStruct((n, d), MXU_DTYPE),
        compiler_params=_cparams("parallel", "arbitrary"),
    )(ya2, att2, wl, wa, proj2, proj2, b_merge)


def _out_proj_kernel(m_ref, w_ref, x_ref, o_ref):
    o_ref[...] = x_ref[...] + jnp.dot(m_ref[...], w_ref[...], preferred_element_type=jnp.float32)


def _out_proj(m2, w_o, x2):
    n, d = m2.shape
    tm = min(OUT_PROJ_TM, n)
    tn = min(OUT_PROJ_TN, d)
    return pl.pallas_call(
        _out_proj_kernel,
        grid=(n // tm, d // tn),
        in_specs=[
            pl.BlockSpec((tm, d), lambda i, j: (i, 0)),
            _weight_spec(d, tn),
            pl.BlockSpec((tm, tn), lambda i, j: (i, j)),
        ],
        out_specs=pl.BlockSpec((tm, tn), lambda i, j: (i, j)),
        out_shape=jax.ShapeDtypeStruct((n, d), jnp.float32),
        compiler_params=_cparams("parallel", "arbitrary"),
    )(m2, w_o, x2)


GROUP_LANE0 = N_EXPERTS
META_E, META_W, META_RANK = 0, 2, 4
GROUP_SHIFT = EXPERTS_PER_GROUP.bit_length() - 1
TOP_K_SHIFT = TOP_K.bit_length() - 1
assert 1 << GROUP_SHIFT == EXPERTS_PER_GROUP and 1 << TOP_K_SHIFT == TOP_K


def _router_kernel(ha_ref, hb_ref, g_ref, whi_ref, wlo_ref, b_ref, xn_ref, meta_ref, cnt_ref, *, steps_a):
    step = pl.program_id(0)

    @pl.when(step == 0)
    def _():
        cnt_ref[...] = jnp.zeros_like(cnt_ref)

    x = jnp.where(step < steps_a, ha_ref[...], hb_ref[...])
    tm = x.shape[0]
    ms = jnp.mean(x * x, axis=-1, keepdims=True)
    xn = (x * lax.rsqrt(ms + EPS)) * g_ref[...]
    xn_ref[...] = xn
    x_hi = xn.astype(MXU_DTYPE)
    x_lo = (xn - x_hi.astype(jnp.float32)).astype(MXU_DTYPE)
    logits = jnp.dot(x_hi, whi_ref[...], preferred_element_type=jnp.float32)
    logits = logits + (jnp.dot(x_hi, wlo_ref[...], preferred_element_type=jnp.float32)
                       + jnp.dot(x_lo, whi_ref[...], preferred_element_type=jnp.float32))
    logits = logits + b_ref[...]
    lane = lax.broadcasted_iota(jnp.int32, (tm, LANES), 1)
    neg_inf = -jnp.inf

    def first_argmax(vals, vmax):
        return jnp.min(jnp.where(vals == vmax, lane, LANES), axis=-1, keepdims=True)

    is_group = (lane >= GROUP_LANE0) & (lane < GROUP_LANE0 + N_GROUPS)
    lg = jnp.where(is_group, logits, neg_inf)
    mg = jnp.max(lg, axis=-1, keepdims=True)
    g_idx = first_argmax(lg, mg) - GROUP_LANE0
    g_val = 1.0 / jnp.sum(jnp.exp(lg - mg), axis=-1, keepdims=True)

    in_group = (lane < N_EXPERTS) & (lax.shift_right_logical(lane, GROUP_SHIFT) == g_idx)
    le = jnp.where(in_group, logits, neg_inf)
    m1 = jnp.max(le, axis=-1, keepdims=True)
    e1 = first_argmax(le, m1)
    le2 = jnp.where(lane == e1, neg_inf, le)
    m2 = jnp.max(le2, axis=-1, keepdims=True)
    e2 = first_argmax(le2, m2)
    z = jnp.sum(jnp.exp(le - m1), axis=-1, keepdims=True)
    p1 = 1.0 / z
    p2 = jnp.exp(m2 - m1) / z
    den = p1 + p2
    w1 = g_val * (p1 / den)
    w2 = g_val * (p2 / den)

    hot1 = lane == e1
    hot2 = lane == e2
    hot = hot1.astype(jnp.float32) + hot2.astype(jnp.float32)
    r_i = lax.broadcasted_iota(jnp.int32, (tm, tm), 0)
    c_i = lax.broadcasted_iota(jnp.int32, (tm, tm), 1)
    lower = (c_i < r_i).astype(MXU_DTYPE)
    before = jnp.dot(lower, hot.astype(MXU_DTYPE), preferred_element_type=jnp.float32) + cnt_ref[...]
    rank1 = jnp.sum(jnp.where(hot1, before, 0.0), axis=-1, keepdims=True)
    rank2 = jnp.sum(jnp.where(hot2, before, 0.0), axis=-1, keepdims=True)
    cnt_ref[...] += jnp.sum(hot, axis=0, keepdims=True)

    meta = jnp.where(lane == META_E, e1.astype(jnp.float32), 0.0)
    meta = jnp.where(lane == META_E + 1, e2.astype(jnp.float32), meta)
    meta = jnp.where(lane == META_W, w1, meta)
    meta = jnp.where(lane == META_W + 1, w2, meta)
    meta = jnp.where(lane == META_RANK, rank1, meta)
    meta = jnp.where(lane == META_RANK + 1, rank2, meta)
    meta_ref[...] = meta


def _router(h_a, h_b, g, w_hi, w_lo, b_router):
    n_a, d = h_a.shape
    n_b = h_b.shape[0]
    n = n_a + n_b
    tm = min(ROUTER_TM, n_a, n_b)
    steps_a = n_a // tm
    return pl.pallas_call(
        functools.partial(_router_kernel, steps_a=steps_a),
        grid=(n // tm,),
        in_specs=[
            pl.BlockSpec((tm, d), lambda i: (jnp.minimum(i, steps_a - 1), 0)),
            pl.BlockSpec((tm, d), lambda i: (jnp.maximum(i - steps_a, 0), 0)),
            pl.BlockSpec((1, d), lambda i: (0, 0)),
            pl.BlockSpec((d, LANES), lambda i: (0, 0)),
            pl.BlockSpec((d, LANES), lambda i: (0, 0)),
            pl.BlockSpec((1, LANES), lambda i: (0, 0)),
        ],
        out_specs=[
            pl.BlockSpec((tm, d), lambda i: (i, 0)),
            pl.BlockSpec((tm, LANES), lambda i: (i, 0)),
            pl.BlockSpec((1, LANES), lambda i: (0, 0)),
        ],
        out_shape=[
            jax.ShapeDtypeStruct((n, d), jnp.float32),
            jax.ShapeDtypeStruct((n, LANES), jnp.float32),
            jax.ShapeDtypeStruct((1, LANES), jnp.float32),
        ],
        compiler_params=_cparams("arbitrary"),
    )(h_a, h_b, g.reshape(1, d), w_hi, w_lo, b_router)


def _experts_kernel(pos_ref, start_ref, count_ref, rows_ref, nused_ref, x_hbm, wg_ref, wu_ref, wd_ref, y_hbm,
                    tok_ref, xbuf, obuf, gsem, osem, *, bm, n_blocks):
    e = pl.program_id(0)
    n_used = nused_ref[0]
    b0 = start_ref[e]

    def start_gather(blk, slot):
        base = blk * bm
        for r in range(bm):
            tok = tok_ref[base + r]
            pltpu.make_async_copy(x_hbm.at[pl.ds(tok, 1), :], xbuf.at[slot, pl.ds(r, 1), :],
                                  gsem.at[slot]).start()

    def wait_gather(slot):
        pltpu.make_async_copy(x_hbm.at[pl.ds(0, bm), :], xbuf.at[slot], gsem.at[slot]).wait()

    def out_copy(blk, slot):
        return pltpu.make_async_copy(obuf.at[slot], y_hbm.at[pl.ds(blk * bm, bm), :], osem.at[slot])

    @pl.when(e == 0)
    def _():
        n_assign = pos_ref.shape[0]

        def clear_padding(ex, _):
            def clear(i, _):
                tok_ref[i] = 0
                return 0
            row0 = start_ref[ex] * bm
            lax.fori_loop(row0 + rows_ref[ex], row0 + count_ref[ex] * bm, clear, 0)
            return 0

        def invert(i8, _):
            for k in range(SUBLANES):
                a = i8 * SUBLANES + k
                tok_ref[pos_ref[a]] = lax.shift_right_logical(a, TOP_K_SHIFT)
            return 0

        lax.fori_loop(0, pl.num_programs(0), clear_padding, 0)
        lax.fori_loop(0, n_assign // SUBLANES, invert, 0)
        start_gather(0, 0)

        @pl.when(n_used > 1)
        def _():
            start_gather(1, 1)

    def block_body(j, _):
        b = b0 + j
        slot = lax.bitwise_and(b, 1)
        gslot = lax.rem(b, GATHER_SLOTS)

        @pl.when(b + GATHER_SLOTS - 1 < n_used)
        def _():
            start_gather(b + GATHER_SLOTS - 1, lax.rem(b + GATHER_SLOTS - 1, GATHER_SLOTS))

        wait_gather(gslot)

        @pl.when(b >= 2)
        def _():
            out_copy(b - 2, slot).wait()

        x = xbuf[gslot].astype(MXU_DTYPE)
        hg = jnp.dot(x, wg_ref[...].astype(MXU_DTYPE), preferred_element_type=jnp.float32)
        hu = jnp.dot(x, wu_ref[...].astype(MXU_DTYPE), preferred_element_type=jnp.float32)
        hid = (jax.nn.silu(hg) * hu).astype(MXU_DTYPE)
        obuf[slot] = jnp.dot(hid, wd_ref[...].astype(MXU_DTYPE), preferred_element_type=jnp.float32)
        out_copy(b, slot).start()
        return 0

    lax.fori_loop(0, count_ref[e], block_body, 0)

    @pl.when(e == pl.num_programs(0) - 1)
    def _():
        @pl.when(n_used >= 2)
        def _():
            out_copy(n_used - 2, n_used % 2).wait()

        out_copy(n_used - 1, (n_used - 1) % 2).wait()
        obuf[0] = jnp.zeros(obuf.shape[1:], jnp.float32)

        def zero_start(b, _):
            out_copy(b, 0).start()
            return 0

        def zero_wait(b, _):
            out_copy(b, 0).wait()
            return 0

        lax.fori_loop(n_used, n_blocks, zero_start, 0)
        lax.fori_loop(n_used, n_blocks, zero_wait, 0)


def _experts(xn2, pos, blk_start, blk_count, exp_rows, n_used, w_gate, w_up, w_down, *, n_blocks):
    n, d = xn2.shape
    n_experts, _, f = w_gate.shape
    bm = MOE_BM
    grid_spec = pltpu.PrefetchScalarGridSpec(
        num_scalar_prefetch=5,
        grid=(n_experts,),
        in_specs=[
            pl.BlockSpec(memory_space=pl.ANY),
            pl.BlockSpec((None, d, f), lambda e, *_: (e, 0, 0)),
            pl.BlockSpec((None, d, f), lambda e, *_: (e, 0, 0)),
            pl.BlockSpec((None, f, d), lambda e, *_: (e, 0, 0)),
        ],
        out_specs=pl.BlockSpec(memory_space=pl.ANY),
        scratch_shapes=[
            pltpu.SMEM((n_blocks * bm,), jnp.int32),
            pltpu.VMEM((GATHER_SLOTS, bm, d), jnp.float32),
            pltpu.VMEM((2, bm, d), jnp.float32),
            pltpu.SemaphoreType.DMA((GATHER_SLOTS,)),
            pltpu.SemaphoreType.DMA((2,)),
        ],
    )
    return pl.pallas_call(
        functools.partial(_experts_kernel, bm=bm, n_blocks=n_blocks),
        grid_spec=grid_spec,
        out_shape=jax.ShapeDtypeStruct((n_blocks * bm, d), jnp.float32),
        compiler_params=_gather_cparams(EXPERTS_VMEM_LIMIT_BYTES),
    )(pos, blk_start, blk_count, exp_rows, n_used, xn2, w_gate, w_up, w_down)


def _combine_kernel(pos_ref, y_hbm, h_ref, meta_ref, g_ref, o_ref, ybuf, sem, *, tm, n_steps, blk0):
    step = pl.program_id(0)

    def start_gather(blk, slot):
        base = (blk0 + blk) * tm * TOP_K
        for r in range(tm):
            for k in range(TOP_K):
                pos = pos_ref[base + r * TOP_K + k]
                pltpu.make_async_copy(y_hbm.at[pl.ds(pos, 1), :], ybuf.at[slot, k, pl.ds(r, 1), :],
                                      sem.at[slot]).start()

    def wait_gather(slot):
        for k in range(TOP_K):
            pltpu.make_async_copy(y_hbm.at[pl.ds(0, tm), :], ybuf.at[slot, k], sem.at[slot]).wait()

    slot = step % 2

    @pl.when(step == 0)
    def _():
        start_gather(0, 0)

    @pl.when(step + 1 < n_steps)
    def _():
        start_gather(step + 1, 1 - slot)

    wait_gather(slot)
    w1 = meta_ref[:, META_W:META_W + 1]
    w2 = meta_ref[:, META_W + 1:META_W + 2]
    h = h_ref[...] + (ybuf[slot, 0] * w1 + ybuf[slot, 1] * w2)
    ms = jnp.mean(h * h, axis=-1, keepdims=True)
    o_ref[...] = (h * lax.rsqrt(ms + EPS)) * g_ref[...]


def _combine(pos, yb, h2, meta, g, row0):
    n, d = h2.shape
    tm = min(COMBINE_TM, n)
    n_steps = n // tm
    assert row0 % tm == 0
    blk0 = row0 // tm
    grid_spec = pltpu.PrefetchScalarGridSpec(
        num_scalar_prefetch=1,
        grid=(n_steps,),
        in_specs=[
            pl.BlockSpec(memory_space=pl.ANY),
            pl.BlockSpec((tm, d), lambda i, pos: (i, 0)),
            pl.BlockSpec((tm, LANES), lambda i, pos: (blk0 + i, 0)),
            pl.BlockSpec((1, d), lambda i, pos: (0, 0)),
        ],
        out_specs=pl.BlockSpec((tm, d), lambda i, pos: (i, 0)),
        scratch_shapes=[
            pltpu.VMEM((2, TOP_K, tm, d), jnp.float32),
            pltpu.SemaphoreType.DMA((2,)),
        ],
    )
    return pl.pallas_call(
        functools.partial(_combine_kernel, tm=tm, n_steps=n_steps, blk0=blk0),
        grid_spec=grid_spec,
        out_shape=jax.ShapeDtypeStruct((n, d), jnp.float32),
        compiler_params=_gather_cparams(VMEM_LIMIT_BYTES),
    )(pos, yb, h2, meta, g.reshape(1, d))


def _dispatch_plan(meta, counts_f, n):
    bm = MOE_BM
    a = n * TOP_K
    expert = meta[:, META_E:META_E + TOP_K].astype(jnp.int32)
    rank = meta[:, META_RANK:META_RANK + TOP_K].astype(jnp.int32)
    counts = counts_f[0, :N_EXPERTS].astype(jnp.int32)
    padded = (counts + bm - 1) // bm * bm
    pad_end = jnp.cumsum(padded)
    pad_start = pad_end - padded
    pos = (pad_start[expert] + rank).reshape(a)
    n_blocks = -(-a // bm) + N_EXPERTS
    blk_start = (pad_start // bm).astype(jnp.int32)
    blk_count = (padded // bm).astype(jnp.int32)
    n_used = (pad_end[-1] // bm).astype(jnp.int32).reshape(1)
    return pos, blk_start, blk_count, counts, n_used, n_blocks


def _mixer_residual(x, p):
    b, s, d = x.shape
    n = b * s
    d_rnn = p["d_rnn"]
    d_attn = p["d_attn"]
    x2 = x.reshape(n, d)
    proj2 = _norm_proj(x2, p["norm_mix_g"], p["w_in"])
    d_in = proj2.shape[1]
    proj3 = proj2.reshape(b, s, d_in)
    hf = _lru_direction(proj3, p["conv_w"], p["conv_b"], p["lru_wg"][0], p["lru_b_a"][0], p["lru_b_x"][0],
                        p["lru_lambda"][0], None, reverse=False, d_rnn=d_rnn)
    ya = _lru_direction(proj3, p["conv_w"], p["conv_b"], p["lru_wg"][1], p["lru_b_a"][1], p["lru_b_x"][1],
                        p["lru_lambda"][1], hf, reverse=True, d_rnn=d_rnn)
    att = _attention(proj3, p["att_table"], d_rnn=d_rnn, d_attn=d_attn)
    m2 = _merge(ya.reshape(n, d_rnn), att.reshape(n, d_attn), p["w_lru_up"], p["w_attn_up"], proj2,
                p["b_merge"], gate_col=2 * d_rnn + 3 * d_attn)
    return _out_proj(m2, p["w_o"], x2)


def _moe_and_final_norm(h_a, h_b, p):
    n_a, n_b = h_a.shape[0], h_b.shape[0]
    xn2, meta, counts = _router(h_a, h_b, p["norm_ffn_g"], p["w_router_hi"], p["w_router_lo"], p["b_router"])
    pos, blk_start, blk_count, exp_rows, n_used, n_blocks = _dispatch_plan(meta, counts, n_a + n_b)
    yb = _experts(xn2, pos, blk_start, blk_count, exp_rows, n_used, p["w_exp_gate"], p["w_exp_up"], p["w_exp_down"],
                  n_blocks=n_blocks)
    y_a = _combine(pos, yb, h_a, meta, p["norm_final_g"], 0)
    y_b = _combine(pos, yb, h_b, meta, p["norm_final_g"], n_a)
    return y_a, y_b


def _prepare_params(norm_mix_g, w_in, conv_w, conv_b, lru_w_a, lru_b_a, lru_w_x, lru_b_x, lru_lambda, rpb,
                    w_lru_up, w_attn_up, b_merge, w_o, norm_ffn_g, w_router_group, b_router_group,
                    w_router_expert, b_router_expert, w_exp_gate, w_exp_up, w_exp_down, norm_final_g):
    bf16 = MXU_DTYPE
    d = w_in.shape[1]
    d_rnn = conv_w.shape[2]
    d_attn = w_attn_up.shape[1]
    lru_wg = jnp.concatenate([lru_w_a[0], lru_w_x[0]], axis=-1).astype(bf16)
    w_router = jnp.zeros((d, LANES), jnp.float32)
    w_router = w_router.at[:, :N_EXPERTS].set(w_router_expert[0])
    w_router = w_router.at[:, GROUP_LANE0:GROUP_LANE0 + N_GROUPS].set(w_router_group[0])
    w_router_hi = w_router.astype(bf16)
    w_router_lo = (w_router - w_router_hi.astype(jnp.float32)).astype(bf16)
    b_router = jnp.zeros((1, LANES), jnp.float32)
    b_router = b_router.at[0, :N_EXPERTS].set(b_router_expert[0])
    b_router = b_router.at[0, GROUP_LANE0:GROUP_LANE0 + N_GROUPS].set(b_router_group[0])
    return dict(
        d_rnn=d_rnn, d_attn=d_attn,
        norm_mix_g=norm_mix_g[0], w_in=w_in[0].astype(bf16),
        conv_w=conv_w[0], conv_b=conv_b[0], lru_wg=lru_wg,
        lru_b_a=lru_b_a[0], lru_b_x=lru_b_x[0], lru_lambda=lru_lambda[0],
        att_table=_attention_bias_table(rpb[0]),
        w_lru_up=w_lru_up[0].astype(bf16), w_attn_up=w_attn_up[0].astype(bf16),
        b_merge=b_merge[0], w_o=w_o[0].astype(bf16),
        norm_ffn_g=norm_ffn_g[0], w_router_hi=w_router_hi, w_router_lo=w_router_lo, b_router=b_router,
        w_exp_gate=w_exp_gate[0], w_exp_up=w_exp_up[0], w_exp_down=w_exp_down[0],
        norm_final_g=norm_final_g,
    )


def kernel(x_prompt, x_sample, norm_mix_g, w_in, conv_w, conv_b, lru_w_a, lru_b_a, lru_w_x, lru_b_x, lru_lambda, rpb, w_lru_up, w_attn_up, b_merge, w_o, norm_ffn_g, w_router_group, b_router_group, w_router_expert, b_router_expert, w_exp_gate, w_exp_up, w_exp_down, norm_final_g):
    assert norm_mix_g.shape[0] == 1, "single-layer encoder"
    p = _prepare_params(norm_mix_g, w_in, conv_w, conv_b, lru_w_a, lru_b_a, lru_w_x, lru_b_x, lru_lambda, rpb,
                        w_lru_up, w_attn_up, b_merge, w_o, norm_ffn_g, w_router_group, b_router_group,
                        w_router_expert, b_router_expert, w_exp_gate, w_exp_up, w_exp_down, norm_final_g)
    h_prompt = _mixer_residual(x_prompt, p)
    h_sample = _mixer_residual(x_sample, p)
    y_prompt, y_sample = _moe_and_final_norm(h_prompt, h_sample, p)
    return (y_prompt.reshape(x_prompt.shape), y_sample.reshape(x_sample.shape))
```

```python
import functools

import numpy as np
import jax
import jax.numpy as jnp
from jax import lax
from jax.experimental import pallas as pl
from jax.experimental.pallas import tpu as pltpu

GRID_W = 64
WIN_R = 8
WIN_C = 16
CONV_W = 4
LRU_C = 8.0
LRU_BLOCK = 128
HEAD_DIM = 128
N_GROUPS = 4
EXPERTS_PER_GROUP = 8
N_EXPERTS = N_GROUPS * EXPERTS_PER_GROUP
TOP_K = 2
EPS = 1e-6
MASK_VALUE = -1e30

MXU_DTYPE = jnp.bfloat16

LANES = 128
SUBLANES = 8
VMEM_LIMIT_BYTES = 56 * 1024 * 1024
EXPERTS_VMEM_LIMIT_BYTES = 62 * 1024 * 1024

PROJ_TM = 1024
PROJ_TN = 2048
MERGE_TM = 512
MERGE_TN = 2048
OUT_PROJ_TM = 1024
OUT_PROJ_TN = 2048
LRU_TC = 8192
LRU_GROUPS = 4
LRU_UNROLL = 4
ATT_ROWS = 64
ROUTER_TM = 512
MOE_BM = 256
COMBINE_TM = 256
GATHER_SLOTS = 3


def _cparams(*sem):
    return pltpu.CompilerParams(dimension_semantics=sem, vmem_limit_bytes=VMEM_LIMIT_BYTES)


def _gather_cparams(vmem_limit):
    return pltpu.CompilerParams(dimension_semantics=("arbitrary",), vmem_limit_bytes=vmem_limit)


def _norm_proj_kernel(x_ref, g_ref, w_ref, o_ref, xn_ref):
    @pl.when(pl.program_id(1) == 0)
    def _():
        x = x_ref[...]
        ms = jnp.mean(x * x, axis=-1, keepdims=True)
        xn_ref[...] = ((x * lax.rsqrt(ms + EPS)) * g_ref[...]).astype(MXU_DTYPE)

    o_ref[...] = jnp.dot(xn_ref[...], w_ref[...], preferred_element_type=jnp.float32)


def _norm_proj(x2, g, w_bf16):
    n, d = x2.shape
    d_out = w_bf16.shape[1]
    tm = min(PROJ_TM, n)
    tn = min(PROJ_TN, d_out)
    return pl.pallas_call(
        _norm_proj_kernel,
        grid=(n // tm, d_out // tn),
        in_specs=[
            pl.BlockSpec((tm, d), lambda i, j: (i, 0)),
            pl.BlockSpec((1, d), lambda i, j: (0, 0)),
            pl.BlockSpec((d, tn), lambda i, j: (0, j)),
        ],
        out_specs=pl.BlockSpec((tm, tn), lambda i, j: (i, j)),
        out_shape=jax.ShapeDtypeStruct((n, d_out), jnp.float32),
        scratch_shapes=[pltpu.VMEM((tm, d), MXU_DTYPE)],
        compiler_params=_cparams("parallel", "arbitrary"),
    )(x2, g.reshape(1, d), w_bf16)


def _softplus(y):
    return jnp.maximum(y, 0.0) + jnp.log1p(jnp.exp(-jnp.abs(y)))


def _lru_pitch(tc):
    seg_len = tc // (LRU_GROUPS * SUBLANES)
    assert seg_len * LRU_GROUPS * SUBLANES == tc and seg_len % SUBLANES == 0
    return seg_len + SUBLANES // 2


def _lru_kernel(*refs, reverse, n_chunks, tc):
    if reverse:
        (x_ref, xp_ref, xn_ref, cw_ref, cb_ref, wg_ref, ba_ref, bx_ref, lam_ref,
         hf_ref, gate_ref, o_ref, xpad_ref, a_ref, u_ref, h_ref, carry_ref) = refs
    else:
        (x_ref, xp_ref, xn_ref, cw_ref, cb_ref, wg_ref, ba_ref, bx_ref, lam_ref,
         o_ref, xpad_ref, a_ref, u_ref, h_ref, carry_ref) = refs
    t = pl.program_id(2)
    chunk = (n_chunks - 1 - t) if reverse else t

    @pl.when(t == 0)
    def _():
        carry_ref[...] = jnp.zeros_like(carry_ref)

    prev = jnp.where(chunk == 0, 0.0, xp_ref[...])
    nxt = jnp.where(chunk == n_chunks - 1, 0.0, xn_ref[...])
    xpad_ref[pl.ds(0, SUBLANES), :] = prev
    xpad_ref[pl.ds(SUBLANES, tc), :] = x_ref[...]
    xpad_ref[pl.ds(SUBLANES + tc, SUBLANES), :] = nxt
    left = CONV_W // 2
    xc = cb_ref[...]
    for tap in range(CONV_W):
        xc = xc + xpad_ref[pl.ds(SUBLANES - left + tap, tc), :] * cw_ref[pl.ds(tap, 1), :]

    z = jnp.dot(xc.astype(MXU_DTYPE), wg_ref[...], preferred_element_type=jnp.float32)
    r = jax.nn.sigmoid(z[:, :LRU_BLOCK] + ba_ref[...])
    i = jax.nn.sigmoid(z[:, LRU_BLOCK:] + bx_ref[...])
    log_a = (-LRU_C * r) * _softplus(-lam_ref[...])
    a = jnp.exp(log_a)
    u = jnp.sqrt(1.0 - a * a) * (i * xc)

    pitch = _lru_pitch(tc)
    slots = LRU_GROUPS * SUBLANES * pitch
    a_ref[pl.ds(0, tc), :] = a
    u_ref[pl.ds(0, tc), :] = u
    a_ref[pl.ds(tc, slots - tc), :] = jnp.ones((slots - tc, LRU_BLOCK), jnp.float32)
    u_ref[pl.ds(tc, slots - tc), :] = jnp.zeros((slots - tc, LRU_BLOCK), jnp.float32)

    def seg_rows(g, k):
        return pl.ds(g * SUBLANES * pitch + k, SUBLANES, stride=pitch)

    def slot_of(kk):
        return (pitch - 1 - kk) if reverse else kk

    def scan_step(kk, carry):
        hs, ps = carry
        k = slot_of(kk)
        new_h, new_p = [], []
        for g in range(LRU_GROUPS):
            a_k = a_ref[seg_rows(g, k), :]
            h = a_k * hs[g] + u_ref[seg_rows(g, k), :]
            h_ref[seg_rows(g, k), :] = h
            new_h.append(h)
            new_p.append(a_k * ps[g])
        return tuple(new_h), tuple(new_p)

    zeros = tuple(jnp.zeros((SUBLANES, LANES), jnp.float32) for _ in range(LRU_GROUPS))
    ones = tuple(jnp.ones((SUBLANES, LANES), jnp.float32) for _ in range(LRU_GROUPS))
    h_end, p_end = lax.fori_loop(0, pitch, scan_step, (zeros, ones), unroll=LRU_UNROLL)

    row = lax.broadcasted_iota(jnp.int32, (SUBLANES, LANES), 0)
    c_in = [None] * LRU_GROUPS
    cg = carry_ref[...]
    for g in (reversed(range(LRU_GROUPS)) if reverse else range(LRU_GROUPS)):
        pa, hb = p_end[g], h_end[g]
        for s in (1, 2, 4):
            keep = (row < SUBLANES - s) if reverse else (row >= s)
            shift = (SUBLANES - s) if reverse else s
            a_sh = jnp.where(keep, pltpu.roll(pa, shift, 0), 1.0)
            b_sh = jnp.where(keep, pltpu.roll(hb, shift, 0), 0.0)
            hb = pa * b_sh + hb
            pa = pa * a_sh
        seg_out = pa * cg + hb
        if reverse:
            c_in[g] = jnp.where(row == SUBLANES - 1, cg, pltpu.roll(seg_out, SUBLANES - 1, 0))
            cg = seg_out[0:1, :]
        else:
            c_in[g] = jnp.where(row == 0, cg, pltpu.roll(seg_out, 1, 0))
            cg = seg_out[SUBLANES - 1:SUBLANES, :]
    carry_ref[...] = cg

    def fix_step(kk, ps):
        k = slot_of(kk)
        new_p = []
        for g in range(LRU_GROUPS):
            p = a_ref[seg_rows(g, k), :] * ps[g]
            u_ref[seg_rows(g, k), :] = h_ref[seg_rows(g, k), :] + p * c_in[g]
            new_p.append(p)
        return tuple(new_p)

    lax.fori_loop(0, pitch, fix_step, ones, unroll=LRU_UNROLL)

    h_all = u_ref[pl.ds(0, tc), :]
    if reverse:
        o_ref[...] = (jax.nn.gelu(gate_ref[...]) * (hf_ref[...] + h_all)).astype(o_ref.dtype)
    else:
        o_ref[...] = h_all


def _lru_direction(proj3, conv_w, conv_b, wg, ba, bx, lam, hf, *, reverse, d_rnn):
    b, s, _ = proj3.shape
    n_cb = d_rnn // LRU_BLOCK
    tc = min(LRU_TC, s)
    n_chunks = s // tc
    tc8 = tc // SUBLANES
    s8 = s // SUBLANES
    slots = LRU_GROUPS * SUBLANES * _lru_pitch(tc)

    def chunk_of(t):
        return (n_chunks - 1 - t) if reverse else t

    vec_spec = pl.BlockSpec((1, LRU_BLOCK), lambda bi, c, t: (0, c))
    in_specs = [
        pl.BlockSpec((None, tc, LRU_BLOCK), lambda bi, c, t: (bi, chunk_of(t), c)),
        pl.BlockSpec((None, SUBLANES, LRU_BLOCK),
                     lambda bi, c, t: (bi, jnp.maximum(chunk_of(t) * tc8 - 1, 0), c)),
        pl.BlockSpec((None, SUBLANES, LRU_BLOCK),
                     lambda bi, c, t: (bi, jnp.minimum((chunk_of(t) + 1) * tc8, s8 - 1), c)),
        pl.BlockSpec((CONV_W, LRU_BLOCK), lambda bi, c, t: (0, c)),
        vec_spec,
        pl.BlockSpec((None, LRU_BLOCK, 2 * LRU_BLOCK), lambda bi, c, t: (c, 0, 0)),
        vec_spec, vec_spec, vec_spec,
    ]
    args = [proj3, proj3, proj3, conv_w, conv_b.reshape(1, d_rnn), wg,
            ba.reshape(1, d_rnn), bx.reshape(1, d_rnn), lam.reshape(1, d_rnn)]
    if reverse:
        in_specs += [
            pl.BlockSpec((None, tc, LRU_BLOCK), lambda bi, c, t: (bi, chunk_of(t), c)),
            pl.BlockSpec((None, tc, LRU_BLOCK), lambda bi, c, t: (bi, chunk_of(t), n_cb + c)),
        ]
        args += [hf, proj3]
        out_dtype = MXU_DTYPE
    else:
        out_dtype = jnp.float32
    return pl.pallas_call(
        functools.partial(_lru_kernel, reverse=reverse, n_chunks=n_chunks, tc=tc),
        grid=(b, n_cb, n_chunks),
        in_specs=in_specs,
        out_specs=pl.BlockSpec((None, tc, LRU_BLOCK), lambda bi, c, t: (bi, chunk_of(t), c)),
        out_shape=jax.ShapeDtypeStruct((b, s, d_rnn), out_dtype),
        scratch_shapes=[
            pltpu.VMEM((tc + 2 * SUBLANES, LRU_BLOCK), jnp.float32),
            pltpu.VMEM((slots, LRU_BLOCK), jnp.float32),
            pltpu.VMEM((slots, LRU_BLOCK), jnp.float32),
            pltpu.VMEM((slots, LRU_BLOCK), jnp.float32),
            pltpu.VMEM((1, LRU_BLOCK), jnp.float32),
        ],
        compiler_params=_cparams("parallel", "parallel", "arbitrary"),
    )(*args)


def _attention_bias_table(rpb):
    n_heads, n_dr, n_dc = rpb.shape
    cq = np.arange(GRID_W)
    cs = np.clip(cq - WIN_C // 2, 0, GRID_W - WIN_C)
    ck = np.arange(GRID_W)
    valid = (ck[None, :] >= cs[:, None]) & (ck[None, :] < cs[:, None] + WIN_C)
    pad = GRID_W - WIN_C
    width = n_dc + 2 * pad + 1
    padded = jnp.pad(rpb.astype(jnp.float32), ((0, 0), (0, 0), (pad, pad + 1)))
    tiled = jnp.tile(padded, (1, 1, GRID_W))[:, :, :GRID_W * (width - 1)]
    skew = tiled.reshape(n_heads, n_dr, GRID_W, width - 1)
    toep = skew[:, :, :, GRID_W - 1:2 * GRID_W - 1]
    toep = jnp.where(valid[None, None], toep, MASK_VALUE)
    toep = toep.transpose(0, 2, 1, 3)
    per_off = [toep[:, :, WIN_R - 1 - off:2 * WIN_R - 1 - off] for off in range(WIN_R)]
    tab = jnp.stack(per_off, axis=1)
    return tab.reshape(n_heads, WIN_R, GRID_W, WIN_R * GRID_W)


def _attention_kernel(q_ref, k_ref, v_ref, tb_ref, o_ref, kb_ref, vb_ref, *, rows, tile_rows):
    kb_ref[...] = k_ref[...].astype(MXU_DTYPE)
    vb_ref[...] = v_ref[...].astype(MXU_DTYPE)
    scale = HEAD_DIM ** -0.5
    band = WIN_R * GRID_W

    def tile_body(i, _):
        starts, scores = [], []
        for rl in range(tile_rows):
            r = i * tile_rows + rl
            rs = jnp.clip(r - WIN_R // 2, 0, rows - WIN_R)
            start = pl.multiple_of(rs * GRID_W, GRID_W)
            q0 = pl.multiple_of(r * GRID_W, GRID_W)
            q = (q_ref[pl.ds(q0, GRID_W), :] * scale).astype(MXU_DTYPE)
            kband = kb_ref[pl.ds(start, band), :]
            s = lax.dot_general(q, kband, (((1,), (1,)), ((), ())), preferred_element_type=jnp.float32)
            starts.append(start)
            scores.append(s + tb_ref[r - rs])
        probs, denoms = [], []
        for s in scores:
            m = jnp.max(s, axis=-1, keepdims=True)
            p = jnp.exp(s - m)
            denoms.append(jnp.sum(p, axis=-1, keepdims=True))
            probs.append(p.astype(MXU_DTYPE))
        for rl in range(tile_rows):
            vband = vb_ref[pl.ds(starts[rl], band), :]
            o = jnp.dot(probs[rl], vband, preferred_element_type=jnp.float32)
            q0 = pl.multiple_of((i * tile_rows + rl) * GRID_W, GRID_W)
            o_ref[pl.ds(q0, GRID_W), :] = (o / denoms[rl]).astype(o_ref.dtype)
        return 0

    lax.fori_loop(0, rows // tile_rows, tile_body, 0)


def _attention(proj3, table, *, d_rnn, d_attn):
    b, s, _ = proj3.shape
    rows = s // GRID_W
    n_heads = d_attn // HEAD_DIM
    q_blk = 2 * d_rnn // HEAD_DIM
    k_blk = q_blk + n_heads
    v_blk = k_blk + n_heads
    tile_rows = min(ATT_ROWS, rows)
    assert rows % tile_rows == 0 and rows >= WIN_R
    return pl.pallas_call(
        functools.partial(_attention_kernel, rows=rows, tile_rows=tile_rows),
        grid=(b, n_heads),
        in_specs=[
            pl.BlockSpec((None, s, HEAD_DIM), lambda bi, h: (bi, 0, q_blk + h)),
            pl.BlockSpec((None, s, HEAD_DIM), lambda bi, h: (bi, 0, k_blk + h)),
            pl.BlockSpec((None, s, HEAD_DIM), lambda bi, h: (bi, 0, v_blk + h)),
            pl.BlockSpec((None, WIN_R, GRID_W, WIN_R * GRID_W), lambda bi, h: (h, 0, 0, 0)),
        ],
        out_specs=pl.BlockSpec((None, s, HEAD_DIM), lambda bi, h: (bi, 0, h)),
        out_shape=jax.ShapeDtypeStruct((b, s, d_attn), MXU_DTYPE),
        scratch_shapes=[
            pltpu.VMEM((s, HEAD_DIM), MXU_DTYPE),
            pltpu.VMEM((s, HEAD_DIM), MXU_DTYPE),
        ],
        compiler_params=_cparams("parallel", "parallel"),
    )(proj3, proj3, proj3, table)


def _weight_spec(d, tn):
    if tn == d:
        return pl.BlockSpec((d, tn), lambda i, j: (0, j), pipeline_mode=pl.Buffered(1))
    return pl.BlockSpec((d, tn), lambda i, j: (0, j))


def _merge_kernel(ya_ref, att_ref, wl_ref, wa_ref, gl_ref, ga_ref, bm_ref, o_ref):
    y_lru = jnp.dot(ya_ref[...], wl_ref[...], preferred_element_type=jnp.float32)
    y_att = jnp.dot(att_ref[...], wa_ref[...], preferred_element_type=jnp.float32)
    s_lru = jax.nn.sigmoid(gl_ref[...] + bm_ref[pl.ds(0, 1), :])
    s_att = jax.nn.sigmoid(ga_ref[...] + bm_ref[pl.ds(1, 1), :])
    o_ref[...] = (s_lru * y_lru + s_att * y_att).astype(o_ref.dtype)


def _merge(ya2, att2, wl, wa, proj2, b_merge, *, gate_col):
    n, d = ya2.shape
    tm = min(MERGE_TM, n)
    tn = min(MERGE_TN, d)
    gl_blk = gate_col // tn
    ga_blk = (gate_col + d) // tn
    return pl.pallas_call(
        _merge_kernel,
        grid=(n // tm, d // tn),
        in_specs=[
            pl.BlockSpec((tm, d), lambda i, j: (i, 0)),
            pl.BlockSpec((tm, d), lambda i, j: (i, 0)),
            _weight_spec(d, tn),
            _weight_spec(d, tn),
            pl.BlockSpec((tm, tn), lambda i, j: (i, gl_blk + j)),
            pl.BlockSpec((tm, tn), lambda i, j: (i, ga_blk + j)),
            pl.BlockSpec((2, tn), lambda i, j: (0, j)),
        ],
        out_specs=pl.BlockSpec((tm, tn), lambda i, j: (i, j)),
        out_shape=jax.ShapeDtypeStruct((n, d), MXU_DTYPE),
        compiler_params=_cparams("parallel", "arbitrary"),
    )(ya2, att2, wl, wa, proj2, proj2, b_merge)


def _out_proj_kernel(m_ref, w_ref, x_ref, o_ref):
    o_ref[...] = x_ref[...] + jnp.dot(m_ref[...], w_ref[...], preferred_element_type=jnp.float32)


def _out_proj(m2, w_o, x2):
    n, d = m2.shape
    tm = min(OUT_PROJ_TM, n)
    tn = min(OUT_PROJ_TN, d)
    return pl.pallas_call(
        _out_proj_kernel,
        grid=(n // tm, d // tn),
        in_specs=[
            pl.BlockSpec((tm, d), lambda i, j: (i, 0)),
            _weight_spec(d, tn),
            pl.BlockSpec((tm, tn), lambda i, j: (i, j)),
        ],
        out_specs=pl.BlockSpec((tm, tn), lambda i, j: (i, j)),
        out_shape=jax.ShapeDtypeStruct((n, d), jnp.float32),
        compiler_params=_cparams("parallel", "arbitrary"),
    )(m2, w_o, x2)


GROUP_LANE0 = N_EXPERTS
META_E, META_W, META_RANK = 0, 2, 4
GROUP_SHIFT = EXPERTS_PER_GROUP.bit_length() - 1
TOP_K_SHIFT = TOP_K.bit_length() - 1
assert 1 << GROUP_SHIFT == EXPERTS_PER_GROUP and 1 << TOP_K_SHIFT == TOP_K


def _router_kernel(ha_ref, hb_ref, g_ref, whi_ref, wlo_ref, b_ref, xn_ref, meta_ref, cnt_ref, *, steps_a):
    step = pl.program_id(0)

    @pl.when(step == 0)
    def _():
        cnt_ref[...] = jnp.zeros_like(cnt_ref)

    x = jnp.where(step < steps_a, ha_ref[...], hb_ref[...])
    tm = x.shape[0]
    ms = jnp.mean(x * x, axis=-1, keepdims=True)
    xn = (x * lax.rsqrt(ms + EPS)) * g_ref[...]
    xn_ref[...] = xn
    x_hi = xn.astype(MXU_DTYPE)
    x_lo = (xn - x_hi.astype(jnp.float32)).astype(MXU_DTYPE)
    logits = jnp.dot(x_hi, whi_ref[...], preferred_element_type=jnp.float32)
    logits = logits + (jnp.dot(x_hi, wlo_ref[...], preferred_element_type=jnp.float32)
                       + jnp.dot(x_lo, whi_ref[...], preferred_element_type=jnp.float32))
    logits = logits + b_ref[...]
    lane = lax.broadcasted_iota(jnp.int32, (tm, LANES), 1)
    neg_inf = -jnp.inf

    def first_argmax(vals, vmax):
        return jnp.min(jnp.where(vals == vmax, lane, LANES), axis=-1, keepdims=True)

    is_group = (lane >= GROUP_LANE0) & (lane < GROUP_LANE0 + N_GROUPS)
    lg = jnp.where(is_group, logits, neg_inf)
    mg = jnp.max(lg, axis=-1, keepdims=True)
    g_idx = first_argmax(lg, mg) - GROUP_LANE0
    g_val = 1.0 / jnp.sum(jnp.exp(lg - mg), axis=-1, keepdims=True)

    in_group = (lane < N_EXPERTS) & (lax.shift_right_logical(lane, GROUP_SHIFT) == g_idx)
    le = jnp.where(in_group, logits, neg_inf)
    m1 = jnp.max(le, axis=-1, keepdims=True)
    e1 = first_argmax(le, m1)
    le2 = jnp.where(lane == e1, neg_inf, le)
    m2 = jnp.max(le2, axis=-1, keepdims=True)
    e2 = first_argmax(le2, m2)
    z = jnp.sum(jnp.exp(le - m1), axis=-1, keepdims=True)
    p1 = 1.0 / z
    p2 = jnp.exp(m2 - m1) / z
    den = p1 + p2
    w1 = g_val * (p1 / den)
    w2 = g_val * (p2 / den)

    hot1 = lane == e1
    hot2 = lane == e2
    hot = hot1.astype(jnp.float32) + hot2.astype(jnp.float32)
    r_i = lax.broadcasted_iota(jnp.int32, (tm, tm), 0)
    c_i = lax.broadcasted_iota(jnp.int32, (tm, tm), 1)
    lower = (c_i < r_i).astype(MXU_DTYPE)
    before = jnp.dot(lower, hot.astype(MXU_DTYPE), preferred_element_type=jnp.float32) + cnt_ref[...]
    rank1 = jnp.sum(jnp.where(hot1, before, 0.0), axis=-1, keepdims=True)
    rank2 = jnp.sum(jnp.where(hot2, before, 0.0), axis=-1, keepdims=True)
    cnt_ref[...] += jnp.sum(hot, axis=0, keepdims=True)

    meta = jnp.where(lane == META_E, e1.astype(jnp.float32), 0.0)
    meta = jnp.where(lane == META_E + 1, e2.astype(jnp.float32), meta)
    meta = jnp.where(lane == META_W, w1, meta)
    meta = jnp.where(lane == META_W + 1, w2, meta)
    meta = jnp.where(lane == META_RANK, rank1, meta)
    meta = jnp.where(lane == META_RANK + 1, rank2, meta)
    meta_ref[...] = meta


def _router(h_a, h_b, g, w_hi, w_lo, b_router):
    n_a, d = h_a.shape
    n_b = h_b.shape[0]
    n = n_a + n_b
    tm = min(ROUTER_TM, n_a, n_b)
    steps_a = n_a // tm
    return pl.pallas_call(
        functools.partial(_router_kernel, steps_a=steps_a),
        grid=(n // tm,),
        in_specs=[
            pl.BlockSpec((tm, d), lambda i: (jnp.minimum(i, steps_a - 1), 0)),
            pl.BlockSpec((tm, d), lambda i: (jnp.maximum(i - steps_a, 0), 0)),
            pl.BlockSpec((1, d), lambda i: (0, 0)),
            pl.BlockSpec((d, LANES), lambda i: (0, 0)),
            pl.BlockSpec((d, LANES), lambda i: (0, 0)),
            pl.BlockSpec((1, LANES), lambda i: (0, 0)),
        ],
        out_specs=[
            pl.BlockSpec((tm, d), lambda i: (i, 0)),
            pl.BlockSpec((tm, LANES), lambda i: (i, 0)),
            pl.BlockSpec((1, LANES), lambda i: (0, 0)),
        ],
        out_shape=[
            jax.ShapeDtypeStruct((n, d), jnp.float32),
            jax.ShapeDtypeStruct((n, LANES), jnp.float32),
            jax.ShapeDtypeStruct((1, LANES), jnp.float32),
        ],
        compiler_params=_cparams("arbitrary"),
    )(h_a, h_b, g.reshape(1, d), w_hi, w_lo, b_router)


def _experts_kernel(pos_ref, start_ref, count_ref, rows_ref, nused_ref, x_hbm, wg_ref, wu_ref, wd_ref, y_hbm,
                    tok_ref, xbuf, obuf, gsem, osem, *, bm, n_blocks):
    e = pl.program_id(0)
    n_used = nused_ref[0]
    b0 = start_ref[e]

    def start_gather(blk, slot):
        base = blk * bm
        for r in range(bm):
            tok = tok_ref[base + r]
            pltpu.make_async_copy(x_hbm.at[pl.ds(tok, 1), :], xbuf.at[slot, pl.ds(r, 1), :],
                                  gsem.at[slot]).start(priority=r % 2)

    def wait_gather(slot):
        pltpu.make_async_copy(x_hbm.at[pl.ds(0, bm), :], xbuf.at[slot], gsem.at[slot]).wait()

    def out_copy(blk, slot):
        return pltpu.make_async_copy(obuf.at[slot], y_hbm.at[pl.ds(blk * bm, bm), :], osem.at[slot])

    @pl.when(e == 0)
    def _():
        n_assign = pos_ref.shape[0]

        def clear_padding(ex, _):
            def clear(i, _):
                tok_ref[i] = 0
                return 0
            row0 = start_ref[ex] * bm
            lax.fori_loop(row0 + rows_ref[ex], row0 + count_ref[ex] * bm, clear, 0)
            return 0

        def invert(i8, _):
            for k in range(SUBLANES):
                a = i8 * SUBLANES + k
                tok_ref[pos_ref[a]] = lax.shift_right_logical(a, TOP_K_SHIFT)
            return 0

        lax.fori_loop(0, pl.num_programs(0), clear_padding, 0)
        lax.fori_loop(0, n_assign // SUBLANES, invert, 0)
        start_gather(0, 0)

        @pl.when(n_used > 1)
        def _():
            start_gather(1, 1)

    def block_body(j, _):
        b = b0 + j
        slot = lax.bitwise_and(b, 1)
        gslot = lax.rem(b, GATHER_SLOTS)

        @pl.when(b + GATHER_SLOTS - 1 < n_used)
        def _():
            start_gather(b + GATHER_SLOTS - 1, lax.rem(b + GATHER_SLOTS - 1, GATHER_SLOTS))

        wait_gather(gslot)

        @pl.when(b >= 2)
        def _():
            out_copy(b - 2, slot).wait()

        x = xbuf[gslot].astype(MXU_DTYPE)
        hg = jnp.dot(x, wg_ref[...].astype(MXU_DTYPE), preferred_element_type=jnp.float32)
        hu = jnp.dot(x, wu_ref[...].astype(MXU_DTYPE), preferred_element_type=jnp.float32)
        hid = (jax.nn.silu(hg) * hu).astype(MXU_DTYPE)
        obuf[slot] = jnp.dot(hid, wd_ref[...].astype(MXU_DTYPE), preferred_element_type=jnp.float32)
        out_copy(b, slot).start()
        return 0

    lax.fori_loop(0, count_ref[e], block_body, 0)

    @pl.when(e == pl.num_programs(0) - 1)
    def _():
        @pl.when(n_used >= 2)
        def _():
            out_copy(n_used - 2, n_used % 2).wait()

        out_copy(n_used - 1, (n_used - 1) % 2).wait()
        obuf[0] = jnp.zeros(obuf.shape[1:], jnp.float32)

        def zero_start(b, _):
            out_copy(b, 0).start()
            return 0

        def zero_wait(b, _):
            out_copy(b, 0).wait()
            return 0

        lax.fori_loop(n_used, n_blocks, zero_start, 0)
        lax.fori_loop(n_used, n_blocks, zero_wait, 0)


def _experts(xn2, pos, blk_start, blk_count, exp_rows, n_used, w_gate, w_up, w_down, *, n_blocks):
    n, d = xn2.shape
    n_experts, _, f = w_gate.shape
    bm = MOE_BM
    grid_spec = pltpu.PrefetchScalarGridSpec(
        num_scalar_prefetch=5,
        grid=(n_experts,),
        in_specs=[
            pl.BlockSpec(memory_space=pl.ANY),
            pl.BlockSpec((None, d, f), lambda e, *_: (e, 0, 0)),
            pl.BlockSpec((None, d, f), lambda e, *_: (e, 0, 0)),
            pl.BlockSpec((None, f, d), lambda e, *_: (e, 0, 0)),
        ],
        out_specs=pl.BlockSpec(memory_space=pl.ANY),
        scratch_shapes=[
            pltpu.SMEM((n_blocks * bm,), jnp.int32),
            pltpu.VMEM((GATHER_SLOTS, bm, d), jnp.float32),
            pltpu.VMEM((2, bm, d), jnp.float32),
            pltpu.SemaphoreType.DMA((GATHER_SLOTS,)),
            pltpu.SemaphoreType.DMA((2,)),
        ],
    )
    return pl.pallas_call(
        functools.partial(_experts_kernel, bm=bm, n_blocks=n_blocks),
        grid_spec=grid_spec,
        out_shape=jax.ShapeDtypeStruct((n_blocks * bm, d), jnp.float32),
        compiler_params=_gather_cparams(EXPERTS_VMEM_LIMIT_BYTES),
    )(pos, blk_start, blk_count, exp_rows, n_used, xn2, w_gate, w_up, w_down)


def _combine_kernel(pos_ref, y_hbm, h_ref, meta_ref, g_ref, o_ref, ybuf, sem, *, tm, n_steps, blk0):
    step = pl.program_id(0)

    def start_gather(blk, slot):
        base = (blk0 + blk) * tm * TOP_K
        for r in range(tm):
            for k in range(TOP_K):
                pos = pos_ref[base + r * TOP_K + k]
                pltpu.make_async_copy(y_hbm.at[pl.ds(pos, 1), :], ybuf.at[slot, k, pl.ds(r, 1), :],
                                      sem.at[slot]).start(priority=k)

    def wait_gather(slot):
        for k in range(TOP_K):
            pltpu.make_async_copy(y_hbm.at[pl.ds(0, tm), :], ybuf.at[slot, k], sem.at[slot]).wait()

    slot = step % 2

    @pl.when(step == 0)
    def _():
        start_gather(0, 0)

    @pl.when(step + 1 < n_steps)
    def _():
        start_gather(step + 1, 1 - slot)

    wait_gather(slot)
    w1 = meta_ref[:, META_W:META_W + 1]
    w2 = meta_ref[:, META_W + 1:META_W + 2]
    h = h_ref[...] + (ybuf[slot, 0] * w1 + ybuf[slot, 1] * w2)
    ms = jnp.mean(h * h, axis=-1, keepdims=True)
    o_ref[...] = (h * lax.rsqrt(ms + EPS)) * g_ref[...]


def _combine(pos, yb, h2, meta, g, row0):
    n, d = h2.shape
    tm = min(COMBINE_TM, n)
    n_steps = n // tm
    assert row0 % tm == 0
    blk0 = row0 // tm
    grid_spec = pltpu.PrefetchScalarGridSpec(
        num_scalar_prefetch=1,
        grid=(n_steps,),
        in_specs=[
            pl.BlockSpec(memory_space=pl.ANY),
            pl.BlockSpec((tm, d), lambda i, pos: (i, 0)),
            pl.BlockSpec((tm, LANES), lambda i, pos: (blk0 + i, 0)),
            pl.BlockSpec((1, d), lambda i, pos: (0, 0)),
        ],
        out_specs=pl.BlockSpec((tm, d), lambda i, pos: (i, 0)),
        scratch_shapes=[
            pltpu.VMEM((2, TOP_K, tm, d), jnp.float32),
            pltpu.SemaphoreType.DMA((2,)),
        ],
    )
    return pl.pallas_call(
        functools.partial(_combine_kernel, tm=tm, n_steps=n_steps, blk0=blk0),
        grid_spec=grid_spec,
        out_shape=jax.ShapeDtypeStruct((n, d), jnp.float32),
        compiler_params=_gather_cparams(VMEM_LIMIT_BYTES),
    )(pos, yb, h2, meta, g.reshape(1, d))


def _dispatch_plan(meta, counts_f, n):
    bm = MOE_BM
    a = n * TOP_K
    expert = meta[:, META_E:META_E + TOP_K].astype(jnp.int32)
    rank = meta[:, META_RANK:META_RANK + TOP_K].astype(jnp.int32)
    counts = counts_f[0, :N_EXPERTS].astype(jnp.int32)
    padded = (counts + bm - 1) // bm * bm
    pad_end = jnp.cumsum(padded)
    pad_start = pad_end - padded
    pos = (pad_start[expert] + rank).reshape(a)
    n_blocks = -(-a // bm) + N_EXPERTS
    blk_start = (pad_start // bm).astype(jnp.int32)
    blk_count = (padded // bm).astype(jnp.int32)
    n_used = (pad_end[-1] // bm).astype(jnp.int32).reshape(1)
    return pos, blk_start, blk_count, counts, n_used, n_blocks


def _mixer_residual(x, p):
    b, s, d = x.shape
    n = b * s
    d_rnn = p["d_rnn"]
    d_attn = p["d_attn"]
    x2 = x.reshape(n, d)
    proj2 = _norm_proj(x2, p["norm_mix_g"], p["w_in"])
    d_in = proj2.shape[1]
    proj3 = proj2.reshape(b, s, d_in)
    hf = _lru_direction(proj3, p["conv_w"], p["conv_b"], p["lru_wg"][0], p["lru_b_a"][0], p["lru_b_x"][0],
                        p["lru_lambda"][0], None, reverse=False, d_rnn=d_rnn)
    ya = _lru_direction(proj3, p["conv_w"], p["conv_b"], p["lru_wg"][1], p["lru_b_a"][1], p["lru_b_x"][1],
                        p["lru_lambda"][1], hf, reverse=True, d_rnn=d_rnn)
    att = _attention(proj3, p["att_table"], d_rnn=d_rnn, d_attn=d_attn)
    m2 = _merge(ya.reshape(n, d_rnn), att.reshape(n, d_attn), p["w_lru_up"], p["w_attn_up"], proj2,
                p["b_merge"], gate_col=2 * d_rnn + 3 * d_attn)
    return _out_proj(m2, p["w_o"], x2)


def _moe_and_final_norm(h_a, h_b, p):
    n_a, n_b = h_a.shape[0], h_b.shape[0]
    xn2, meta, counts = _router(h_a, h_b, p["norm_ffn_g"], p["w_router_hi"], p["w_router_lo"], p["b_router"])
    pos, blk_start, blk_count, exp_rows, n_used, n_blocks = _dispatch_plan(meta, counts, n_a + n_b)
    yb = _experts(xn2, pos, blk_start, blk_count, exp_rows, n_used, p["w_exp_gate"], p["w_exp_up"], p["w_exp_down"],
                  n_blocks=n_blocks)
    y_a = _combine(pos, yb, h_a, meta, p["norm_final_g"], 0)
    y_b = _combine(pos, yb, h_b, meta, p["norm_final_g"], n_a)
    return y_a, y_b


def _prepare_params(norm_mix_g, w_in, conv_w, conv_b, lru_w_a, lru_b_a, lru_w_x, lru_b_x, lru_lambda, rpb,
                    w_lru_up, w_attn_up, b_merge, w_o, norm_ffn_g, w_router_group, b_router_group,
                    w_router_expert, b_router_expert, w_exp_gate, w_exp_up, w_exp_down, norm_final_g):
    bf16 = MXU_DTYPE
    d = w_in.shape[1]
    d_rnn = conv_w.shape[2]
    d_attn = w_attn_up.shape[1]
    lru_wg = jnp.concatenate([lru_w_a[0], lru_w_x[0]], axis=-1).astype(bf16)
    w_router = jnp.zeros((d, LANES), jnp.float32)
    w_router = w_router.at[:, :N_EXPERTS].set(w_router_expert[0])
    w_router = w_router.at[:, GROUP_LANE0:GROUP_LANE0 + N_GROUPS].set(w_router_group[0])
    w_router_hi = w_router.astype(bf16)
    w_router_lo = (w_router - w_router_hi.astype(jnp.float32)).astype(bf16)
    b_router = jnp.zeros((1, LANES), jnp.float32)
    b_router = b_router.at[0, :N_EXPERTS].set(b_router_expert[0])
    b_router = b_router.at[0, GROUP_LANE0:GROUP_LANE0 + N_GROUPS].set(b_router_group[0])
    return dict(
        d_rnn=d_rnn, d_attn=d_attn,
        norm_mix_g=norm_mix_g[0], w_in=w_in[0].astype(bf16),
        conv_w=conv_w[0], conv_b=conv_b[0], lru_wg=lru_wg,
        lru_b_a=lru_b_a[0], lru_b_x=lru_b_x[0], lru_lambda=lru_lambda[0],
        att_table=_attention_bias_table(rpb[0]),
        w_lru_up=w_lru_up[0].astype(bf16), w_attn_up=w_attn_up[0].astype(bf16),
        b_merge=b_merge[0], w_o=w_o[0].astype(bf16),
        norm_ffn_g=norm_ffn_g[0], w_router_hi=w_router_hi, w_router_lo=w_router_lo, b_router=b_router,
        w_exp_gate=w_exp_gate[0], w_exp_up=w_exp_up[0], w_exp_down=w_exp_down[0],
        norm_final_g=norm_final_g,
    )


def kernel(x_prompt, x_sample, norm_mix_g, w_in, conv_w, conv_b, lru_w_a, lru_b_a, lru_w_x, lru_b_x, lru_lambda, rpb, w_lru_up, w_attn_up, b_merge, w_o, norm_ffn_g, w_router_group, b_router_group, w_router_expert, b_router_expert, w_exp_gate, w_exp_up, w_exp_down, norm_final_g):
    assert norm_mix_g.shape[0] == 1, "single-layer encoder"
    p = _prepare_params(norm_mix_g, w_in, conv_w, conv_b, lru_w_a, lru_b_a, lru_w_x, lru_b_x, lru_lambda, rpb,
                        w_lru_up, w_attn_up, b_merge, w_o, norm_ffn_g, w_router_group, b_router_group,
                        w_router_expert, b_router_expert, w_exp_gate, w_exp_up, w_exp_down, norm_final_g)
    h_prompt = _mixer_residual(x_prompt, p)
    h_sample = _mixer_residual(x_sample, p)
    y_prompt, y_sample = _moe_and_final_norm(h_prompt, h_sample, p)
    return (y_prompt.reshape(x_prompt.shape), y_sample.reshape(x_sample.shape))
```
